```python
import math
import jax, jax.numpy as jnp
from jax import lax
import numpy as np

D_MODEL = 1024
BATCH = 8
SEQ = 2048
DEPTH = 2
DEC_BATCH = 128
DEC_SEQ = 4
PAST_LEN = 16384
PAGE_SIZE = 128

HEAD_DIM = 64
A_WIDTH = D_MODEL // 2
N_HEADS = A_WIDTH // HEAD_DIM
DECAY_LORA = max(32, int(round(1.8 * D_MODEL ** 0.5 / 32)) * 32)
AAA_LORA = max(32, int(round(1.8 * D_MODEL ** 0.5 / 32)) * 32)
GATE_LORA = max(32, int(round(0.6 * D_MODEL ** 0.8 / 32)) * 32)
A_PROJ = 3 * A_WIDTH + DECAY_LORA + AAA_LORA + GATE_LORA
A_SPLITS = (A_WIDTH, 2 * A_WIDTH, 3 * A_WIDTH, 3 * A_WIDTH + DECAY_LORA, 3 * A_WIDTH + DECAY_LORA + AAA_LORA)
GN_EPS = 64e-5
POOL_WINDOWS = (2, 4, 8, 16)
N_POOL_GROUPS = len(POOL_WINDOWS)
B_WIDTH = D_MODEL // 2
POOL_GW = B_WIDTH // N_POOL_GROUPS
POOL_HIST = max(POOL_WINDOWS) - 1
P_TOTAL = A_PROJ + B_WIDTH
D_FF = 256 * ((8 * D_MODEL // 3 + 255) // 256)
N_SUB = 3
NORM_EPS = 1e-6

kernel_name = "rwkv7_pool_gated_hybrid_step"


def _rmsnorm(x, g):
    x32 = x.astype(jnp.float32)
    y = x32 * lax.rsqrt(jnp.mean(x32 * x32, axis=-1, keepdims=True) + NORM_EPS) * g.astype(jnp.float32)
    return y.astype(x.dtype)


def _modnorm(x, g, shift, scale):
    return _rmsnorm(x, g) * (1 + scale) + shift


def _swiglu(u, w_in, w_out):
    gate, up = jnp.split(u @ w_in, 2, axis=-1)
    return (jax.nn.silu(gate) * up) @ w_out


def _wkv_step(S, inp):
    r_t, w_t, k_t, v_t, a_t, b_t = inp
    sa = jnp.einsum('bhvk,bhk->bhv', S, a_t)
    S = S * w_t[:, :, None, :] + sa[..., None] * b_t[:, :, None, :] + v_t[..., None] * k_t[:, :, None, :]
    y = jnp.einsum('bhvk,bhk->bhv', S, r_t)
    return S, y


def _rwkv7(pa, shift0, wkv0, mu, w0, w2, a0, a2, g2, k_k, k_a, r_k, ln_w, ln_b):
    f32 = jnp.float32
    Bn, L, _ = pa.shape
    prev = jnp.concatenate([shift0.astype(pa.dtype), pa[:, :-1]], axis=1)
    xs = pa + (prev - pa) * mu
    r, k, v, wl, al, gl = jnp.split(xs, A_SPLITS, axis=-1)
    wlog = -jax.nn.softplus(-(w0 + jnp.tanh(wl) @ w2).astype(f32)) - 0.5
    decay = jnp.exp(-jnp.exp(wlog))
    a = jax.nn.sigmoid((a0 + al @ a2).astype(f32))
    g = jax.nn.sigmoid(gl) @ g2
    hs = lambda t: t.reshape(Bn, L, N_HEADS, HEAD_DIM)
    kk = hs((k * k_k).astype(f32))
    kk = kk / jnp.maximum(jnp.sqrt(jnp.sum(kk * kk, axis=-1, keepdims=True)), 1e-12)
    k = hs(k.astype(f32) * (1 + (a - 1) * k_a.astype(f32)))
    r = hs(r.astype(f32))
    v = hs(v.astype(f32))
    a = hs(a)
    decay = hs(decay)
    tm = lambda t: jnp.moveaxis(t, 1, 0)
    S_T, y = lax.scan(_wkv_step, wkv0.astype(f32),
                      (tm(r), tm(decay), tm(k), tm(v), tm(-kk), tm(kk * a)))
    y = jnp.moveaxis(y, 0, 1)
    mean = jnp.mean(y, axis=-1, keepdims=True)
    var = jnp.mean(jnp.square(y - mean), axis=-1, keepdims=True)
    y = ((y - mean) * lax.rsqrt(var + GN_EPS)).reshape(Bn, L, A_WIDTH)
    y = y * ln_w.astype(f32) + ln_b.astype(f32)
    bonus = jnp.sum(r * k * r_k.astype(f32), axis=-1, keepdims=True) * v
    y = (y + bonus.reshape(Bn, L, A_WIDTH)) * g.astype(f32)
    return y.astype(pa.dtype), S_T, pa[:, -1:]


def _pool(pb, hist, pos, w_pool, pool_scale):
    Bn, L, _ = pb.shape
    xp = jnp.concatenate([hist.astype(pb.dtype), pb], axis=1)
    cs = jnp.cumsum(xp.astype(jnp.float32), axis=1)
    cs = jnp.concatenate([jnp.zeros((Bn, 1, B_WIDTH), jnp.float32), cs], axis=1)
    outs = []
    for gi, w in enumerate(POOL_WINDOWS):
        lo, hi = gi * POOL_GW, (gi + 1) * POOL_GW
        s = cs[:, POOL_HIST + 1:POOL_HIST + 1 + L, lo:hi] - cs[:, POOL_HIST + 1 - w:POOL_HIST + 1 - w + L, lo:hi]
        cnt = jnp.minimum(w, pos + 1).astype(jnp.float32)
        outs.append(s / cnt[None, :, None])
    pooled = jnp.stack(outs, axis=2)
    d = pooled - pb.reshape(Bn, L, N_POOL_GROUPS, POOL_GW).astype(jnp.float32)
    y = jnp.einsum('blgc,gcd->blgd', d.astype(pb.dtype), w_pool).reshape(Bn, L, B_WIDTH)
    return y * pool_scale, xp[:, -POOL_HIST:]


def _trunk(x, c, pos, wkv_in, shift_in, pool_in, W):
    (norm_g, w_mod, b_mod, w_ffn_in, w_ffn_out, w_in, mu_shift, w0, w2, a0, a2, g2, k_k, k_a, r_k,
     ln_x_w, ln_x_b, w_pool, pool_scale, w_br_a, w_br_b, w_gate, b_gate, w_out, final_g) = W
    Bn = x.shape[0]
    new_wkv, new_shift, new_pool = [], [], []
    for l in range(DEPTH):
        mod = (jax.nn.silu(c) @ w_mod[l] + b_mod[l]).reshape(Bn, 3 * N_SUB, D_MODEL)[:, None]
        u = _modnorm(x, norm_g[l, 0], mod[:, :, 0], mod[:, :, 1])
        x = x + 0.5 * mod[:, :, 2] * _swiglu(u, w_ffn_in[l, 0], w_ffn_out[l, 0])
        u = _modnorm(x, norm_g[l, 1], mod[:, :, 3], mod[:, :, 4])
        p = u @ w_in[l]
        ya, S_T, sh = _rwkv7(p[..., :A_PROJ], shift_in[l], wkv_in[l], mu_shift[l], w0[l], w2[l], a0[l],
                             a2[l], g2[l], k_k[l], k_a[l], r_k[l], ln_x_w[l], ln_x_b[l])
        yb, ph = _pool(p[..., A_PROJ:], pool_in[l], pos, w_pool[l], pool_scale[l])
        gates = jax.nn.sigmoid((u @ w_gate[l] + b_gate[l]).astype(jnp.float32)).astype(x.dtype)
        merged = gates[..., :D_MODEL] * (ya @ w_br_a[l]) + gates[..., D_MODEL:] * (yb @ w_br_b[l])
        x = x + mod[:, :, 5] * (merged @ w_out[l])
        u = _modnorm(x, norm_g[l, 2], mod[:, :, 6], mod[:, :, 7])
        x = x + 0.5 * mod[:, :, 8] * _swiglu(u, w_ffn_in[l, 1], w_ffn_out[l, 1])
        new_wkv.append(S_T.astype(x.dtype))
        new_shift.append(sh)
        new_pool.append(ph)
    y = _rmsnorm(x, final_g)
    return y, jnp.stack(new_wkv), jnp.stack(new_shift), jnp.stack(new_pool)


def setup_inputs(seed: int = 0) -> dict:
    key = jax.random.key(seed)
    ks = iter(jax.random.split(key, 48))

    def nrm(shape, scale):
        return jax.random.normal(next(ks), shape, jnp.float32) * scale

    D = D_MODEL
    return {
        "x_prompt": nrm((BATCH, SEQ, D), 1.0),
        "x_sample": nrm((DEC_BATCH, DEC_SEQ, D), 1.0),
        "state_wkv": nrm((DEPTH, DEC_BATCH, N_HEADS, HEAD_DIM, HEAD_DIM), 0.3),
        "state_shift": nrm((DEPTH, DEC_BATCH, 1, A_PROJ), 1.0),
        "state_pool": nrm((DEPTH, DEC_BATCH, POOL_HIST, B_WIDTH), 1.0),
        "c_prompt": nrm((BATCH, D), 1.0),
        "c_sample": nrm((DEC_BATCH, D), 1.0),
        "norm_g": 1.0 + nrm((DEPTH, N_SUB, D), 0.05),
        "w_mod": nrm((DEPTH, D, 3 * N_SUB * D), 0.5 * D ** -0.5),
        "b_mod": nrm((DEPTH, 3 * N_SUB * D), 0.02),
        "w_ffn_in": nrm((DEPTH, 2, D, 2 * D_FF), D ** -0.5),
        "w_ffn_out": nrm((DEPTH, 2, D_FF, D), D_FF ** -0.5),
        "w_in": nrm((DEPTH, D, P_TOTAL), D ** -0.5),
        "mu_shift": jax.random.uniform(next(ks), (DEPTH, A_PROJ), jnp.float32),
        "w0": -1.0 + nrm((DEPTH, A_WIDTH), 1.0),
        "w2": nrm((DEPTH, DECAY_LORA, A_WIDTH), DECAY_LORA ** -0.5),
        "a0": nrm((DEPTH, A_WIDTH), 0.5),
        "a2": nrm((DEPTH, AAA_LORA, A_WIDTH), AAA_LORA ** -0.5),
        "g2": nrm((DEPTH, GATE_LORA, A_WIDTH), GATE_LORA ** -0.5),
        "k_k": 0.85 + nrm((DEPTH, A_WIDTH), 0.05),
        "k_a": 1.0 + nrm((DEPTH, A_WIDTH), 0.05),
        "r_k": nrm((DEPTH, N_HEADS, HEAD_DIM), 0.1),
        "ln_x_w": 1.0 + nrm((DEPTH, A_WIDTH), 0.05),
        "ln_x_b": nrm((DEPTH, A_WIDTH), 0.02),
        "w_pool": nrm((DEPTH, N_POOL_GROUPS, POOL_GW, POOL_GW), POOL_GW ** -0.5),
        "pool_scale": 1.0 + nrm((DEPTH, B_WIDTH), 0.1),
        "w_br_a": nrm((DEPTH, A_WIDTH, D), A_WIDTH ** -0.5),
        "w_br_b": nrm((DEPTH, B_WIDTH, D), B_WIDTH ** -0.5),
        "w_gate": nrm((DEPTH, D, 2 * D), D ** -0.5),
        "b_gate": nrm((DEPTH, 2 * D), 0.02),
        "w_out": nrm((DEPTH, D, D), D ** -0.5),
        "final_g": 1.0 + nrm((D,), 0.05),
    }


def reference(x_prompt, x_sample, state_wkv, state_shift, state_pool, c_prompt, c_sample,
              norm_g, w_mod, b_mod, w_ffn_in, w_ffn_out, w_in, mu_shift, w0, w2, a0, a2, g2,
              k_k, k_a, r_k, ln_x_w, ln_x_b, w_pool, pool_scale, w_br_a, w_br_b, w_gate, b_gate,
              w_out, final_g):
    W = (norm_g, w_mod, b_mod, w_ffn_in, w_ffn_out, w_in, mu_shift, w0, w2, a0, a2, g2, k_k, k_a, r_k,
         ln_x_w, ln_x_b, w_pool, pool_scale, w_br_a, w_br_b, w_gate, b_gate, w_out, final_g)
    bp = x_prompt.shape[0]
    wkv0 = jnp.zeros((DEPTH, bp, N_HEADS, HEAD_DIM, HEAD_DIM), jnp.float32)
    shift0 = jnp.zeros((DEPTH, bp, 1, A_PROJ), x_prompt.dtype)
    pool0 = jnp.zeros((DEPTH, bp, POOL_HIST, B_WIDTH), x_prompt.dtype)
    pos_p = jnp.arange(x_prompt.shape[1], dtype=jnp.int32)
    pos_s = PAST_LEN + jnp.arange(x_sample.shape[1], dtype=jnp.int32)
    y_prompt, wkv_p, shift_p, pool_p = _trunk(x_prompt, c_prompt, pos_p, wkv0, shift0, pool0, W)
    y_sample, wkv_s, shift_s, pool_s = _trunk(x_sample, c_sample, pos_s, state_wkv, state_shift, state_pool, W)
    return (y_prompt, y_sample, wkv_p, shift_p, pool_p, wkv_s, shift_s, pool_s)
```

```python
import functools
import math

import jax
import jax.numpy as jnp
from jax import lax
from jax.experimental import pallas as pl
from jax.experimental.pallas import tpu as pltpu

F32 = jnp.float32
BF16 = jnp.bfloat16

HEAD_DIM = 64
POOL_WINDOWS = (2, 4, 8, 16)
POOL_HIST = max(POOL_WINDOWS) - 1
N_SUB = 3
NORM_EPS = 1e-6
GN_EPS = 64e-5
PAST_LEN = 16384
DECAY_SCALE = math.exp(-0.5)

LANES = 128
WKV_CHUNK = 64
VMEM_LIMIT_BYTES = 56 * 1024 * 1024


def _cparams(n_axes):
    return pltpu.CompilerParams(dimension_semantics=("arbitrary",) * n_axes,
                                vmem_limit_bytes=VMEM_LIMIT_BYTES)


def _sigmoid(x):
    return 1.0 / (1.0 + jnp.exp(-x))


def _dot(a, b):
    return jnp.dot(a.astype(BF16), b.astype(BF16), preferred_element_type=F32)


def _dot_nt(a, b):
    return lax.dot_general(a.astype(BF16), b.astype(BF16), (((1,), (1,)), ((), ())),
                           preferred_element_type=F32)


def _dot_tn(a, b):
    return lax.dot_general(a.astype(BF16), b.astype(BF16), (((0,), (0,)), ((), ())),
                           preferred_element_type=F32)


def _dot_hilo(a, b_exact):
    hi = a.astype(BF16)
    lo = (a - hi.astype(F32)).astype(BF16)
    return (jnp.dot(hi, b_exact, preferred_element_type=F32)
            + jnp.dot(lo, b_exact, preferred_element_type=F32))


def _rms(x):
    return x * lax.rsqrt(jnp.mean(x * x, axis=-1, keepdims=True) + NORM_EPS)


def _rows(ref, tm):
    m = ref[...]
    if m.shape[0] in (1, tm):
        return m
    return jnp.concatenate([m] * (tm // m.shape[0]), axis=0)


def _modnorm(x, g, shift, scale):
    return _rms(x) * g * (1.0 + scale) + shift


def _mod_kernel(c_ref, w_ref, b_ref, o_ref):
    c = c_ref[...]
    o_ref[...] = _dot(c * _sigmoid(c), w_ref[...]) + b_ref[...]


def _mod_call(c_all, w_mod, b_mod):
    depth, d, n9 = w_mod.shape
    rows = c_all.shape[0]
    n_sub9 = n9 // d
    tn = d
    return pl.pallas_call(
        _mod_kernel,
        grid=(depth, n_sub9),
        in_specs=[pl.BlockSpec((rows, d), lambda l, j: (0, 0)),
                  pl.BlockSpec((None, d, tn), lambda l, j: (l, 0, j)),
                  pl.BlockSpec((None, 1, tn), lambda l, j: (l, 0, j))],
        out_specs=pl.BlockSpec((None, None, rows, tn), lambda l, j: (l, j, 0, 0)),
        out_shape=jax.ShapeDtypeStruct((depth, n_sub9, rows, d), F32),
        compiler_params=_cparams(2),
        name="mod",
    )(c_all, w_mod, b_mod.reshape(depth, 1, n9))


class _Group:
    def __init__(self, kind, batch, seq, tm, mod_row0):
        self.kind, self.batch, self.seq, self.tm, self.mod_row0 = kind, batch, seq, tm, mod_row0
        self.rows = batch * seq
        assert self.rows % tm == 0
        if kind == "prompt":
            assert seq % tm == 0

    def mod_spec(self, d, layer, idx):
        if self.kind == "prompt":
            per_seq = self.seq // self.tm
            return pl.BlockSpec((None, None, None, 1, d),
                                lambda i, *_: (layer, idx, self.mod_row0 + i // per_seq, 0, 0))
        return pl.BlockSpec((None, None, self.batch, d), lambda i, *_: (layer, idx, 0, 0))

    def mod_arg(self, mod):
        if self.kind == "prompt":
            return mod.reshape(mod.shape[:3] + (1, mod.shape[3]))
        return mod


def _ffn_kernel(x_ref, sh_ref, sc_ref, gt_ref, g_ref, wg_ref, wu_ref, wo_ref, *rest, n_f, final):
    if final:
        fg_ref, o_ref, u_sc, acc_sc = rest
    else:
        o_ref, u_sc, acc_sc = rest
    tm = x_ref.shape[0]
    j = pl.program_id(1)

    @pl.when(j == 0)
    def _():
        u_sc[...] = _modnorm(x_ref[...], g_ref[...], _rows(sh_ref, tm), _rows(sc_ref, tm)).astype(BF16)
        acc_sc[...] = jnp.zeros_like(acc_sc)

    u = u_sc[...]
    hg = jnp.dot(u, wg_ref[...], preferred_element_type=F32)
    hu = jnp.dot(u, wu_ref[...], preferred_element_type=F32)
    act = hg * _sigmoid(hg) * hu
    acc_sc[...] += jnp.dot(act.astype(BF16), wo_ref[...], preferred_element_type=F32)

    @pl.when(j == n_f - 1)
    def _():
        y = x_ref[...] + 0.5 * _rows(gt_ref, tm) * acc_sc[...]
        if final:
            y = _rms(y) * fg_ref[...]
        o_ref[...] = y


def _ffn_call(x, grp, mod, layer, sub, norm_g, w_in, w_out, final_g=None):
    rows, d = x.shape
    d_ff = w_out.shape[0]
    n_f = 2
    tf = d_ff // n_f
    assert tf % LANES == 0
    tm = grp.tm
    final = final_g is not None
    row = lambda i, j: (i, 0)
    in_specs = [pl.BlockSpec((tm, d), row),
                grp.mod_spec(d, layer, 3 * sub), grp.mod_spec(d, layer, 3 * sub + 1),
                grp.mod_spec(d, layer, 3 * sub + 2),
                pl.BlockSpec((1, d), lambda i, j: (0, 0)),
                pl.BlockSpec((d, tf), lambda i, j: (0, j)),
                pl.BlockSpec((d, tf), lambda i, j: (0, n_f + j)),
                pl.BlockSpec((tf, d), lambda i, j: (j, 0))]
    m = grp.mod_arg(mod)
    args = [x, m, m, m, norm_g.reshape(1, d), w_in, w_in, w_out]
    if final:
        in_specs.append(pl.BlockSpec((1, d), lambda i, j: (0, 0)))
        args.append(final_g.reshape(1, d))
    return pl.pallas_call(
        functools.partial(_ffn_kernel, n_f=n_f, final=final),
        grid=(rows // tm, n_f),
        in_specs=in_specs,
        out_specs=pl.BlockSpec((tm, d), row),
        out_shape=jax.ShapeDtypeStruct((rows, d), F32),
        scratch_shapes=[pltpu.VMEM((tm, d), BF16), pltpu.VMEM((tm, d), F32)],
        compiler_params=_cparams(2),
        name=f"ffn_{grp.kind}",
    )(*args)


def _mixin_kernel(x_ref, sh_ref, sc_ref, g_ref, wa_ref, wb_ref, pa_ref, pb_ref):
    tm = x_ref.shape[0]
    u = _modnorm(x_ref[...], g_ref[...], _rows(sh_ref, tm), _rows(sc_ref, tm)).astype(BF16)
    pa_ref[...] = jnp.dot(u, wa_ref[...], preferred_element_type=F32)
    pb_ref[...] = jnp.dot(u, wb_ref[...], preferred_element_type=F32)


def _mixin_call(x, grp, mod, layer, norm_g, w_a, w_b):
    rows, d = x.shape
    tm = grp.tm
    na, nb = w_a.shape[1], w_b.shape[1]
    row = lambda i: (i, 0)
    const = lambda i: (0, 0)
    m = grp.mod_arg(mod)
    return pl.pallas_call(
        _mixin_kernel,
        grid=(rows // tm,),
        in_specs=[pl.BlockSpec((tm, d), row), grp.mod_spec(d, layer, 3), grp.mod_spec(d, layer, 4),
                  pl.BlockSpec((1, d), const), pl.BlockSpec((d, na), const), pl.BlockSpec((d, nb), const)],
        out_specs=[pl.BlockSpec((tm, na), row), pl.BlockSpec((tm, nb), row)],
        out_shape=[jax.ShapeDtypeStruct((rows, na), F32), jax.ShapeDtypeStruct((rows, nb), F32)],
        compiler_params=_cparams(1),
        name=f"mixin_{grp.kind}",
    )(x, m, m, norm_g.reshape(1, d), w_a, w_b)


def _mixout_kernel(x_ref, sh_ref, sc_ref, gt_ref, g_ref, ya_ref, yb_ref, wgate_ref, bgate_ref,
                   wbra_ref, wbrb_ref, wout_ref, o_ref):
    tm, d = x_ref.shape
    x = x_ref[...]
    u = _modnorm(x, g_ref[...], _rows(sh_ref, tm), _rows(sc_ref, tm)).astype(BF16)
    gates = _sigmoid(jnp.dot(u, wgate_ref[...], preferred_element_type=F32) + bgate_ref[...])
    ma = _dot(ya_ref[...], wbra_ref[...])
    mb = _dot(yb_ref[...], wbrb_ref[...])
    merged = gates[:, :d] * ma + gates[:, d:] * mb
    o_ref[...] = x + _rows(gt_ref, tm) * _dot(merged, wout_ref[...])


def _mixout_call(x, grp, mod, layer, norm_g, ya, yb, w_gate, b_gate, w_br_a, w_br_b, w_out):
    rows, d = x.shape
    tm = grp.tm
    wa, wb = ya.shape[1], yb.shape[1]
    row = lambda i: (i, 0)
    const = lambda i: (0, 0)
    m = grp.mod_arg(mod)
    return pl.pallas_call(
        _mixout_kernel,
        grid=(rows // tm,),
        in_specs=[pl.BlockSpec((tm, d), row), grp.mod_spec(d, layer, 3), grp.mod_spec(d, layer, 4),
                  grp.mod_spec(d, layer, 5), pl.BlockSpec((1, d), const),
                  pl.BlockSpec((tm, wa), row), pl.BlockSpec((tm, wb), row),
                  pl.BlockSpec((d, 2 * d), const), pl.BlockSpec((1, 2 * d), const),
                  pl.BlockSpec((wa, d), const), pl.BlockSpec((wb, d), const), pl.BlockSpec((d, d), const)],
        out_specs=pl.BlockSpec((tm, d), row),
        out_shape=jax.ShapeDtypeStruct((rows, d), F32),
        compiler_params=_cparams(1),
        name=f"mixout_{grp.kind}",
    )(x, m, m, m, norm_g.reshape(1, d), ya, yb, w_gate, b_gate.reshape(1, 2 * d), w_br_a, w_br_b, w_out)


def _rwkv_prep(xs, aw, seg_ones, w0, w2, a0, a2, g2, k_k, k_a):
    n_dec, n_aaa = w2.shape[0], a2.shape[0]
    r = xs[:, 0:aw]
    k = xs[:, aw:2 * aw]
    v = xs[:, 2 * aw:3 * aw]
    o = 3 * aw
    wl = xs[:, o:o + n_dec]
    al = xs[:, o + n_dec:o + n_dec + n_aaa]
    gl = xs[:, o + n_dec + n_aaa:]
    logw = -DECAY_SCALE * _sigmoid(w0 + _dot(jnp.tanh(wl), w2))
    asig = _sigmoid(a0 + _dot(al, a2))
    g = _dot(_sigmoid(gl), g2)
    kk = k * k_k
    kk = kk / jnp.maximum(jnp.sqrt(_dot_hilo(kk * kk, seg_ones)), 1e-12)
    k = k * (1.0 + (asig - 1.0) * k_a)
    return r, k, v, logw, -kk, kk * asig, g


def _pool_group_sum(read_rows, w):
    s = read_rows(0)
    for j in range(1, w):
        s = s + read_rows(j)
    return s


def _prompt_mix_kernel(pa_ref, pb_ref, mu_ref, w0_ref, w2_ref, a0_ref, a2_ref, g2_ref, kk_ref, ka_ref, rk_ref,
                       lnw_ref, lnb_ref, wpool_ref, pscale_ref, seg_ref,
                       ya_ref, yb_ref, wkv_ref,
                       pbuf, xbuf, s_sc, r_sc, k_sc, v_sc, a_sc, b_sc, lw_sc, g_sc, y_sc, *, n_blocks):
    tb, aw = ya_ref.shape
    bw = yb_ref.shape[1]
    t = WKV_CHUNK
    n_pairs = aw // LANES
    blk = pl.program_id(1)

    @pl.when(blk == 0)
    def _():
        pbuf[0:8, :] = jnp.zeros((8, pbuf.shape[1]), F32)
        xbuf[0:16, :] = jnp.zeros((16, bw), F32)
        s_sc[...] = jnp.zeros_like(s_sc)

    pa = pa_ref[...]
    pbuf[8:8 + tb, :] = pa
    prev = pbuf[7:7 + tb, :]
    xs = pa + (prev - pa) * mu_ref[...]
    pbuf[0:8, :] = pbuf[tb:tb + 8, :]

    seg_ones = seg_ref[...]
    r, k, v, logw, a, b, g = _rwkv_prep(xs, aw, seg_ones, w0_ref[...], w2_ref[...], a0_ref[...], a2_ref[...],
                                        g2_ref[...], kk_ref[...], ka_ref[...])
    r_sc[...] = r
    k_sc[...] = k
    v_sc[...] = v
    a_sc[...] = a
    b_sc[...] = b
    lw_sc[...] = logw
    g_sc[...] = g

    ri = lax.broadcasted_iota(jnp.int32, (t, t), 0)
    ci = lax.broadcasted_iota(jnp.int32, (t, t), 1)
    tri_incl = (ri >= ci).astype(BF16)
    strict_tt = ri > ci
    ri2 = lax.broadcasted_iota(jnp.int32, (t, 2 * t), 0)
    ci2 = lax.broadcasted_iota(jnp.int32, (t, 2 * t), 1)
    ak_strict = (ci2 >= t) & ((ci2 - t) < ri2)
    bk_incl = jnp.where(ci2 >= t, ci2 - t, ci2) <= ri2
    lane = lax.broadcasted_iota(jnp.int32, (1, LANES), 1)
    head_mask = (lane < HEAD_DIM, lane >= HEAD_DIM)
    rb = lax.broadcasted_iota(jnp.int32, (LANES, LANES), 0)
    cb = lax.broadcasted_iota(jnp.int32, (LANES, LANES), 1)
    blockdiag = (rb < HEAD_DIM) == (cb < HEAD_DIM)

    def chunk(ci_, carry):
        sl = pl.ds(pl.multiple_of(ci_ * t, t), t)
        lw = lw_sc[sl, :]
        hi = lw.astype(BF16)
        lo = (lw - hi.astype(F32)).astype(BF16)
        c = (jnp.dot(tri_incl, hi, preferred_element_type=F32)
             + jnp.dot(tri_incl, lo, preferred_element_type=F32))
        e_in = jnp.exp(c)
        e_prev = jnp.exp(c - lw)
        e_neg = jnp.exp(-c)
        p_end = e_in[t - 1:t, :]
        at = a_sc[sl, :] * e_prev
        rt = r_sc[sl, :] * e_in
        bt = b_sc[sl, :] * e_neg
        kt = k_sc[sl, :] * e_neg
        vv = v_sc[sl, :]
        for p in range(n_pairs):
            cols = slice(p * LANES, (p + 1) * LANES)
            at_p, rt_p, bt_p, kt_p, v_p = at[:, cols], rt[:, cols], bt[:, cols], kt[:, cols], vv[:, cols]
            pe_p = p_end[:, cols]
            lhs = jnp.concatenate([at_p, rt_p], axis=0)
            bk = jnp.concatenate([bt_p, kt_p], axis=0).astype(BF16)
            s_p = s_sc[p]
            s_bf = s_p.astype(BF16)
            vv2 = jnp.concatenate([v_p, v_p], axis=0).astype(BF16)
            xm, ga, gr = [], [], []
            for h in range(2):
                gmat = _dot_nt(jnp.where(head_mask[h], lhs, 0.0), bk)
                gtop, gbot = gmat[:t], gmat[t:]
                n = jnp.where(strict_tt, gtop[:, :t], 0.0)
                x = n
                pw = n
                for _ in range(int(math.log2(t)) - 1):
                    pw = _dot(pw, pw)
                    x = x + pw + _dot(x, pw)
                xm.append(x)
                ga.append(jnp.where(ak_strict, gtop, 0.0).astype(BF16))
                gr.append(jnp.where(bk_incl, gbot, 0.0).astype(BF16))
            sel = head_mask[0]
            rhs = _dot_nt(at_p, s_bf) + jnp.where(sel, jnp.dot(ga[0], vv2, preferred_element_type=F32),
                                                  jnp.dot(ga[1], vv2, preferred_element_type=F32))
            u = rhs + jnp.where(sel, _dot(xm[0], rhs), _dot(xm[1], rhs))
            uv = jnp.concatenate([u, v_p], axis=0).astype(BF16)
            y = _dot_nt(rt_p, s_bf) + jnp.where(sel, jnp.dot(gr[0], uv, preferred_element_type=F32),
                                                jnp.dot(gr[1], uv, preferred_element_type=F32))
            bkh = jnp.concatenate([bt_p * pe_p, kt_p * pe_p], axis=0)
            s_sc[p] = s_p * pe_p + jnp.where(blockdiag, _dot_tn(uv, bkh), 0.0)
            y_sc[sl, cols] = y
        return carry

    lax.fori_loop(0, tb // t, chunk, 0)

    y = y_sc[...]
    inv_n = 1.0 / HEAD_DIM
    mean = _dot_hilo(y, seg_ones) * inv_n
    yc = y - mean
    var = _dot_hilo(yc * yc, seg_ones) * inv_n
    yn = yc * lax.rsqrt(var + GN_EPS) * lnw_ref[...] + lnb_ref[...]
    bonus = _dot_hilo(r_sc[...] * k_sc[...] * rk_ref[...], seg_ones) * v_sc[...]
    ya_ref[...] = (yn + bonus) * g_sc[...]

    pb = pb_ref[...]
    xbuf[16:16 + tb, :] = pb
    pos = blk * tb + lax.broadcasted_iota(jnp.int32, (tb, 1), 0)
    gw = bw // len(POOL_WINDOWS)
    for gi, w in enumerate(POOL_WINDOWS):
        cols = slice(gi * gw, (gi + 1) * gw)
        s = _pool_group_sum(lambda j: xbuf[16 - j:16 - j + tb, cols], w)
        cnt = jnp.minimum(w, pos + 1).astype(F32)
        dlt = s / cnt - pb[:, cols]
        yb_ref[:, cols] = _dot(dlt, wpool_ref[gi]) * pscale_ref[:, cols]
    xbuf[0:16, :] = xbuf[tb:tb + 16, :]

    @pl.when(blk == n_blocks - 1)
    def _():
        for p in range(n_pairs):
            s_p = s_sc[p]
            for h in range(2):
                o = h * HEAD_DIM
                wkv_ref[2 * p + h] = s_p[o:o + HEAD_DIM, o:o + HEAD_DIM]


def _prompt_mix_call(pa, pb, batch, seq, prm):
    rows, na = pa.shape
    bw = pb.shape[1]
    aw = prm["w0"].shape[1]
    n_heads = aw // HEAD_DIM
    tb = 256
    assert seq % tb == 0 and tb % WKV_CHUNK == 0 and aw % LANES == 0
    n_blocks = seq // tb
    row = lambda b, c: (b * n_blocks + c, 0)
    const2 = lambda b, c: (0, 0)
    const3 = lambda b, c: (0, 0, 0)
    full = lambda a: pl.BlockSpec(a.shape, const2 if a.ndim == 2 else const3)
    names = ("mu", "w0", "w2", "a0", "a2", "g2", "k_k", "k_a", "r_k", "ln_w", "ln_b", "w_pool", "pool_scale", "seg")
    params = [prm[n] for n in names]
    blk = lambda: pltpu.VMEM((tb, aw), F32)
    return pl.pallas_call(
        functools.partial(_prompt_mix_kernel, n_blocks=n_blocks),
        grid=(batch, n_blocks),
        in_specs=[pl.BlockSpec((tb, na), row), pl.BlockSpec((tb, bw), row)] + [full(a) for a in params],
        out_specs=[pl.BlockSpec((tb, aw), row), pl.BlockSpec((tb, bw), row),
                   pl.BlockSpec((None, n_heads, HEAD_DIM, HEAD_DIM), lambda b, c: (b, 0, 0, 0))],
        out_shape=[jax.ShapeDtypeStruct((rows, aw), F32), jax.ShapeDtypeStruct((rows, bw), F32),
                   jax.ShapeDtypeStruct((batch, n_heads, HEAD_DIM, HEAD_DIM), F32)],
        scratch_shapes=[pltpu.VMEM((tb + 8, na), F32), pltpu.VMEM((tb + 16, bw), F32),
                        pltpu.VMEM((aw // LANES, LANES, LANES), F32),
                        blk(), blk(), blk(), blk(), blk(), blk(), blk(), blk()],
        compiler_params=_cparams(2),
        name="mix_prompt",
    )(pa, pb, *params)


def _sample_prep_kernel(pa_ref, sh0_ref, pb_ref, hist_ref, mu_ref, w0_ref, w2_ref, a0_ref, a2_ref, g2_ref,
                        kk_ref, ka_ref, wpool_ref, pscale_ref, seg_ref,
                        r_ref, w_ref, k_ref, v_ref, a_ref, b_ref, g_ref, yb_ref, *, batch, seq):
    aw = r_ref.shape[1]
    bw = yb_ref.shape[1]
    pa = pa_ref[...]
    prev = jnp.concatenate([sh0_ref[...], pa[:(seq - 1) * batch]], axis=0)
    xs = pa + (prev - pa) * mu_ref[...]
    r, k, v, logw, a, b, g = _rwkv_prep(xs, aw, seg_ref[...], w0_ref[...], w2_ref[...], a0_ref[...], a2_ref[...],
                                        g2_ref[...], kk_ref[...], ka_ref[...])
    r_ref[...] = r
    w_ref[...] = jnp.exp(logw)
    k_ref[...] = k
    v_ref[...] = v
    a_ref[...] = a
    b_ref[...] = b
    g_ref[...] = g

    gw = bw // len(POOL_WINDOWS)

    def slab(i, cols):
        if i < POOL_HIST:
            return hist_ref[i * batch:(i + 1) * batch, cols]
        return pb_ref[(i - POOL_HIST) * batch:(i - POOL_HIST + 1) * batch, cols]

    for ti in range(seq):
        rows = slice(ti * batch, (ti + 1) * batch)
        for gi, w in enumerate(POOL_WINDOWS):
            cols = slice(gi * gw, (gi + 1) * gw)
            s = _pool_group_sum(lambda j: slab(POOL_HIST + ti - j, cols), w)
            cnt = float(min(w, PAST_LEN + ti + 1))
            dlt = s / cnt - pb_ref[rows, cols]
            yb_ref[rows, cols] = _dot(dlt, wpool_ref[gi]) * pscale_ref[:, cols]


def _sample_prep_call(pa, sh0, pb, hist, batch, seq, prm):
    rows = pa.shape[0]
    aw = prm["w0"].shape[1]
    bw = pb.shape[1]
    names = ("mu", "w0", "w2", "a0", "a2", "g2", "k_k", "k_a", "w_pool", "pool_scale", "seg")
    args = [pa, sh0, pb, hist] + [prm[n] for n in names]
    full = lambda a: pl.BlockSpec(a.shape, (lambda i: (0, 0)) if a.ndim == 2 else (lambda i: (0, 0, 0)))
    o = jax.ShapeDtypeStruct((rows, aw), F32)
    ob = jax.ShapeDtypeStruct((rows, bw), F32)
    return pl.pallas_call(
        functools.partial(_sample_prep_kernel, batch=batch, seq=seq),
        grid=(1,),
        in_specs=[full(a) for a in args],
        out_specs=[full(o)] * 7 + [full(ob)],
        out_shape=[o] * 7 + [ob],
        compiler_params=_cparams(1),
        name="prep_sample",
    )(*args)


def _sample_wkv_kernel(r_ref, w_ref, k_ref, v_ref, a_ref, b_ref, g_ref, rk_ref, lnw_ref, lnb_ref, s_ref,
                       y_ref, so_ref):
    seq, bb, n_heads, hd = r_ref.shape
    eye = (lax.broadcasted_iota(jnp.int32, (hd, hd), 0) == lax.broadcasted_iota(jnp.int32, (hd, hd), 1))
    rk, lnw, lnb = rk_ref[...], lnw_ref[...], lnb_ref[...]

    def per_seq(bi, carry):
        s = [s_ref[bi, h] for h in range(n_heads)]
        for ti in range(seq):
            rt, wt, kt, vt = r_ref[ti, bi], w_ref[ti, bi], k_ref[ti, bi], v_ref[ti, bi]
            at, bt = a_ref[ti, bi], b_ref[ti, bi]
            yrows = []
            for h in range(n_heads):
                hs = slice(h, h + 1)
                sa = jnp.sum(s[h] * at[hs], axis=1, keepdims=True)
                vcol = jnp.sum(jnp.where(eye, vt[hs], 0.0), axis=1, keepdims=True)
                s[h] = s[h] * wt[hs] + sa * bt[hs] + vcol * kt[hs]
                ycol = jnp.sum(s[h] * rt[hs], axis=1, keepdims=True)
                yrows.append(jnp.sum(jnp.where(eye, ycol, 0.0), axis=0, keepdims=True))
            y = jnp.concatenate(yrows, axis=0)
            yc = y - jnp.mean(y, axis=-1, keepdims=True)
            var = jnp.mean(yc * yc, axis=-1, keepdims=True)
            yn = yc * lax.rsqrt(var + GN_EPS) * lnw + lnb
            bonus = jnp.sum(rt * kt * rk, axis=-1, keepdims=True) * vt
            y_ref[ti, bi] = (yn + bonus) * g_ref[ti, bi]
        for h in range(n_heads):
            so_ref[bi, h] = s[h]
        return carry

    lax.fori_loop(0, bb, per_seq, 0)


def _sample_wkv_call(vecs, state, batch, seq, prm):
    n_heads = state.shape[1]
    bb = 8
    assert batch % bb == 0
    v4 = [a.reshape(seq, batch, n_heads, HEAD_DIM) for a in vecs]
    vspec = pl.BlockSpec((seq, bb, n_heads, HEAD_DIM), lambda i: (0, i, 0, 0))
    hspec = pl.BlockSpec((n_heads, HEAD_DIM), lambda i: (0, 0))
    sspec = pl.BlockSpec((bb, n_heads, HEAD_DIM, HEAD_DIM), lambda i: (i, 0, 0, 0))
    hp = [prm[n].reshape(n_heads, HEAD_DIM) for n in ("r_k", "ln_w", "ln_b")]
    y, s_out = pl.pallas_call(
        _sample_wkv_kernel,
        grid=(batch // bb,),
        in_specs=[vspec] * 7 + [hspec] * 3 + [sspec],
        out_specs=[vspec, sspec],
        out_shape=[jax.ShapeDtypeStruct(v4[0].shape, F32), jax.ShapeDtypeStruct(state.shape, F32)],
        compiler_params=_cparams(1),
        name="wkv_sample",
    )(*v4, *hp, state)
    return y.reshape(seq * batch, n_heads * HEAD_DIM), s_out


def kernel(x_prompt, x_sample, state_wkv, state_shift, state_pool, c_prompt, c_sample, norm_g, w_mod, b_mod, w_ffn_in, w_ffn_out, w_in, mu_shift, w0, w2, a0, a2, g2, k_k, k_a, r_k, ln_x_w, ln_x_b, w_pool, pool_scale, w_br_a, w_br_b, w_gate, b_gate, w_out, final_g):
    bp, lp, d = x_prompt.shape
    bs, ls, _ = x_sample.shape
    depth = w_mod.shape[0]
    aw = w0.shape[1]
    a_proj = mu_shift.shape[1]
    bw = pool_scale.shape[1]
    n_heads = aw // HEAD_DIM

    bf = lambda w: w.astype(BF16)
    w_ffn_in_b, w_ffn_out_b = bf(w_ffn_in), bf(w_ffn_out)
    w_in_a, w_in_b = bf(w_in[:, :, :a_proj]), bf(w_in[:, :, a_proj:])
    w_gate_b, w_br_a_b, w_br_b_b, w_out_b = bf(w_gate), bf(w_br_a), bf(w_br_b), bf(w_out)
    w2_b, a2_b, g2_b, w_pool_b = bf(w2), bf(a2), bf(g2), bf(w_pool)
    head_of = jnp.arange(aw, dtype=jnp.int32) // HEAD_DIM
    seg_ones = (head_of[:, None] == head_of[None, :]).astype(BF16)

    mod = _mod_call(jnp.concatenate([c_sample, c_prompt], axis=0), w_mod, b_mod)
    gp = _Group("prompt", bp, lp, 512, mod_row0=bs)
    gs = _Group("sample", bs, ls, bs * ls, mod_row0=0)

    xp = x_prompt.reshape(bp * lp, d)
    xs = jnp.transpose(x_sample, (1, 0, 2)).reshape(ls * bs, d)
    hist_s = jnp.transpose(state_pool, (0, 2, 1, 3)).reshape(depth, POOL_HIST * bs, bw)

    wkv_p, shift_p, pool_p, wkv_s, shift_s, pool_s = [], [], [], [], [], []
    for l in range(depth):
        prm = dict(mu=mu_shift[l][None], w0=w0[l][None], w2=w2_b[l], a0=a0[l][None], a2=a2_b[l], g2=g2_b[l],
                   k_k=k_k[l][None], k_a=k_a[l][None], r_k=r_k[l].reshape(1, aw), ln_w=ln_x_w[l][None],
                   ln_b=ln_x_b[l][None], w_pool=w_pool_b[l], pool_scale=pool_scale[l][None], seg=seg_ones)
        last = l == depth - 1

        xp = _ffn_call(xp, gp, mod, l, 0, norm_g[l, 0], w_ffn_in_b[l, 0], w_ffn_out_b[l, 0])
        pa, pb = _mixin_call(xp, gp, mod, l, norm_g[l, 1], w_in_a[l], w_in_b[l])
        ya, yb, s_new = _prompt_mix_call(pa, pb, bp, lp, prm)
        xp = _mixout_call(xp, gp, mod, l, norm_g[l, 1], ya, yb, w_gate_b[l], b_gate[l], w_br_a_b[l], w_br_b_b[l],
                          w_out_b[l])
        xp = _ffn_call(xp, gp, mod, l, 2, norm_g[l, 2], w_ffn_in_b[l, 1], w_ffn_out_b[l, 1],
                       final_g if last else None)
        wkv_p.append(s_new)
        shift_p.append(pa.reshape(bp, lp, a_proj)[:, -1:])
        pool_p.append(pb.reshape(bp, lp, bw)[:, -POOL_HIST:])

        xs = _ffn_call(xs, gs, mod, l, 0, norm_g[l, 0], w_ffn_in_b[l, 0], w_ffn_out_b[l, 0])
        pa, pb = _mixin_call(xs, gs, mod, l, norm_g[l, 1], w_in_a[l], w_in_b[l])
        *vecs, yb = _sample_prep_call(pa, state_shift[l, :, 0, :], pb, hist_s[l], bs, ls, prm)
        ya, s_new = _sample_wkv_call(vecs, state_wkv[l], bs, ls, prm)
        xs = _mixout_call(xs, gs, mod, l, norm_g[l, 1], ya, yb, w_gate_b[l], b_gate[l], w_br_a_b[l], w_br_b_b[l],
                          w_out_b[l])
        xs = _ffn_call(xs, gs, mod, l, 2, norm_g[l, 2], w_ffn_in_b[l, 1], w_ffn_out_b[l, 1],
                       final_g if last else None)
        wkv_s.append(s_new)
        shift_s.append(pa.reshape(ls, bs, a_proj)[-1][:, None, :])
        pb_bm = jnp.transpose(pb.reshape(ls, bs, bw), (1, 0, 2))
        pool_s.append(jnp.concatenate([state_pool[l], pb_bm], axis=1)[:, -POOL_HIST:])

    y_prompt = xp.reshape(bp, lp, d)
    y_sample = jnp.transpose(xs.reshape(ls, bs, d), (1, 0, 2))
    return (y_prompt, y_sample, jnp.stack(wkv_p), jnp.stack(shift_p), jnp.stack(pool_p),
            jnp.stack(wkv_s), jnp.stack(shift_s), jnp.stack(pool_s))
```

```python
import functools
import math

import jax
import jax.numpy as jnp
from jax import lax
from jax.experimental import pallas as pl
from jax.experimental.pallas import tpu as pltpu

F32 = jnp.float32
BF16 = jnp.bfloat16

HEAD_DIM = 64
POOL_WINDOWS = (2, 4, 8, 16)
POOL_HIST = max(POOL_WINDOWS) - 1
N_SUB = 3
NORM_EPS = 1e-6
GN_EPS = 64e-5
PAST_LEN = 16384
DECAY_SCALE = math.exp(-0.5)

LANES = 128
WKV_CHUNK = 64
VMEM_LIMIT_BYTES = 56 * 1024 * 1024


def _cparams(n_axes):
    return pltpu.CompilerParams(dimension_semantics=("arbitrary",) * n_axes,
                                vmem_limit_bytes=VMEM_LIMIT_BYTES)


def _sigmoid(x):
    return 1.0 / (1.0 + jnp.exp(-x))


def _dot(a, b):
    return jnp.dot(a.astype(BF16), b.astype(BF16), preferred_element_type=F32)


def _dot_nt(a, b):
    return lax.dot_general(a.astype(BF16), b.astype(BF16), (((1,), (1,)), ((), ())),
                           preferred_element_type=F32)


def _dot_tn(a, b):
    return lax.dot_general(a.astype(BF16), b.astype(BF16), (((0,), (0,)), ((), ())),
                           preferred_element_type=F32)


def _dot_hilo(a, b_exact):
    hi = a.astype(BF16)
    lo = (a - hi.astype(F32)).astype(BF16)
    return (jnp.dot(hi, b_exact, preferred_element_type=F32)
            + jnp.dot(lo, b_exact, preferred_element_type=F32))


def _rms(x):
    return x * lax.rsqrt(jnp.mean(x * x, axis=-1, keepdims=True) + NORM_EPS)


def _rows(ref, tm):
    m = ref[...]
    if m.shape[0] in (1, tm):
        return m
    return jnp.concatenate([m] * (tm // m.shape[0]), axis=0)


def _modnorm(x, g, shift, scale):
    return _rms(x) * g * (1.0 + scale) + shift


def _mod_kernel(c_ref, w_ref, b_ref, o_ref):
    c = c_ref[...]
    o_ref[...] = _dot(c * _sigmoid(c), w_ref[...]) + b_ref[...]


def _mod_call(c_all, w_mod, b_mod):
    depth, d, n9 = w_mod.shape
    rows = c_all.shape[0]
    n_sub9 = n9 // d
    tn = d
    return pl.pallas_call(
        _mod_kernel,
        grid=(depth, n_sub9),
        in_specs=[pl.BlockSpec((rows, d), lambda l, j: (0, 0)),
                  pl.BlockSpec((None, d, tn), lambda l, j: (l, 0, j)),
                  pl.BlockSpec((None, 1, tn), lambda l, j: (l, 0, j))],
        out_specs=pl.BlockSpec((None, None, rows, tn), lambda l, j: (l, j, 0, 0)),
        out_shape=jax.ShapeDtypeStruct((depth, n_sub9, rows, d), F32),
        compiler_params=_cparams(2),
        name="mod",
    )(c_all, w_mod, b_mod.reshape(depth, 1, n9))


class _Group:
    def __init__(self, kind, batch, seq, tm, mod_row0):
        self.kind, self.batch, self.seq, self.tm, self.mod_row0 = kind, batch, seq, tm, mod_row0
        self.rows = batch * seq
        assert self.rows % tm == 0
        if kind == "prompt":
            assert seq % tm == 0

    def mod_spec(self, d, layer, idx):
        if self.kind == "prompt":
            per_seq = self.seq // self.tm
            return pl.BlockSpec((None, None, None, 1, d),
                                lambda i, *_: (layer, idx, self.mod_row0 + i // per_seq, 0, 0))
        return pl.BlockSpec((None, None, self.batch, d), lambda i, *_: (layer, idx, 0, 0))

    def mod_arg(self, mod):
        if self.kind == "prompt":
            return mod.reshape(mod.shape[:3] + (1, mod.shape[3]))
        return mod


def _ffn_kernel(x_ref, sh_ref, sc_ref, gt_ref, g_ref, wg_ref, wu_ref, wo_ref, *rest, n_f, final):
    if final:
        fg_ref, o_ref, u_sc, acc_sc = rest
    else:
        o_ref, u_sc, acc_sc = rest
    tm = x_ref.shape[0]
    j = pl.program_id(1)

    @pl.when(j == 0)
    def _():
        u_sc[...] = _modnorm(x_ref[...], g_ref[...], _rows(sh_ref, tm), _rows(sc_ref, tm)).astype(BF16)
        acc_sc[...] = jnp.zeros_like(acc_sc)

    u = u_sc[...]
    hg = jnp.dot(u, wg_ref[...], preferred_element_type=F32)
    hu = jnp.dot(u, wu_ref[...], preferred_element_type=F32)
    act = hg * _sigmoid(hg) * hu
    acc_sc[...] += jnp.dot(act.astype(BF16), wo_ref[...], preferred_element_type=F32)

    @pl.when(j == n_f - 1)
    def _():
        y = x_ref[...] + 0.5 * _rows(gt_ref, tm) * acc_sc[...]
        if final:
            y = _rms(y) * fg_ref[...]
        o_ref[...] = y


def _ffn_call(x, grp, mod, layer, sub, norm_g, w_in, w_out, final_g=None):
    rows, d = x.shape
    d_ff = w_out.shape[0]
    n_f = 2
    tf = d_ff // n_f
    assert tf % LANES == 0
    tm = grp.tm
    final = final_g is not None
    row = lambda i, j: (i, 0)
    in_specs = [pl.BlockSpec((tm, d), row),
                grp.mod_spec(d, layer, 3 * sub), grp.mod_spec(d, layer, 3 * sub + 1),
                grp.mod_spec(d, layer, 3 * sub + 2),
                pl.BlockSpec((1, d), lambda i, j: (0, 0)),
                pl.BlockSpec((d, tf), lambda i, j: (0, j)),
                pl.BlockSpec((d, tf), lambda i, j: (0, n_f + j)),
                pl.BlockSpec((tf, d), lambda i, j: (j, 0))]
    m = grp.mod_arg(mod)
    args = [x, m, m, m, norm_g.reshape(1, d), w_in, w_in, w_out]
    if final:
        in_specs.append(pl.BlockSpec((1, d), lambda i, j: (0, 0)))
        args.append(final_g.reshape(1, d))
    return pl.pallas_call(
        functools.partial(_ffn_kernel, n_f=n_f, final=final),
        grid=(rows // tm, n_f),
        in_specs=in_specs,
        out_specs=pl.BlockSpec((tm, d), row),
        out_shape=jax.ShapeDtypeStruct((rows, d), F32),
        scratch_shapes=[pltpu.VMEM((tm, d), BF16), pltpu.VMEM((tm, d), F32)],
        compiler_params=_cparams(2),
        name=f"ffn_{grp.kind}",
    )(*args)


def _mixin_kernel(x_ref, sh_ref, sc_ref, g_ref, wa_ref, wb_ref, pa_ref, pb_ref):
    tm = x_ref.shape[0]
    u = _modnorm(x_ref[...], g_ref[...], _rows(sh_ref, tm), _rows(sc_ref, tm)).astype(BF16)
    pa_ref[...] = jnp.dot(u, wa_ref[...], preferred_element_type=F32)
    pb_ref[...] = jnp.dot(u, wb_ref[...], preferred_element_type=F32)


def _mixin_call(x, grp, mod, layer, norm_g, w_a, w_b):
    rows, d = x.shape
    tm = grp.tm
    na, nb = w_a.shape[1], w_b.shape[1]
    row = lambda i: (i, 0)
    const = lambda i: (0, 0)
    m = grp.mod_arg(mod)
    return pl.pallas_call(
        _mixin_kernel,
        grid=(rows // tm,),
        in_specs=[pl.BlockSpec((tm, d), row), grp.mod_spec(d, layer, 3), grp.mod_spec(d, layer, 4),
                  pl.BlockSpec((1, d), const), pl.BlockSpec((d, na), const), pl.BlockSpec((d, nb), const)],
        out_specs=[pl.BlockSpec((tm, na), row), pl.BlockSpec((tm, nb), row)],
        out_shape=[jax.ShapeDtypeStruct((rows, na), F32), jax.ShapeDtypeStruct((rows, nb), F32)],
        compiler_params=_cparams(1),
        name=f"mixin_{grp.kind}",
    )(x, m, m, norm_g.reshape(1, d), w_a, w_b)


def _mixout_kernel(x_ref, sh_ref, sc_ref, gt_ref, g_ref, ya_ref, yb_ref, wgate_ref, bgate_ref,
                   wbra_ref, wbrb_ref, wout_ref, o_ref):
    tm, d = x_ref.shape
    x = x_ref[...]
    u = _modnorm(x, g_ref[...], _rows(sh_ref, tm), _rows(sc_ref, tm)).astype(BF16)
    gates = _sigmoid(jnp.dot(u, wgate_ref[...], preferred_element_type=F32) + bgate_ref[...])
    ma = _dot(ya_ref[...], wbra_ref[...])
    mb = _dot(yb_ref[...], wbrb_ref[...])
    merged = gates[:, :d] * ma + gates[:, d:] * mb
    o_ref[...] = x + _rows(gt_ref, tm) * _dot(merged, wout_ref[...])


def _mixout_call(x, grp, mod, layer, norm_g, ya, yb, w_gate, b_gate, w_br_a, w_br_b, w_out):
    rows, d = x.shape
    tm = grp.tm
    wa, wb = ya.shape[1], yb.shape[1]
    row = lambda i: (i, 0)
    const = lambda i: (0, 0)
    m = grp.mod_arg(mod)
    return pl.pallas_call(
        _mixout_kernel,
        grid=(rows // tm,),
        in_specs=[pl.BlockSpec((tm, d), row), grp.mod_spec(d, layer, 3), grp.mod_spec(d, layer, 4),
                  grp.mod_spec(d, layer, 5), pl.BlockSpec((1, d), const),
                  pl.BlockSpec((tm, wa), row), pl.BlockSpec((tm, wb), row),
                  pl.BlockSpec((d, 2 * d), const), pl.BlockSpec((1, 2 * d), const),
                  pl.BlockSpec((wa, d), const), pl.BlockSpec((wb, d), const), pl.BlockSpec((d, d), const)],
        out_specs=pl.BlockSpec((tm, d), row),
        out_shape=jax.ShapeDtypeStruct((rows, d), F32),
        compiler_params=_cparams(1),
        name=f"mixout_{grp.kind}",
    )(x, m, m, m, norm_g.reshape(1, d), ya, yb, w_gate, b_gate.reshape(1, 2 * d), w_br_a, w_br_b, w_out)


def _rwkv_prep(xs, aw, seg_ones, w0, w2, a0, a2, g2, k_k, k_a):
    n_dec, n_aaa = w2.shape[0], a2.shape[0]
    r = xs[:, 0:aw]
    k = xs[:, aw:2 * aw]
    v = xs[:, 2 * aw:3 * aw]
    o = 3 * aw
    wl = xs[:, o:o + n_dec]
    al = xs[:, o + n_dec:o + n_dec + n_aaa]
    gl = xs[:, o + n_dec + n_aaa:]
    logw = -DECAY_SCALE * _sigmoid(w0 + _dot(jnp.tanh(wl), w2))
    asig = _sigmoid(a0 + _dot(al, a2))
    g = _dot(_sigmoid(gl), g2)
    kk = k * k_k
    kk = kk / jnp.maximum(jnp.sqrt(_dot_hilo(kk * kk, seg_ones)), 1e-12)
    k = k * (1.0 + (asig - 1.0) * k_a)
    return r, k, v, logw, -kk, kk * asig, g


def _pool_group_sum(read_rows, w):
    s = read_rows(0)
    for j in range(1, w):
        s = s + read_rows(j)
    return s


def _prompt_mix_kernel(pa_ref, pb_ref, mu_ref, w0_ref, w2_ref, a0_ref, a2_ref, g2_ref, kk_ref, ka_ref, rk_ref,
                       lnw_ref, lnb_ref, wpool_ref, pscale_ref, seg_ref,
                       ya_ref, yb_ref, wkv_ref,
                       pbuf, xbuf, s_sc, r_sc, k_sc, v_sc, a_sc, b_sc, lw_sc, g_sc, y_sc, *, n_blocks):
    tb, aw = ya_ref.shape
    bw = yb_ref.shape[1]
    t = WKV_CHUNK
    n_pairs = aw // LANES
    blk = pl.program_id(1)

    @pl.when(blk == 0)
    def _():
        pbuf[0:8, :] = jnp.zeros((8, pbuf.shape[1]), F32)
        xbuf[0:16, :] = jnp.zeros((16, bw), F32)
        s_sc[...] = jnp.zeros_like(s_sc)

    pa = pa_ref[...]
    pbuf[8:8 + tb, :] = pa
    prev = pbuf[7:7 + tb, :]
    xs = pa + (prev - pa) * mu_ref[...]
    pbuf[0:8, :] = pbuf[tb:tb + 8, :]

    seg_ones = seg_ref[...]
    r, k, v, logw, a, b, g = _rwkv_prep(xs, aw, seg_ones, w0_ref[...], w2_ref[...], a0_ref[...], a2_ref[...],
                                        g2_ref[...], kk_ref[...], ka_ref[...])
    r_sc[...] = r
    k_sc[...] = k
    v_sc[...] = v
    a_sc[...] = a
    b_sc[...] = b
    lw_sc[...] = logw
    g_sc[...] = g

    t2 = 2 * t
    ri = lax.broadcasted_iota(jnp.int32, (t, t), 0)
    ci = lax.broadcasted_iota(jnp.int32, (t, t), 1)
    tri_incl = (ri >= ci).astype(BF16)
    rs = lax.broadcasted_iota(jnp.int32, (t2, t2), 0)
    cs = lax.broadcasted_iota(jnp.int32, (t2, t2), 1)
    strict = cs < rs
    rs2 = lax.broadcasted_iota(jnp.int32, (t2, 2 * t2), 0)
    cs2 = lax.broadcasted_iota(jnp.int32, (t2, 2 * t2), 1)
    incl2 = jnp.where(cs2 >= t2, cs2 - t2, cs2) <= rs2
    lane = lax.broadcasted_iota(jnp.int32, (1, LANES), 1)
    head_mask = (lane < HEAD_DIM, lane >= HEAD_DIM)
    pairs = range(n_pairs)
    n_levels = int(math.log2(t))

    def stack(x):
        return jnp.concatenate([jnp.where(head_mask[0], x, 0.0), jnp.where(head_mask[1], x, 0.0)], axis=0)

    def chunk(ci_, carry):
        sl = pl.ds(pl.multiple_of(ci_ * t, t), t)
        lw = lw_sc[sl, :]
        hi = lw.astype(BF16)
        lo = (lw - hi.astype(F32)).astype(BF16)
        c = (jnp.dot(tri_incl, hi, preferred_element_type=F32)
             + jnp.dot(tri_incl, lo, preferred_element_type=F32))
        e_in = jnp.exp(c)
        e_prev = jnp.exp(c - lw)
        e_neg = jnp.exp(-c)
        p_end = e_in[t - 1:t, :]
        at = a_sc[sl, :] * e_prev
        rt = r_sc[sl, :] * e_in
        bt = b_sc[sl, :] * e_neg
        kt = k_sc[sl, :] * e_neg
        vv = v_sc[sl, :]
        cols = [slice(p * LANES, (p + 1) * LANES) for p in pairs]
        pe = [p_end[:, cols[p]] for p in pairs]
        a2 = [stack(at[:, cols[p]]).astype(BF16) for p in pairs]
        r2 = [stack(rt[:, cols[p]]).astype(BF16) for p in pairs]
        v2b = [stack(vv[:, cols[p]]).astype(BF16) for p in pairs]
        bk = [jnp.concatenate([stack(bt[:, cols[p]]), stack(kt[:, cols[p]])], axis=0) for p in pairs]
        g = [_dot_nt(jnp.concatenate([a2[p], r2[p]], axis=0), bk[p]) for p in pairs]
        n = [jnp.where(strict, g[p][:t2, :t2], 0.0) for p in pairs]
        ak = [jnp.where(strict, g[p][:t2, t2:], 0.0).astype(BF16) for p in pairs]
        rbk = [jnp.where(incl2, g[p][t2:, :], 0.0).astype(BF16) for p in pairs]
        xm = n
        pw = [_dot(n[p], n[p]) for p in pairs]
        for _ in range(n_levels - 2):
            res = [_dot(jnp.concatenate([pw[p], xm[p]], axis=0), pw[p]) for p in pairs]
            xm = [xm[p] + pw[p] + res[p][t2:] for p in pairs]
            pw = [res[p][:t2] for p in pairs]
        res = [_dot(xm[p], pw[p]) for p in pairs]
        xm = [(xm[p] + pw[p] + res[p]).astype(BF16) for p in pairs]
        s_old = [s_sc[p] for p in pairs]
        s_bf = [s_old[p].astype(BF16) for p in pairs]
        rhs = [_dot_nt(a2[p], s_bf[p]) for p in pairs]
        akv = [jnp.dot(ak[p], v2b[p], preferred_element_type=F32) for p in pairs]
        rhs = [rhs[p] + akv[p] for p in pairs]
        u = [rhs[p] + _dot(xm[p], rhs[p]) for p in pairs]
        uv = [jnp.concatenate([u[p].astype(BF16), v2b[p]], axis=0) for p in pairs]
        ys = [_dot_nt(r2[p], s_bf[p]) for p in pairs]
        yi = [jnp.dot(rbk[p], uv[p], preferred_element_type=F32) for p in pairs]
        sn = [_dot_tn(uv[p], bk[p] * pe[p]) for p in pairs]
        for p in pairs:
            y2 = ys[p] + yi[p]
            y_sc[sl, cols[p]] = y2[:t] + y2[t:]
            s_sc[p] = s_old[p] * pe[p] + sn[p]
        return carry

    lax.fori_loop(0, tb // t, chunk, 0)

    y = y_sc[...]
    inv_n = 1.0 / HEAD_DIM
    mean = _dot_hilo(y, seg_ones) * inv_n
    yc = y - mean
    var = _dot_hilo(yc * yc, seg_ones) * inv_n
    yn = yc * lax.rsqrt(var + GN_EPS) * lnw_ref[...] + lnb_ref[...]
    bonus = _dot_hilo(r_sc[...] * k_sc[...] * rk_ref[...], seg_ones) * v_sc[...]
    ya_ref[...] = (yn + bonus) * g_sc[...]

    pb = pb_ref[...]
    xbuf[16:16 + tb, :] = pb
    pos = blk * tb + lax.broadcasted_iota(jnp.int32, (tb, 1), 0)
    gw = bw // len(POOL_WINDOWS)
    for gi, w in enumerate(POOL_WINDOWS):
        cols = slice(gi * gw, (gi + 1) * gw)
        s = _pool_group_sum(lambda j: xbuf[16 - j:16 - j + tb, cols], w)
        cnt = jnp.minimum(w, pos + 1).astype(F32)
        dlt = s / cnt - pb[:, cols]
        yb_ref[:, cols] = _dot(dlt, wpool_ref[gi]) * pscale_ref[:, cols]
    xbuf[0:16, :] = xbuf[tb:tb + 16, :]

    @pl.when(blk == n_blocks - 1)
    def _():
        for p in range(n_pairs):
            s_p = s_sc[p]
            for h in range(2):
                o = h * HEAD_DIM
                wkv_ref[2 * p + h] = s_p[o:o + HEAD_DIM, o:o + HEAD_DIM]


def _prompt_mix_call(pa, pb, batch, seq, prm):
    rows, na = pa.shape
    bw = pb.shape[1]
    aw = prm["w0"].shape[1]
    n_heads = aw // HEAD_DIM
    tb = 256
    assert seq % tb == 0 and tb % WKV_CHUNK == 0 and aw % LANES == 0
    n_blocks = seq // tb
    row = lambda b, c: (b * n_blocks + c, 0)
    const2 = lambda b, c: (0, 0)
    const3 = lambda b, c: (0, 0, 0)
    full = lambda a: pl.BlockSpec(a.shape, const2 if a.ndim == 2 else const3)
    names = ("mu", "w0", "w2", "a0", "a2", "g2", "k_k", "k_a", "r_k", "ln_w", "ln_b", "w_pool", "pool_scale", "seg")
    params = [prm[n] for n in names]
    blk = lambda: pltpu.VMEM((tb, aw), F32)
    return pl.pallas_call(
        functools.partial(_prompt_mix_kernel, n_blocks=n_blocks),
        grid=(batch, n_blocks),
        in_specs=[pl.BlockSpec((tb, na), row), pl.BlockSpec((tb, bw), row)] + [full(a) for a in params],
        out_specs=[pl.BlockSpec((tb, aw), row), pl.BlockSpec((tb, bw), row),
                   pl.BlockSpec((None, n_heads, HEAD_DIM, HEAD_DIM), lambda b, c: (b, 0, 0, 0))],
        out_shape=[jax.ShapeDtypeStruct((rows, aw), F32), jax.ShapeDtypeStruct((rows, bw), F32),
                   jax.ShapeDtypeStruct((batch, n_heads, HEAD_DIM, HEAD_DIM), F32)],
        scratch_shapes=[pltpu.VMEM((tb + 8, na), F32), pltpu.VMEM((tb + 16, bw), F32),
                        pltpu.VMEM((aw // LANES, LANES, LANES), F32),
                        blk(), blk(), blk(), blk(), blk(), blk(), blk(), blk()],
        compiler_params=_cparams(2),
        name="mix_prompt",
    )(pa, pb, *params)


def _sample_prep_kernel(pa_ref, sh0_ref, pb_ref, hist_ref, mu_ref, w0_ref, w2_ref, a0_ref, a2_ref, g2_ref,
                        kk_ref, ka_ref, wpool_ref, pscale_ref, seg_ref,
                        r_ref, w_ref, k_ref, v_ref, a_ref, b_ref, g_ref, yb_ref, *, batch, seq):
    aw = r_ref.shape[1]
    bw = yb_ref.shape[1]
    pa = pa_ref[...]
    prev = jnp.concatenate([sh0_ref[...], pa[:(seq - 1) * batch]], axis=0)
    xs = pa + (prev - pa) * mu_ref[...]
    r, k, v, logw, a, b, g = _rwkv_prep(xs, aw, seg_ref[...], w0_ref[...], w2_ref[...], a0_ref[...], a2_ref[...],
                                        g2_ref[...], kk_ref[...], ka_ref[...])
    r_ref[...] = r
    w_ref[...] = jnp.exp(logw)
    k_ref[...] = k
    v_ref[...] = v
    a_ref[...] = a
    b_ref[...] = b
    g_ref[...] = g

    gw = bw // len(POOL_WINDOWS)

    def slab(i, cols):
        if i < POOL_HIST:
            return hist_ref[i * batch:(i + 1) * batch, cols]
        return pb_ref[(i - POOL_HIST) * batch:(i - POOL_HIST + 1) * batch, cols]

    for ti in range(seq):
        rows = slice(ti * batch, (ti + 1) * batch)
        for gi, w in enumerate(POOL_WINDOWS):
            cols = slice(gi * gw, (gi + 1) * gw)
            s = _pool_group_sum(lambda j: slab(POOL_HIST + ti - j, cols), w)
            cnt = float(min(w, PAST_LEN + ti + 1))
            dlt = s / cnt - pb_ref[rows, cols]
            yb_ref[rows, cols] = _dot(dlt, wpool_ref[gi]) * pscale_ref[:, cols]


def _sample_prep_call(pa, sh0, pb, hist, batch, seq, prm):
    rows = pa.shape[0]
    aw = prm["w0"].shape[1]
    bw = pb.shape[1]
    names = ("mu", "w0", "w2", "a0", "a2", "g2", "k_k", "k_a", "w_pool", "pool_scale", "seg")
    args = [pa, sh0, pb, hist] + [prm[n] for n in names]
    full = lambda a: pl.BlockSpec(a.shape, (lambda i: (0, 0)) if a.ndim == 2 else (lambda i: (0, 0, 0)))
    o = jax.ShapeDtypeStruct((rows, aw), F32)
    ob = jax.ShapeDtypeStruct((rows, bw), F32)
    return pl.pallas_call(
        functools.partial(_sample_prep_kernel, batch=batch, seq=seq),
        grid=(1,),
        in_specs=[full(a) for a in args],
        out_specs=[full(o)] * 7 + [full(ob)],
        out_shape=[o] * 7 + [ob],
        compiler_params=_cparams(1),
        name="prep_sample",
    )(*args)


def _sample_wkv_kernel(r_ref, w_ref, k_ref, v_ref, a_ref, b_ref, g_ref, rk_ref, lnw_ref, lnb_ref, s_ref,
                       y_ref, so_ref):
    seq, bb, n_heads, hd = r_ref.shape
    eye = (lax.broadcasted_iota(jnp.int32, (hd, hd), 0) == lax.broadcasted_iota(jnp.int32, (hd, hd), 1))
    rk, lnw, lnb = rk_ref[...], lnw_ref[...], lnb_ref[...]

    def per_seq(bi, carry):
        s = [s_ref[bi, h] for h in range(n_heads)]
        for ti in range(seq):
            rt, wt, kt, vt = r_ref[ti, bi], w_ref[ti, bi], k_ref[ti, bi], v_ref[ti, bi]
            at, bt = a_ref[ti, bi], b_ref[ti, bi]
            yrows = []
            for h in range(n_heads):
                hs = slice(h, h + 1)
                sa = jnp.sum(s[h] * at[hs], axis=1, keepdims=True)
                vcol = jnp.sum(jnp.where(eye, vt[hs], 0.0), axis=1, keepdims=True)
                s[h] = s[h] * wt[hs] + sa * bt[hs] + vcol * kt[hs]
                ycol = jnp.sum(s[h] * rt[hs], axis=1, keepdims=True)
                yrows.append(jnp.sum(jnp.where(eye, ycol, 0.0), axis=0, keepdims=True))
            y = jnp.concatenate(yrows, axis=0)
            yc = y - jnp.mean(y, axis=-1, keepdims=True)
            var = jnp.mean(yc * yc, axis=-1, keepdims=True)
            yn = yc * lax.rsqrt(var + GN_EPS) * lnw + lnb
            bonus = jnp.sum(rt * kt * rk, axis=-1, keepdims=True) * vt
            y_ref[ti, bi] = (yn + bonus) * g_ref[ti, bi]
        for h in range(n_heads):
            so_ref[bi, h] = s[h]
        return carry

    lax.fori_loop(0, bb, per_seq, 0)


def _sample_wkv_call(vecs, state, batch, seq, prm):
    n_heads = state.shape[1]
    bb = 8
    assert batch % bb == 0
    v4 = [a.reshape(seq, batch, n_heads, HEAD_DIM) for a in vecs]
    vspec = pl.BlockSpec((seq, bb, n_heads, HEAD_DIM), lambda i: (0, i, 0, 0))
    hspec = pl.BlockSpec((n_heads, HEAD_DIM), lambda i: (0, 0))
    sspec = pl.BlockSpec((bb, n_heads, HEAD_DIM, HEAD_DIM), lambda i: (i, 0, 0, 0))
    hp = [prm[n].reshape(n_heads, HEAD_DIM) for n in ("r_k", "ln_w", "ln_b")]
    y, s_out = pl.pallas_call(
        _sample_wkv_kernel,
        grid=(batch // bb,),
        in_specs=[vspec] * 7 + [hspec] * 3 + [sspec],
        out_specs=[vspec, sspec],
        out_shape=[jax.ShapeDtypeStruct(v4[0].shape, F32), jax.ShapeDtypeStruct(state.shape, F32)],
        compiler_params=_cparams(1),
        name="wkv_sample",
    )(*v4, *hp, state)
    return y.reshape(seq * batch, n_heads * HEAD_DIM), s_out


def kernel(x_prompt, x_sample, state_wkv, state_shift, state_pool, c_prompt, c_sample, norm_g, w_mod, b_mod, w_ffn_in, w_ffn_out, w_in, mu_shift, w0, w2, a0, a2, g2, k_k, k_a, r_k, ln_x_w, ln_x_b, w_pool, pool_scale, w_br_a, w_br_b, w_gate, b_gate, w_out, final_g):
    bp, lp, d = x_prompt.shape
    bs, ls, _ = x_sample.shape
    depth = w_mod.shape[0]
    aw = w0.shape[1]
    a_proj = mu_shift.shape[1]
    bw = pool_scale.shape[1]
    n_heads = aw // HEAD_DIM

    bf = lambda w: w.astype(BF16)
    w_ffn_in_b, w_ffn_out_b = bf(w_ffn_in), bf(w_ffn_out)
    w_in_a, w_in_b = bf(w_in[:, :, :a_proj]), bf(w_in[:, :, a_proj:])
    w_gate_b, w_br_a_b, w_br_b_b, w_out_b = bf(w_gate), bf(w_br_a), bf(w_br_b), bf(w_out)
    w2_b, a2_b, g2_b, w_pool_b = bf(w2), bf(a2), bf(g2), bf(w_pool)
    head_of = jnp.arange(aw, dtype=jnp.int32) // HEAD_DIM
    seg_ones = (head_of[:, None] == head_of[None, :]).astype(BF16)

    mod = _mod_call(jnp.concatenate([c_sample, c_prompt], axis=0), w_mod, b_mod)
    gp = _Group("prompt", bp, lp, 512, mod_row0=bs)
    gs = _Group("sample", bs, ls, bs * ls, mod_row0=0)

    xp = x_prompt.reshape(bp * lp, d)
    xs = jnp.transpose(x_sample, (1, 0, 2)).reshape(ls * bs, d)
    hist_s = jnp.transpose(state_pool, (0, 2, 1, 3)).reshape(depth, POOL_HIST * bs, bw)

    wkv_p, shift_p, pool_p, wkv_s, shift_s, pool_s = [], [], [], [], [], []
    for l in range(depth):
        prm = dict(mu=mu_shift[l][None], w0=w0[l][None], w2=w2_b[l], a0=a0[l][None], a2=a2_b[l], g2=g2_b[l],
                   k_k=k_k[l][None], k_a=k_a[l][None], r_k=r_k[l].reshape(1, aw), ln_w=ln_x_w[l][None],
                   ln_b=ln_x_b[l][None], w_pool=w_pool_b[l], pool_scale=pool_scale[l][None], seg=seg_ones)
        last = l == depth - 1

        xp = _ffn_call(xp, gp, mod, l, 0, norm_g[l, 0], w_ffn_in_b[l, 0], w_ffn_out_b[l, 0])
        pa, pb = _mixin_call(xp, gp, mod, l, norm_g[l, 1], w_in_a[l], w_in_b[l])
        ya, yb, s_new = _prompt_mix_call(pa, pb, bp, lp, prm)
        xp = _mixout_call(xp, gp, mod, l, norm_g[l, 1], ya, yb, w_gate_b[l], b_gate[l], w_br_a_b[l], w_br_b_b[l],
                          w_out_b[l])
        xp = _ffn_call(xp, gp, mod, l, 2, norm_g[l, 2], w_ffn_in_b[l, 1], w_ffn_out_b[l, 1],
                       final_g if last else None)
        wkv_p.append(s_new)
        shift_p.append(pa.reshape(bp, lp, a_proj)[:, -1:])
        pool_p.append(pb.reshape(bp, lp, bw)[:, -POOL_HIST:])

        xs = _ffn_call(xs, gs, mod, l, 0, norm_g[l, 0], w_ffn_in_b[l, 0], w_ffn_out_b[l, 0])
        pa, pb = _mixin_call(xs, gs, mod, l, norm_g[l, 1], w_in_a[l], w_in_b[l])
        *vecs, yb = _sample_prep_call(pa, state_shift[l, :, 0, :], pb, hist_s[l], bs, ls, prm)
        ya, s_new = _sample_wkv_call(vecs, state_wkv[l], bs, ls, prm)
        xs = _mixout_call(xs, gs, mod, l, norm_g[l, 1], ya, yb, w_gate_b[l], b_gate[l], w_br_a_b[l], w_br_b_b[l],
                          w_out_b[l])
        xs = _ffn_call(xs, gs, mod, l, 2, norm_g[l, 2], w_ffn_in_b[l, 1], w_ffn_out_b[l, 1],
                       final_g if last else None)
        wkv_s.append(s_new)
        shift_s.append(pa.reshape(ls, bs, a_proj)[-1][:, None, :])
        pb_bm = jnp.transpose(pb.reshape(ls, bs, bw), (1, 0, 2))
        pool_s.append(jnp.concatenate([state_pool[l], pb_bm], axis=1)[:, -POOL_HIST:])

    y_prompt = xp.reshape(bp, lp, d)
    y_sample = jnp.transpose(xs.reshape(ls, bs, d), (1, 0, 2))
    return (y_prompt, y_sample, jnp.stack(wkv_p), jnp.stack(shift_p), jnp.stack(pool_p),
            jnp.stack(wkv_s), jnp.stack(shift_s), jnp.stack(pool_s))
```

```python
import functools
import math

import jax
import jax.numpy as jnp
from jax import lax
from jax.experimental import pallas as pl
from jax.experimental.pallas import tpu as pltpu

F32 = jnp.float32
BF16 = jnp.bfloat16

HEAD_DIM = 64
POOL_WINDOWS = (2, 4, 8, 16)
POOL_HIST = max(POOL_WINDOWS) - 1
N_SUB = 3
NORM_EPS = 1e-6
GN_EPS = 64e-5
PAST_LEN = 16384
DECAY_SCALE = math.exp(-0.5)

LANES = 128
WKV_CHUNK = 64
VMEM_LIMIT_BYTES = 56 * 1024 * 1024


def _cparams(n_axes):
    return pltpu.CompilerParams(dimension_semantics=("arbitrary",) * n_axes,
                                vmem_limit_bytes=VMEM_LIMIT_BYTES)


def _sigmoid(x):
    return 1.0 / (1.0 + jnp.exp(-x))


def _dot(a, b):
    return jnp.dot(a.astype(BF16), b.astype(BF16), preferred_element_type=F32)


def _dot_nt(a, b):
    return lax.dot_general(a.astype(BF16), b.astype(BF16), (((1,), (1,)), ((), ())),
                           preferred_element_type=F32)


def _dot_tn(a, b):
    return lax.dot_general(a.astype(BF16), b.astype(BF16), (((0,), (0,)), ((), ())),
                           preferred_element_type=F32)


def _dot_hilo(a, b_exact):
    hi = a.astype(BF16)
    lo = (a - hi.astype(F32)).astype(BF16)
    return (jnp.dot(hi, b_exact, preferred_element_type=F32)
            + jnp.dot(lo, b_exact, preferred_element_type=F32))


def _rms(x):
    return x * lax.rsqrt(jnp.mean(x * x, axis=-1, keepdims=True) + NORM_EPS)


def _rows(ref, tm):
    m = ref[...]
    if m.shape[0] in (1, tm):
        return m
    return jnp.concatenate([m] * (tm // m.shape[0]), axis=0)


def _modnorm(x, g, shift, scale):
    return _rms(x) * g * (1.0 + scale) + shift


def _mod_kernel(c_ref, w_ref, b_ref, o_ref):
    c = c_ref[...]
    o_ref[...] = _dot(c * _sigmoid(c), w_ref[...]) + b_ref[...]


def _mod_call(c_all, w_mod, b_mod):
    depth, d, n9 = w_mod.shape
    rows = c_all.shape[0]
    n_sub9 = n9 // d
    tn = d
    return pl.pallas_call(
        _mod_kernel,
        grid=(depth, n_sub9),
        in_specs=[pl.BlockSpec((rows, d), lambda l, j: (0, 0)),
                  pl.BlockSpec((None, d, tn), lambda l, j: (l, 0, j)),
                  pl.BlockSpec((None, 1, tn), lambda l, j: (l, 0, j))],
        out_specs=pl.BlockSpec((None, None, rows, tn), lambda l, j: (l, j, 0, 0)),
        out_shape=jax.ShapeDtypeStruct((depth, n_sub9, rows, d), F32),
        compiler_params=_cparams(2),
        name="mod",
    )(c_all, w_mod, b_mod.reshape(depth, 1, n9))


class _Group:
    def __init__(self, kind, batch, seq, tm, ffn_tm, mod_row0):
        self.kind, self.batch, self.seq, self.tm, self.ffn_tm, self.mod_row0 = kind, batch, seq, tm, ffn_tm, mod_row0
        self.rows = batch * seq
        for t in (tm, ffn_tm):
            assert self.rows % t == 0
            if kind == "prompt":
                assert seq % t == 0

    def mod_spec(self, d, layer, idx, tm=None):
        if self.kind == "prompt":
            per_seq = self.seq // (tm or self.tm)
            return pl.BlockSpec((None, None, None, 1, d),
                                lambda i, *_: (layer, idx, self.mod_row0 + i // per_seq, 0, 0))
        return pl.BlockSpec((None, None, self.batch, d), lambda i, *_: (layer, idx, 0, 0))

    def mod_arg(self, mod):
        if self.kind == "prompt":
            return mod.reshape(mod.shape[:3] + (1, mod.shape[3]))
        return mod


FFN_CHUNK = 512


def _ffn_kernel(x_ref, sh_ref, sc_ref, gt_ref, g_ref, win_ref, wo_ref, *rest, final):
    if final:
        fg_ref, o_ref, acc_sc = rest
    else:
        o_ref, acc_sc = rest
    tm = x_ref.shape[0]
    d_ff = wo_ref.shape[0]
    x = x_ref[...]
    u = _modnorm(x, g_ref[...], _rows(sh_ref, tm), _rows(sc_ref, tm)).astype(BF16)
    for c0 in range(0, d_ff, FFN_CHUNK):
        c1 = min(c0 + FFN_CHUNK, d_ff)
        hg = jnp.dot(u, win_ref[:, c0:c1], preferred_element_type=F32)
        hu = jnp.dot(u, win_ref[:, d_ff + c0:d_ff + c1], preferred_element_type=F32)
        act = (hg * _sigmoid(hg) * hu).astype(BF16)
        part = jnp.dot(act, wo_ref[c0:c1, :], preferred_element_type=F32)
        if c0 == 0:
            acc_sc[...] = part
        else:
            acc_sc[...] += part
    y = x + 0.5 * _rows(gt_ref, tm) * acc_sc[...]
    if final:
        y = _rms(y) * fg_ref[...]
    o_ref[...] = y


def _ffn_call(x, grp, mod, layer, sub, norm_g, w_in, w_out, final_g=None):
    rows, d = x.shape
    d_ff = w_out.shape[0]
    assert d_ff % LANES == 0 and FFN_CHUNK % LANES == 0
    tm = grp.ffn_tm
    final = final_g is not None
    row = lambda i: (i, 0)
    const = lambda i: (0, 0)
    resident = dict(pipeline_mode=pl.Buffered(1))
    in_specs = [pl.BlockSpec((tm, d), row),
                grp.mod_spec(d, layer, 3 * sub, tm), grp.mod_spec(d, layer, 3 * sub + 1, tm),
                grp.mod_spec(d, layer, 3 * sub + 2, tm),
                pl.BlockSpec((1, d), const),
                pl.BlockSpec((d, 2 * d_ff), const, **resident),
                pl.BlockSpec((d_ff, d), const, **resident)]
    m = grp.mod_arg(mod)
    args = [x, m, m, m, norm_g.reshape(1, d), w_in, w_out]
    if final:
        in_specs.append(pl.BlockSpec((1, d), const))
        args.append(final_g.reshape(1, d))
    return pl.pallas_call(
        functools.partial(_ffn_kernel, final=final),
        grid=(rows // tm,),
        in_specs=in_specs,
        out_specs=pl.BlockSpec((tm, d), row),
        out_shape=jax.ShapeDtypeStruct((rows, d), F32),
        scratch_shapes=[pltpu.VMEM((tm, d), F32)],
        compiler_params=_cparams(1),
        name=f"ffn_{grp.kind}",
    )(*args)


def _mixin_kernel(x_ref, sh_ref, sc_ref, g_ref, wa_ref, wb_ref, pa_ref, pb_ref):
    tm = x_ref.shape[0]
    u = _modnorm(x_ref[...], g_ref[...], _rows(sh_ref, tm), _rows(sc_ref, tm)).astype(BF16)
    pa_ref[...] = jnp.dot(u, wa_ref[...], preferred_element_type=F32)
    pb_ref[...] = jnp.dot(u, wb_ref[...], preferred_element_type=F32)


def _mixin_call(x, grp, mod, layer, norm_g, w_a, w_b):
    rows, d = x.shape
    tm = grp.tm
    na, nb = w_a.shape[1], w_b.shape[1]
    row = lambda i: (i, 0)
    const = lambda i: (0, 0)
    m = grp.mod_arg(mod)
    return pl.pallas_call(
        _mixin_kernel,
        grid=(rows // tm,),
        in_specs=[pl.BlockSpec((tm, d), row), grp.mod_spec(d, layer, 3), grp.mod_spec(d, layer, 4),
                  pl.BlockSpec((1, d), const), pl.BlockSpec((d, na), const), pl.BlockSpec((d, nb), const)],
        out_specs=[pl.BlockSpec((tm, na), row), pl.BlockSpec((tm, nb), row)],
        out_shape=[jax.ShapeDtypeStruct((rows, na), F32), jax.ShapeDtypeStruct((rows, nb), F32)],
        compiler_params=_cparams(1),
        name=f"mixin_{grp.kind}",
    )(x, m, m, norm_g.reshape(1, d), w_a, w_b)


def _mixout_kernel(x_ref, sh_ref, sc_ref, gt_ref, g_ref, ya_ref, yb_ref, wgate_ref, bgate_ref,
                   wbra_ref, wbrb_ref, wout_ref, o_ref):
    tm, d = x_ref.shape
    x = x_ref[...]
    u = _modnorm(x, g_ref[...], _rows(sh_ref, tm), _rows(sc_ref, tm)).astype(BF16)
    gates = _sigmoid(jnp.dot(u, wgate_ref[...], preferred_element_type=F32) + bgate_ref[...])
    ma = _dot(ya_ref[...], wbra_ref[...])
    mb = _dot(yb_ref[...], wbrb_ref[...])
    merged = gates[:, :d] * ma + gates[:, d:] * mb
    o_ref[...] = x + _rows(gt_ref, tm) * _dot(merged, wout_ref[...])


def _mixout_call(x, grp, mod, layer, norm_g, ya, yb, w_gate, b_gate, w_br_a, w_br_b, w_out):
    rows, d = x.shape
    tm = grp.tm
    wa, wb = ya.shape[1], yb.shape[1]
    row = lambda i: (i, 0)
    const = lambda i: (0, 0)
    m = grp.mod_arg(mod)
    return pl.pallas_call(
        _mixout_kernel,
        grid=(rows // tm,),
        in_specs=[pl.BlockSpec((tm, d), row), grp.mod_spec(d, layer, 3), grp.mod_spec(d, layer, 4),
                  grp.mod_spec(d, layer, 5), pl.BlockSpec((1, d), const),
                  pl.BlockSpec((tm, wa), row), pl.BlockSpec((tm, wb), row),
                  pl.BlockSpec((d, 2 * d), const), pl.BlockSpec((1, 2 * d), const),
                  pl.BlockSpec((wa, d), const), pl.BlockSpec((wb, d), const), pl.BlockSpec((d, d), const)],
        out_specs=pl.BlockSpec((tm, d), row),
        out_shape=jax.ShapeDtypeStruct((rows, d), F32),
        compiler_params=_cparams(1),
        name=f"mixout_{grp.kind}",
    )(x, m, m, m, norm_g.reshape(1, d), ya, yb, w_gate, b_gate.reshape(1, 2 * d), w_br_a, w_br_b, w_out)


def _rwkv_prep(xs, aw, seg_ones, w0, w2, a0, a2, g2, k_k, k_a):
    n_dec, n_aaa = w2.shape[0], a2.shape[0]
    r = xs[:, 0:aw]
    k = xs[:, aw:2 * aw]
    v = xs[:, 2 * aw:3 * aw]
    o = 3 * aw
    wl = xs[:, o:o + n_dec]
    al = xs[:, o + n_dec:o + n_dec + n_aaa]
    gl = xs[:, o + n_dec + n_aaa:]
    logw = -DECAY_SCALE * _sigmoid(w0 + _dot(jnp.tanh(wl), w2))
    asig = _sigmoid(a0 + _dot(al, a2))
    g = _dot(_sigmoid(gl), g2)
    kk = k * k_k
    kk = kk / jnp.maximum(jnp.sqrt(_dot_hilo(kk * kk, seg_ones)), 1e-12)
    k = k * (1.0 + (asig - 1.0) * k_a)
    return r, k, v, logw, -kk, kk * asig, g


def _pool_group_sum(read_rows, w):
    s = read_rows(0)
    for j in range(1, w):
        s = s + read_rows(j)
    return s


def _prompt_mix_kernel(pa_ref, pb_ref, mu_ref, w0_ref, w2_ref, a0_ref, a2_ref, g2_ref, kk_ref, ka_ref, rk_ref,
                       lnw_ref, lnb_ref, wpool_ref, pscale_ref, seg_ref,
                       ya_ref, yb_ref, wkv_ref,
                       pbuf, xbuf, s_sc, r_sc, k_sc, v_sc, a_sc, b_sc, lw_sc, g_sc, y_sc, *, n_blocks):
    tb, aw = ya_ref.shape
    bw = yb_ref.shape[1]
    t = WKV_CHUNK
    n_pairs = aw // LANES
    blk = pl.program_id(1)

    @pl.when(blk == 0)
    def _():
        pbuf[0:8, :] = jnp.zeros((8, pbuf.shape[1]), F32)
        xbuf[0:16, :] = jnp.zeros((16, bw), F32)
        s_sc[...] = jnp.zeros_like(s_sc)

    pa = pa_ref[...]
    pbuf[8:8 + tb, :] = pa
    prev = pbuf[7:7 + tb, :]
    xs = pa + (prev - pa) * mu_ref[...]
    pbuf[0:8, :] = pbuf[tb:tb + 8, :]

    seg_ones = seg_ref[...]
    r, k, v, logw, a, b, g = _rwkv_prep(xs, aw, seg_ones, w0_ref[...], w2_ref[...], a0_ref[...], a2_ref[...],
                                        g2_ref[...], kk_ref[...], ka_ref[...])
    r_sc[...] = r
    k_sc[...] = k
    v_sc[...] = v
    a_sc[...] = a
    b_sc[...] = b
    lw_sc[...] = logw
    g_sc[...] = g

    t2 = 2 * t
    ri = lax.broadcasted_iota(jnp.int32, (t, t), 0)
    ci = lax.broadcasted_iota(jnp.int32, (t, t), 1)
    tri_incl = (ri >= ci).astype(BF16)
    rs = lax.broadcasted_iota(jnp.int32, (t2, t2), 0)
    cs = lax.broadcasted_iota(jnp.int32, (t2, t2), 1)
    strict = cs < rs
    rs2 = lax.broadcasted_iota(jnp.int32, (t2, 2 * t2), 0)
    cs2 = lax.broadcasted_iota(jnp.int32, (t2, 2 * t2), 1)
    incl2 = jnp.where(cs2 >= t2, cs2 - t2, cs2) <= rs2
    lane = lax.broadcasted_iota(jnp.int32, (1, LANES), 1)
    head_mask = (lane < HEAD_DIM, lane >= HEAD_DIM)
    pairs = range(n_pairs)
    n_levels = int(math.log2(t))

    def stack(x):
        return jnp.concatenate([jnp.where(head_mask[0], x, 0.0), jnp.where(head_mask[1], x, 0.0)], axis=0)

    def chunk(ci_, carry):
        sl = pl.ds(pl.multiple_of(ci_ * t, t), t)
        lw = lw_sc[sl, :]
        hi = lw.astype(BF16)
        lo = (lw - hi.astype(F32)).astype(BF16)
        c = (jnp.dot(tri_incl, hi, preferred_element_type=F32)
             + jnp.dot(tri_incl, lo, preferred_element_type=F32))
        e_in = jnp.exp(c)
        e_prev = jnp.exp(c - lw)
        e_neg = jnp.exp(-c)
        p_end = e_in[t - 1:t, :]
        at = a_sc[sl, :] * e_prev
        rt = r_sc[sl, :] * e_in
        bt = b_sc[sl, :] * e_neg
        kt = k_sc[sl, :] * e_neg
        vv = v_sc[sl, :]
        cols = [slice(p * LANES, (p + 1) * LANES) for p in pairs]
        pe = [p_end[:, cols[p]] for p in pairs]
        a2 = [stack(at[:, cols[p]]).astype(BF16) for p in pairs]
        r2 = [stack(rt[:, cols[p]]).astype(BF16) for p in pairs]
        v2b = [stack(vv[:, cols[p]]).astype(BF16) for p in pairs]
        bk = [jnp.concatenate([stack(bt[:, cols[p]]), stack(kt[:, cols[p]])], axis=0) for p in pairs]
        g = [_dot_nt(jnp.concatenate([a2[p], r2[p]], axis=0), bk[p]) for p in pairs]
        n = [jnp.where(strict, g[p][:t2, :t2], 0.0) for p in pairs]
        ak = [jnp.where(strict, g[p][:t2, t2:], 0.0).astype(BF16) for p in pairs]
        rbk = [jnp.where(incl2, g[p][t2:, :], 0.0).astype(BF16) for p in pairs]
        xm = n
        pw = [_dot(n[p], n[p]) for p in pairs]
        for _ in range(n_levels - 2):
            res = [_dot(jnp.concatenate([pw[p], xm[p]], axis=0), pw[p]) for p in pairs]
            xm = [xm[p] + pw[p] + res[p][t2:] for p in pairs]
            pw = [res[p][:t2] for p in pairs]
        res = [_dot(xm[p], pw[p]) for p in pairs]
        xm = [(xm[p] + pw[p] + res[p]).astype(BF16) for p in pairs]
        s_old = [s_sc[p] for p in pairs]
        s_bf = [s_old[p].astype(BF16) for p in pairs]
        rhs = [_dot_nt(a2[p], s_bf[p]) for p in pairs]
        akv = [jnp.dot(ak[p], v2b[p], preferred_element_type=F32) for p in pairs]
        rhs = [rhs[p] + akv[p] for p in pairs]
        u = [rhs[p] + _dot(xm[p], rhs[p]) for p in pairs]
        uv = [jnp.concatenate([u[p].astype(BF16), v2b[p]], axis=0) for p in pairs]
        ys = [_dot_nt(r2[p], s_bf[p]) for p in pairs]
        yi = [jnp.dot(rbk[p], uv[p], preferred_element_type=F32) for p in pairs]
        sn = [_dot_tn(uv[p], bk[p] * pe[p]) for p in pairs]
        for p in pairs:
            y2 = ys[p] + yi[p]
            y_sc[sl, cols[p]] = y2[:t] + y2[t:]
            s_sc[p] = s_old[p] * pe[p] + sn[p]
        return carry

    lax.fori_loop(0, tb // t, chunk, 0)

    y = y_sc[...]
    inv_n = 1.0 / HEAD_DIM
    mean = _dot_hilo(y, seg_ones) * inv_n
    yc = y - mean
    var = _dot_hilo(yc * yc, seg_ones) * inv_n
    yn = yc * lax.rsqrt(var + GN_EPS) * lnw_ref[...] + lnb_ref[...]
    bonus = _dot_hilo(r_sc[...] * k_sc[...] * rk_ref[...], seg_ones) * v_sc[...]
    ya_ref[...] = (yn + bonus) * g_sc[...]

    pb = pb_ref[...]
    xbuf[16:16 + tb, :] = pb
    pos = blk * tb + lax.broadcasted_iota(jnp.int32, (tb, 1), 0)
    gw = bw // len(POOL_WINDOWS)
    for gi, w in enumerate(POOL_WINDOWS):
        cols = slice(gi * gw, (gi + 1) * gw)
        s = _pool_group_sum(lambda j: xbuf[16 - j:16 - j + tb, cols], w)
        cnt = jnp.minimum(w, pos + 1).astype(F32)
        dlt = s / cnt - pb[:, cols]
        yb_ref[:, cols] = _dot(dlt, wpool_ref[gi]) * pscale_ref[:, cols]
    xbuf[0:16, :] = xbuf[tb:tb + 16, :]

    @pl.when(blk == n_blocks - 1)
    def _():
        for p in range(n_pairs):
            s_p = s_sc[p]
            for h in range(2):
                o = h * HEAD_DIM
                wkv_ref[2 * p + h] = s_p[o:o + HEAD_DIM, o:o + HEAD_DIM]


def _prompt_mix_call(pa, pb, batch, seq, prm):
    rows, na = pa.shape
    bw = pb.shape[1]
    aw = prm["w0"].shape[1]
    n_heads = aw // HEAD_DIM
    tb = 256
    assert seq % tb == 0 and tb % WKV_CHUNK == 0 and aw % LANES == 0
    n_blocks = seq // tb
    row = lambda b, c: (b * n_blocks + c, 0)
    const2 = lambda b, c: (0, 0)
    const3 = lambda b, c: (0, 0, 0)
    full = lambda a: pl.BlockSpec(a.shape, const2 if a.ndim == 2 else const3)
    names = ("mu", "w0", "w2", "a0", "a2", "g2", "k_k", "k_a", "r_k", "ln_w", "ln_b", "w_pool", "pool_scale", "seg")
    params = [prm[n] for n in names]
    blk = lambda: pltpu.VMEM((tb, aw), F32)
    return pl.pallas_call(
        functools.partial(_prompt_mix_kernel, n_blocks=n_blocks),
        grid=(batch, n_blocks),
        in_specs=[pl.BlockSpec((tb, na), row), pl.BlockSpec((tb, bw), row)] + [full(a) for a in params],
        out_specs=[pl.BlockSpec((tb, aw), row), pl.BlockSpec((tb, bw), row),
                   pl.BlockSpec((None, n_heads, HEAD_DIM, HEAD_DIM), lambda b, c: (b, 0, 0, 0))],
        out_shape=[jax.ShapeDtypeStruct((rows, aw), F32), jax.ShapeDtypeStruct((rows, bw), F32),
                   jax.ShapeDtypeStruct((batch, n_heads, HEAD_DIM, HEAD_DIM), F32)],
        scratch_shapes=[pltpu.VMEM((tb + 8, na), F32), pltpu.VMEM((tb + 16, bw), F32),
                        pltpu.VMEM((aw // LANES, LANES, LANES), F32),
                        blk(), blk(), blk(), blk(), blk(), blk(), blk(), blk()],
        compiler_params=_cparams(2),
        name="mix_prompt",
    )(pa, pb, *params)


def _sample_prep_kernel(pa_ref, sh0_ref, pb_ref, hist_ref, mu_ref, w0_ref, w2_ref, a0_ref, a2_ref, g2_ref,
                        kk_ref, ka_ref, wpool_ref, pscale_ref, seg_ref,
                        r_ref, w_ref, k_ref, v_ref, a_ref, b_ref, g_ref, yb_ref, *, batch, seq):
    aw = r_ref.shape[1]
    bw = yb_ref.shape[1]
    pa = pa_ref[...]
    prev = jnp.concatenate([sh0_ref[...], pa[:(seq - 1) * batch]], axis=0)
    xs = pa + (prev - pa) * mu_ref[...]
    r, k, v, logw, a, b, g = _rwkv_prep(xs, aw, seg_ref[...], w0_ref[...], w2_ref[...], a0_ref[...], a2_ref[...],
                                        g2_ref[...], kk_ref[...], ka_ref[...])
    outs = ((r_ref, r), (w_ref, jnp.exp(logw)), (k_ref, k), (v_ref, v), (a_ref, a), (b_ref, b), (g_ref, g))
    for ref, val in outs:
        for ti in range(seq):
            ref[ti] = val[ti * batch:(ti + 1) * batch, :].T

    gw = bw // len(POOL_WINDOWS)

    def slab(i, cols):
        if i < POOL_HIST:
            return hist_ref[i * batch:(i + 1) * batch, cols]
        return pb_ref[(i - POOL_HIST) * batch:(i - POOL_HIST + 1) * batch, cols]

    for ti in range(seq):
        rows = slice(ti * batch, (ti + 1) * batch)
        for gi, w in enumerate(POOL_WINDOWS):
            cols = slice(gi * gw, (gi + 1) * gw)
            s = _pool_group_sum(lambda j: slab(POOL_HIST + ti - j, cols), w)
            cnt = float(min(w, PAST_LEN + ti + 1))
            dlt = s / cnt - pb_ref[rows, cols]
            yb_ref[rows, cols] = _dot(dlt, wpool_ref[gi]) * pscale_ref[:, cols]


def _sample_prep_call(pa, sh0, pb, hist, batch, seq, prm):
    rows = pa.shape[0]
    aw = prm["w0"].shape[1]
    bw = pb.shape[1]
    names = ("mu", "w0", "w2", "a0", "a2", "g2", "k_k", "k_a", "w_pool", "pool_scale", "seg")
    args = [pa, sh0, pb, hist] + [prm[n] for n in names]
    full = lambda a: pl.BlockSpec(a.shape, (lambda i: (0, 0)) if a.ndim == 2 else (lambda i: (0, 0, 0)))
    o = jax.ShapeDtypeStruct((seq, aw, batch), F32)
    ob = jax.ShapeDtypeStruct((rows, bw), F32)
    return pl.pallas_call(
        functools.partial(_sample_prep_kernel, batch=batch, seq=seq),
        grid=(1,),
        in_specs=[full(a) for a in args],
        out_specs=[full(o)] * 7 + [full(ob)],
        out_shape=[o] * 7 + [ob],
        compiler_params=_cparams(1),
        name="prep_sample",
    )(*args)


SAMPLE_STEPS_PER_PAIR = 8


def _sample_wkv_kernel(r_ref, w_ref, k_ref, v_ref, a_ref, b_ref, g_ref, rk_ref, lnw_ref, lnb_ref, s_ref,
                       y_ref, so_ref, y_sc):
    seq, _, batch = r_ref.shape
    hd = HEAD_DIM
    nblk = s_ref.shape[1] // LANES
    rows_per_blk = LANES // hd
    st = pl.program_id(1)
    steps_per_head = SAMPLE_STEPS_PER_PAIR // 2
    hrow = pl.multiple_of((st // steps_per_head) * hd, hd)
    vbase = hrow + (st % steps_per_head) * (nblk * rows_per_blk)
    hs = pl.ds(hrow, hd)
    for jb in range(nblk):
        cols = slice(jb * LANES, (jb + 1) * LANES)
        tl = s_ref[:, cols].T
        tiles = [tl[q * hd:(q + 1) * hd] for q in range(rows_per_blk)]
        for ti in range(seq):
            for q in range(rows_per_blk):
                vi = vbase + jb * rows_per_blk + q
                sa = jnp.sum(tiles[q] * a_ref[ti, hs, :], axis=0, keepdims=True)
                tiles[q] = (tiles[q] * w_ref[ti, hs, :] + sa * b_ref[ti, hs, :]
                            + v_ref[ti, pl.ds(vi, 1), :] * k_ref[ti, hs, :])
                y_sc[ti, pl.ds(vi, 1), :] = jnp.sum(tiles[q] * r_ref[ti, hs, :], axis=0, keepdims=True)
        so_ref[:, cols] = jnp.concatenate(tiles, axis=0).T

    @pl.when(st == SAMPLE_STEPS_PER_PAIR - 1)
    def _():
        for ti in range(seq):
            outs = []
            for hh in range(2):
                rows = slice(hh * hd, (hh + 1) * hd)
                y = y_sc[ti, rows, :]
                yc = y - jnp.mean(y, axis=0, keepdims=True)
                var = jnp.mean(yc * yc, axis=0, keepdims=True)
                yn = yc * lax.rsqrt(var + GN_EPS) * lnw_ref[rows, :] + lnb_ref[rows, :]
                rk = r_ref[ti, rows, :] * k_ref[ti, rows, :] * rk_ref[rows, :]
                bonus = jnp.sum(rk, axis=0, keepdims=True) * v_ref[ti, rows, :]
                outs.append((yn + bonus) * g_ref[ti, rows, :])
            y_ref[ti * batch:(ti + 1) * batch, :] = jnp.concatenate(outs, axis=0).T


def _sample_wkv_call(vecs, state, batch, seq, prm):
    n_heads = state.shape[1]
    aw = n_heads * HEAD_DIM
    pair_cols = 2 * HEAD_DIM * HEAD_DIM
    assert 2 * HEAD_DIM == LANES and pair_cols % (SAMPLE_STEPS_PER_PAIR * LANES) == 0
    step_cols = pair_cols // SAMPLE_STEPS_PER_PAIR
    vspec = pl.BlockSpec((seq, LANES, batch), lambda p, s: (0, p, 0))
    cspec = pl.BlockSpec((LANES, 1), lambda p, s: (p, 0))
    sspec = pl.BlockSpec((batch, step_cols), lambda p, s: (0, p * SAMPLE_STEPS_PER_PAIR + s))
    cols = [prm[n].reshape(aw, 1) for n in ("r_k", "ln_w", "ln_b")]
    y, s_out = pl.pallas_call(
        _sample_wkv_kernel,
        grid=(n_heads // 2, SAMPLE_STEPS_PER_PAIR),
        in_specs=[vspec] * 7 + [cspec] * 3 + [sspec],
        out_specs=[pl.BlockSpec((seq * batch, LANES), lambda p, s: (0, p)), sspec],
        out_shape=[jax.ShapeDtypeStruct((seq * batch, aw), F32),
                   jax.ShapeDtypeStruct((batch, n_heads * HEAD_DIM * HEAD_DIM), F32)],
        scratch_shapes=[pltpu.VMEM((seq, LANES, batch), F32)],
        compiler_params=_cparams(2),
        name="wkv_sample",
    )(*vecs, *cols, state.reshape(batch, n_heads * HEAD_DIM * HEAD_DIM))
    return y, s_out.reshape(state.shape)


def kernel(x_prompt, x_sample, state_wkv, state_shift, state_pool, c_prompt, c_sample, norm_g, w_mod, b_mod, w_ffn_in, w_ffn_out, w_in, mu_shift, w0, w2, a0, a2, g2, k_k, k_a, r_k, ln_x_w, ln_x_b, w_pool, pool_scale, w_br_a, w_br_b, w_gate, b_gate, w_out, final_g):
    bp, lp, d = x_prompt.shape
    bs, ls, _ = x_sample.shape
    depth = w_mod.shape[0]
    aw = w0.shape[1]
    a_proj = mu_shift.shape[1]
    bw = pool_scale.shape[1]
    n_heads = aw // HEAD_DIM

    bf = lambda w: w.astype(BF16)
    w_ffn_in_b, w_ffn_out_b = bf(w_ffn_in), bf(w_ffn_out)
    w_in_a, w_in_b = bf(w_in[:, :, :a_proj]), bf(w_in[:, :, a_proj:])
    w_gate_b, w_br_a_b, w_br_b_b, w_out_b = bf(w_gate), bf(w_br_a), bf(w_br_b), bf(w_out)
    w2_b, a2_b, g2_b, w_pool_b = bf(w2), bf(a2), bf(g2), bf(w_pool)
    head_of = jnp.arange(aw, dtype=jnp.int32) // HEAD_DIM
    seg_ones = (head_of[:, None] == head_of[None, :]).astype(BF16)

    mod = _mod_call(jnp.concatenate([c_sample, c_prompt], axis=0), w_mod, b_mod)
    gp = _Group("prompt", bp, lp, min(512, lp), min(1024, lp), mod_row0=bs)
    gs = _Group("sample", bs, ls, bs * ls, bs * ls, mod_row0=0)

    xp = x_prompt.reshape(bp * lp, d)
    xs = jnp.transpose(x_sample, (1, 0, 2)).reshape(ls * bs, d)
    hist_s = jnp.transpose(state_pool, (0, 2, 1, 3)).reshape(depth, POOL_HIST * bs, bw)

    wkv_p, shift_p, pool_p, wkv_s, shift_s, pool_s = [], [], [], [], [], []
    for l in range(depth):
        prm = dict(mu=mu_shift[l][None], w0=w0[l][None], w2=w2_b[l], a0=a0[l][None], a2=a2_b[l], g2=g2_b[l],
                   k_k=k_k[l][None], k_a=k_a[l][None], r_k=r_k[l].reshape(1, aw), ln_w=ln_x_w[l][None],
                   ln_b=ln_x_b[l][None], w_pool=w_pool_b[l], pool_scale=pool_scale[l][None], seg=seg_ones)
        last = l == depth - 1

        xp = _ffn_call(xp, gp, mod, l, 0, norm_g[l, 0], w_ffn_in_b[l, 0], w_ffn_out_b[l, 0])
        pa, pb = _mixin_call(xp, gp, mod, l, norm_g[l, 1], w_in_a[l], w_in_b[l])
        ya, yb, s_new = _prompt_mix_call(pa, pb, bp, lp, prm)
        xp = _mixout_call(xp, gp, mod, l, norm_g[l, 1], ya, yb, w_gate_b[l], b_gate[l], w_br_a_b[l], w_br_b_b[l],
                          w_out_b[l])
        xp = _ffn_call(xp, gp, mod, l, 2, norm_g[l, 2], w_ffn_in_b[l, 1], w_ffn_out_b[l, 1],
                       final_g if last else None)
        wkv_p.append(s_new)
        shift_p.append(pa.reshape(bp, lp, a_proj)[:, -1:])
        pool_p.append(pb.reshape(bp, lp, bw)[:, -POOL_HIST:])

        xs = _ffn_call(xs, gs, mod, l, 0, norm_g[l, 0], w_ffn_in_b[l, 0], w_ffn_out_b[l, 0])
        pa, pb = _mixin_call(xs, gs, mod, l, norm_g[l, 1], w_in_a[l], w_in_b[l])
        *vecs, yb = _sample_prep_call(pa, state_shift[l, :, 0, :], pb, hist_s[l], bs, ls, prm)
        ya, s_new = _sample_wkv_call(vecs, state_wkv[l], bs, ls, prm)
        xs = _mixout_call(xs, gs, mod, l, norm_g[l, 1], ya, yb, w_gate_b[l], b_gate[l], w_br_a_b[l], w_br_b_b[l],
                          w_out_b[l])
        xs = _ffn_call(xs, gs, mod, l, 2, norm_g[l, 2], w_ffn_in_b[l, 1], w_ffn_out_b[l, 1],
                       final_g if last else None)
        wkv_s.append(s_new)
        shift_s.append(pa.reshape(ls, bs, a_proj)[-1][:, None, :])
        pb_bm = jnp.transpose(pb.reshape(ls, bs, bw), (1, 0, 2))
        pool_s.append(jnp.concatenate([state_pool[l], pb_bm], axis=1)[:, -POOL_HIST:])

    y_prompt = xp.reshape(bp, lp, d)
    y_sample = jnp.transpose(xs.reshape(ls, bs, d), (1, 0, 2))
    return (y_prompt, y_sample, jnp.stack(wkv_p), jnp.stack(shift_p), jnp.stack(pool_p),
            jnp.stack(wkv_s), jnp.stack(shift_s), jnp.stack(pool_s))
```

```python
import functools
import math

import jax
import jax.numpy as jnp
from jax import lax
from jax.experimental import pallas as pl
from jax.experimental.pallas import tpu as pltpu

F32 = jnp.float32
BF16 = jnp.bfloat16

HEAD_DIM = 64
POOL_WINDOWS = (2, 4, 8, 16)
POOL_HIST = max(POOL_WINDOWS) - 1
N_SUB = 3
NORM_EPS = 1e-6
GN_EPS = 64e-5
PAST_LEN = 16384
DECAY_SCALE = math.exp(-0.5)

LANES = 128
MXU_TILE = 256
WKV_CHUNK = 64
VMEM_LIMIT_BYTES = 56 * 1024 * 1024


_RESIDENT = dict(pipeline_mode=pl.Buffered(1))


def _cparams(n_axes):
    return pltpu.CompilerParams(dimension_semantics=("arbitrary",) * n_axes,
                                vmem_limit_bytes=VMEM_LIMIT_BYTES)


def _sigmoid(x):
    return 1.0 / (1.0 + jnp.exp(-x))


def _dot(a, b):
    return jnp.dot(a.astype(BF16), b.astype(BF16), preferred_element_type=F32)


def _dot_nt(a, b):
    return lax.dot_general(a.astype(BF16), b.astype(BF16), (((1,), (1,)), ((), ())),
                           preferred_element_type=F32)


def _dot_tn(a, b):
    return lax.dot_general(a.astype(BF16), b.astype(BF16), (((0,), (0,)), ((), ())),
                           preferred_element_type=F32)


def _seg_sum(x, seg_ones):
    n, width = x.shape
    tw = seg_ones.shape[0]
    outs = []
    for c0 in range(0, width, tw):
        xc = x[:, c0:c0 + tw]
        hi = xc.astype(BF16)
        lo = (xc - hi.astype(F32)).astype(BF16)
        both = jnp.dot(jnp.concatenate([hi, lo], axis=0), seg_ones, preferred_element_type=F32)
        outs.append(both[:n] + both[n:])
    return jnp.concatenate(outs, axis=1)


def _rms(x):
    return x * lax.rsqrt(jnp.mean(x * x, axis=-1, keepdims=True) + NORM_EPS)


def _rows(ref, tm):
    m = ref[...]
    if m.shape[0] in (1, tm):
        return m
    return jnp.concatenate([m] * (tm // m.shape[0]), axis=0)


def _modnorm(x, g, shift, scale):
    return _rms(x) * g * (1.0 + scale) + shift


def _mod_kernel(c_ref, w_ref, b_ref, o_ref):
    c = c_ref[...]
    o_ref[...] = _dot(c * _sigmoid(c), w_ref[...]) + b_ref[...]


def _mod_call(c_all, w_mod, b_mod):
    depth, d, n9 = w_mod.shape
    rows = c_all.shape[0]
    n_sub9 = n9 // d
    tn = d
    return pl.pallas_call(
        _mod_kernel,
        grid=(depth, n_sub9),
        in_specs=[pl.BlockSpec((rows, d), lambda l, j: (0, 0)),
                  pl.BlockSpec((None, d, tn), lambda l, j: (l, 0, j)),
                  pl.BlockSpec((None, 1, tn), lambda l, j: (l, 0, j))],
        out_specs=pl.BlockSpec((None, None, rows, tn), lambda l, j: (l, j, 0, 0)),
        out_shape=jax.ShapeDtypeStruct((depth, n_sub9, rows, d), F32),
        compiler_params=_cparams(2),
        name="mod",
    )(c_all, w_mod, b_mod.reshape(depth, 1, n9))


class _Group:
    def __init__(self, kind, batch, seq, tm, ffn_tm, mod_row0):
        self.kind, self.batch, self.seq, self.tm, self.ffn_tm, self.mod_row0 = kind, batch, seq, tm, ffn_tm, mod_row0
        self.rows = batch * seq
        for t in (tm, ffn_tm):
            assert self.rows % t == 0
            if kind == "prompt":
                assert seq % t == 0

    def mod_spec(self, d, layer, idx, tm=None):
        if self.kind == "prompt":
            per_seq = self.seq // (tm or self.tm)
            return pl.BlockSpec((None, None, None, 1, d),
                                lambda i, *_: (layer, idx, self.mod_row0 + i // per_seq, 0, 0))
        return pl.BlockSpec((None, None, self.batch, d), lambda i, *_: (layer, idx, 0, 0))

    def mod_arg(self, mod):
        if self.kind == "prompt":
            return mod.reshape(mod.shape[:3] + (1, mod.shape[3]))
        return mod


FFN_CHUNK = 2 * MXU_TILE


def _ffn_kernel(x_ref, sh_ref, sc_ref, gt_ref, g_ref, win_ref, wo_ref, *rest, final):
    if final:
        fg_ref, o_ref, acc_sc = rest
    else:
        o_ref, acc_sc = rest
    tm = x_ref.shape[0]
    d_ff = wo_ref.shape[0]
    x = x_ref[...]
    u = _modnorm(x, g_ref[...], _rows(sh_ref, tm), _rows(sc_ref, tm)).astype(BF16)
    for c0 in range(0, d_ff, FFN_CHUNK):
        c1 = min(c0 + FFN_CHUNK, d_ff)
        hg = jnp.dot(u, win_ref[:, c0:c1], preferred_element_type=F32)
        hu = jnp.dot(u, win_ref[:, d_ff + c0:d_ff + c1], preferred_element_type=F32)
        act = (hg * _sigmoid(hg) * hu).astype(BF16)
        part = jnp.dot(act, wo_ref[c0:c1, :], preferred_element_type=F32)
        if c0 == 0:
            acc_sc[...] = part
        else:
            acc_sc[...] += part
    y = x + 0.5 * _rows(gt_ref, tm) * acc_sc[...]
    if final:
        y = _rms(y) * fg_ref[...]
    o_ref[...] = y


def _ffn_call(x, grp, mod, layer, sub, norm_g, w_in, w_out, final_g=None):
    rows, d = x.shape
    d_ff = w_out.shape[0]
    assert d_ff % LANES == 0 and FFN_CHUNK % LANES == 0
    tm = grp.ffn_tm
    final = final_g is not None
    row = lambda i: (i, 0)
    const = lambda i: (0, 0)
    in_specs = [pl.BlockSpec((tm, d), row),
                grp.mod_spec(d, layer, 3 * sub, tm), grp.mod_spec(d, layer, 3 * sub + 1, tm),
                grp.mod_spec(d, layer, 3 * sub + 2, tm),
                pl.BlockSpec((1, d), const),
                pl.BlockSpec((d, 2 * d_ff), const, **_RESIDENT),
                pl.BlockSpec((d_ff, d), const, **_RESIDENT)]
    m = grp.mod_arg(mod)
    args = [x, m, m, m, norm_g.reshape(1, d), w_in, w_out]
    if final:
        in_specs.append(pl.BlockSpec((1, d), const))
        args.append(final_g.reshape(1, d))
    return pl.pallas_call(
        functools.partial(_ffn_kernel, final=final),
        grid=(rows // tm,),
        in_specs=in_specs,
        out_specs=pl.BlockSpec((tm, d), row),
        out_shape=jax.ShapeDtypeStruct((rows, d), F32),
        scratch_shapes=[pltpu.VMEM((tm, d), F32)],
        compiler_params=_cparams(1),
        name=f"ffn_{grp.kind}",
    )(*args)


def _mixin_kernel(x_ref, sh_ref, sc_ref, g_ref, wa_ref, wb_ref, pa_ref, pb_ref):
    tm = x_ref.shape[0]
    u = _modnorm(x_ref[...], g_ref[...], _rows(sh_ref, tm), _rows(sc_ref, tm)).astype(BF16)
    pa_ref[...] = jnp.dot(u, wa_ref[...], preferred_element_type=F32)
    pb_ref[...] = jnp.dot(u, wb_ref[...], preferred_element_type=F32)


def _mixin_call(x, grp, mod, layer, norm_g, w_a, w_b):
    rows, d = x.shape
    tm = grp.tm
    na, nb = w_a.shape[1], w_b.shape[1]
    row = lambda i: (i, 0)
    const = lambda i: (0, 0)
    m = grp.mod_arg(mod)
    return pl.pallas_call(
        _mixin_kernel,
        grid=(rows // tm,),
        in_specs=[pl.BlockSpec((tm, d), row), grp.mod_spec(d, layer, 3), grp.mod_spec(d, layer, 4),
                  pl.BlockSpec((1, d), const), pl.BlockSpec((d, na), const, **_RESIDENT),
                  pl.BlockSpec((d, nb), const, **_RESIDENT)],
        out_specs=[pl.BlockSpec((tm, na), row), pl.BlockSpec((tm, nb), row)],
        out_shape=[jax.ShapeDtypeStruct((rows, na), F32), jax.ShapeDtypeStruct((rows, nb), F32)],
        compiler_params=_cparams(1),
        name=f"mixin_{grp.kind}",
    )(x, m, m, norm_g.reshape(1, d), w_a, w_b)


def _mixout_kernel(x_ref, sh_ref, sc_ref, gt_ref, g_ref, ya_ref, yb_ref, wgate_ref, bgate_ref,
                   wbra_ref, wbrb_ref, wout_ref, o_ref):
    tm, d = x_ref.shape
    x = x_ref[...]
    u = _modnorm(x, g_ref[...], _rows(sh_ref, tm), _rows(sc_ref, tm)).astype(BF16)
    gates = _sigmoid(jnp.dot(u, wgate_ref[...], preferred_element_type=F32) + bgate_ref[...])
    ma = _dot(ya_ref[...], wbra_ref[...])
    mb = _dot(yb_ref[...], wbrb_ref[...])
    merged = gates[:, :d] * ma + gates[:, d:] * mb
    o_ref[...] = x + _rows(gt_ref, tm) * _dot(merged, wout_ref[...])


def _mixout_call(x, grp, mod, layer, norm_g, ya, yb, w_gate, b_gate, w_br_a, w_br_b, w_out):
    rows, d = x.shape
    tm = grp.tm
    wa, wb = ya.shape[1], yb.shape[1]
    row = lambda i: (i, 0)
    const = lambda i: (0, 0)
    m = grp.mod_arg(mod)
    return pl.pallas_call(
        _mixout_kernel,
        grid=(rows // tm,),
        in_specs=[pl.BlockSpec((tm, d), row), grp.mod_spec(d, layer, 3), grp.mod_spec(d, layer, 4),
                  grp.mod_spec(d, layer, 5), pl.BlockSpec((1, d), const),
                  pl.BlockSpec((tm, wa), row), pl.BlockSpec((tm, wb), row),
                  pl.BlockSpec((d, 2 * d), const, **_RESIDENT), pl.BlockSpec((1, 2 * d), const),
                  pl.BlockSpec((wa, d), const, **_RESIDENT), pl.BlockSpec((wb, d), const, **_RESIDENT),
                  pl.BlockSpec((d, d), const, **_RESIDENT)],
        out_specs=pl.BlockSpec((tm, d), row),
        out_shape=jax.ShapeDtypeStruct((rows, d), F32),
        compiler_params=_cparams(1),
        name=f"mixout_{grp.kind}",
    )(x, m, m, m, norm_g.reshape(1, d), ya, yb, w_gate, b_gate.reshape(1, 2 * d), w_br_a, w_br_b, w_out)


def _rwkv_prep(xs, aw, seg_ones, w0, w2, a0, a2, g2, k_k, k_a):
    n_dec, n_aaa = w2.shape[0], a2.shape[0]
    r = xs[:, 0:aw]
    k = xs[:, aw:2 * aw]
    v = xs[:, 2 * aw:3 * aw]
    o = 3 * aw
    wl = xs[:, o:o + n_dec]
    al = xs[:, o + n_dec:o + n_dec + n_aaa]
    gl = xs[:, o + n_dec + n_aaa:]
    logw = -DECAY_SCALE * _sigmoid(w0 + _dot(jnp.tanh(wl), w2))
    asig = _sigmoid(a0 + _dot(al, a2))
    g = _dot(_sigmoid(gl), g2)
    kk = k * k_k
    kk = kk / jnp.maximum(jnp.sqrt(_seg_sum(kk * kk, seg_ones)), 1e-12)
    k = k * (1.0 + (asig - 1.0) * k_a)
    return r, k, v, logw, -kk, kk * asig, g


def _pool_group_sum(read_rows, w):
    s = read_rows(0)
    for j in range(1, w):
        s = s + read_rows(j)
    return s


PROMPT_BLOCK = 256
PROMPT_SEQS = 2
SHIFT_PAD = 8
POOL_PAD = 24


def _pool_window_sums(xbuf, e2, e4, e8, s, tb):
    gw = xbuf.shape[2] // len(POOL_WINDOWS)
    lo, hi = POOL_PAD - 16, POOL_PAD + tb
    e2[s, lo:hi, :] = xbuf[s, lo:hi, :] + xbuf[s, lo - 1:hi - 1, :]
    e4[s, lo:hi, :] = e2[s, lo:hi, gw:] + e2[s, lo - 2:hi - 2, gw:]
    e8[s, lo:hi, :] = e4[s, lo:hi, gw:] + e4[s, lo - 4:hi - 4, gw:]
    blk = slice(POOL_PAD, POOL_PAD + tb)
    back8 = slice(POOL_PAD - 8, POOL_PAD - 8 + tb)
    return (e2[s, blk, 0:gw], e4[s, blk, 0:gw], e8[s, blk, 0:gw], e8[s, blk, gw:] + e8[s, back8, gw:])


def _prompt_mix_kernel(pa_ref, pb_ref, mu_ref, w0_ref, w2_ref, a0_ref, a2_ref, g2_ref, kk_ref, ka_ref, rk_ref,
                       lnw_ref, lnb_ref, wpool_ref, pscale_ref, seg_ref,
                       ya_ref, yb_ref, wkv_ref,
                       pbuf, xbuf, e2, e4, e8, s_sc, r_sc, k_sc, v_sc, a_sc, b_sc, lw_sc, g_sc, y_sc, *, n_blocks):
    ns, tb, aw = ya_ref.shape
    bw = yb_ref.shape[2]
    t = WKV_CHUNK
    n_pairs = aw // LANES
    blk = pl.program_id(1)
    seqs = range(ns)

    @pl.when(blk == 0)
    def _():
        pbuf[:, 0:SHIFT_PAD, :] = jnp.zeros((ns, SHIFT_PAD, pbuf.shape[2]), F32)
        xbuf[:, 0:POOL_PAD, :] = jnp.zeros((ns, POOL_PAD, bw), F32)
        e2[:, 0:POOL_PAD, :] = jnp.zeros((ns, POOL_PAD, e2.shape[2]), F32)
        e4[:, 0:POOL_PAD, :] = jnp.zeros((ns, POOL_PAD, e4.shape[2]), F32)
        s_sc[...] = jnp.zeros_like(s_sc)

    xs = []
    for s in seqs:
        pa = pa_ref[s]
        pbuf[s, SHIFT_PAD:SHIFT_PAD + tb, :] = pa
        prev = pbuf[s, SHIFT_PAD - 1:SHIFT_PAD - 1 + tb, :]
        xs.append(pa + (prev - pa) * mu_ref[...])
        pbuf[s, 0:SHIFT_PAD, :] = pbuf[s, tb:tb + SHIFT_PAD, :]
    xs = jnp.concatenate(xs, axis=0)

    seg_ones = seg_ref[...]
    r, k, v, logw, a, b, g = _rwkv_prep(xs, aw, seg_ones, w0_ref[...], w2_ref[...], a0_ref[...], a2_ref[...],
                                        g2_ref[...], kk_ref[...], ka_ref[...])
    r_sc[...] = r
    k_sc[...] = k
    v_sc[...] = v
    a_sc[...] = a
    b_sc[...] = b
    lw_sc[...] = logw
    g_sc[...] = g

    t2 = 2 * t
    ri = lax.broadcasted_iota(jnp.int32, (t, t), 0)
    ci = lax.broadcasted_iota(jnp.int32, (t, t), 1)
    tri_incl = (ri >= ci).astype(BF16)
    rs = lax.broadcasted_iota(jnp.int32, (t2, t2), 0)
    cs = lax.broadcasted_iota(jnp.int32, (t2, t2), 1)
    strict = cs < rs
    rs2 = lax.broadcasted_iota(jnp.int32, (t2, 2 * t2), 0)
    cs2 = lax.broadcasted_iota(jnp.int32, (t2, 2 * t2), 1)
    incl2 = jnp.where(cs2 >= t2, cs2 - t2, cs2) <= rs2
    lane = lax.broadcasted_iota(jnp.int32, (1, LANES), 1)
    head_mask = (lane < HEAD_DIM, lane >= HEAD_DIM)
    units = [(s, p) for s in seqs for p in range(n_pairs)]
    n_levels = int(math.log2(t))

    def stack(x):
        return jnp.concatenate([jnp.where(head_mask[0], x, 0.0), jnp.where(head_mask[1], x, 0.0)], axis=0)

    def chunk(ci_, carry):
        at, rt, bt, kt, vv, p_end, sls = [], [], [], [], [], [], []
        for s in seqs:
            sl = pl.ds(pl.multiple_of(s * tb + ci_ * t, t), t)
            lw = lw_sc[sl, :]
            hi = lw.astype(BF16)
            lo = (lw - hi.astype(F32)).astype(BF16)
            c = (jnp.dot(tri_incl, hi, preferred_element_type=F32)
                 + jnp.dot(tri_incl, lo, preferred_element_type=F32))
            e_in = jnp.exp(c)
            e_neg = jnp.exp(-c)
            p_end.append(e_in[t - 1:t, :])
            at.append(a_sc[sl, :] * jnp.exp(c - lw))
            rt.append(r_sc[sl, :] * e_in)
            bt.append(b_sc[sl, :] * e_neg)
            kt.append(k_sc[sl, :] * e_neg)
            vv.append(v_sc[sl, :])
            sls.append(sl)
        cols = [slice(p * LANES, (p + 1) * LANES) for p in range(n_pairs)]
        un = range(len(units))
        pe = [p_end[s][:, cols[p]] for s, p in units]
        a2 = [stack(at[s][:, cols[p]]).astype(BF16) for s, p in units]
        r2 = [stack(rt[s][:, cols[p]]).astype(BF16) for s, p in units]
        v2b = [stack(vv[s][:, cols[p]]).astype(BF16) for s, p in units]
        bk = [jnp.concatenate([stack(bt[s][:, cols[p]]), stack(kt[s][:, cols[p]])], axis=0) for s, p in units]
        g_ = [_dot_nt(jnp.concatenate([a2[i], r2[i]], axis=0), bk[i]) for i in un]
        n = [jnp.where(strict, g_[i][:t2, :t2], 0.0) for i in un]
        ak = [jnp.where(strict, g_[i][:t2, t2:], 0.0).astype(BF16) for i in un]
        rbk = [jnp.where(incl2, g_[i][t2:, :], 0.0).astype(BF16) for i in un]
        xm = n
        pw = [_dot(n[i], n[i]) for i in un]
        for _ in range(n_levels - 2):
            res = [_dot(jnp.concatenate([pw[i], xm[i]], axis=0), pw[i]) for i in un]
            xm = [xm[i] + pw[i] + res[i][t2:] for i in un]
            pw = [res[i][:t2] for i in un]
        res = [_dot(xm[i], pw[i]) for i in un]
        xm = [(xm[i] + pw[i] + res[i]).astype(BF16) for i in un]
        s_old = [s_sc[s * n_pairs + p] for s, p in units]
        s_bf = [s_old[i].astype(BF16) for i in un]
        rhs = [_dot_nt(a2[i], s_bf[i]) for i in un]
        akv = [jnp.dot(ak[i], v2b[i], preferred_element_type=F32) for i in un]
        rhs = [rhs[i] + akv[i] for i in un]
        u = [rhs[i] + _dot(xm[i], rhs[i]) for i in un]
        uv = [jnp.concatenate([u[i].astype(BF16), v2b[i]], axis=0) for i in un]
        ys = [_dot_nt(r2[i], s_bf[i]) for i in un]
        yi = [jnp.dot(rbk[i], uv[i], preferred_element_type=F32) for i in un]
        sn = [_dot_tn(uv[i], bk[i] * pe[i]) for i in un]
        for i, (s, p) in enumerate(units):
            y2 = ys[i] + yi[i]
            y_sc[sls[s], cols[p]] = y2[:t] + y2[t:]
            s_sc[s * n_pairs + p] = s_old[i] * pe[i] + sn[i]
        return carry

    lax.fori_loop(0, tb // t, chunk, 0)

    y = y_sc[...]
    inv_n = 1.0 / HEAD_DIM
    mean = _seg_sum(y, seg_ones) * inv_n
    yc = y - mean
    var = _seg_sum(yc * yc, seg_ones) * inv_n
    yn = yc * lax.rsqrt(var + GN_EPS) * lnw_ref[...] + lnb_ref[...]
    bonus = _seg_sum(r_sc[...] * k_sc[...] * rk_ref[...], seg_ones) * v_sc[...]
    ya = (yn + bonus) * g_sc[...]
    for s in seqs:
        ya_ref[s] = ya[s * tb:(s + 1) * tb]

    pos = blk * tb + lax.broadcasted_iota(jnp.int32, (tb, 1), 0)
    gw = bw // len(POOL_WINDOWS)
    for s in seqs:
        pb = pb_ref[s]
        xbuf[s, POOL_PAD:POOL_PAD + tb, :] = pb
        sums = _pool_window_sums(xbuf, e2, e4, e8, s, tb)
        for gi, w in enumerate(POOL_WINDOWS):
            cols = slice(gi * gw, (gi + 1) * gw)
            cnt = jnp.minimum(w, pos + 1).astype(F32)
            dlt = sums[gi] / cnt - pb[:, cols]
            yb_ref[s, :, cols] = _dot(dlt, wpool_ref[gi]) * pscale_ref[:, cols]
        xbuf[s, POOL_PAD - 16:POOL_PAD, :] = xbuf[s, tb + POOL_PAD - 16:tb + POOL_PAD, :]

    @pl.when(blk == n_blocks - 1)
    def _():
        for s in seqs:
            for p in range(n_pairs):
                s_p = s_sc[s * n_pairs + p]
                for h in range(2):
                    o = h * HEAD_DIM
                    wkv_ref[s, 2 * p + h] = s_p[o:o + HEAD_DIM, o:o + HEAD_DIM]


def _prompt_mix_call(pa, pb, batch, seq, prm):
    na = pa.shape[1]
    bw = pb.shape[1]
    aw = prm["w0"].shape[1]
    n_heads = aw // HEAD_DIM
    tb, ns = min(PROMPT_BLOCK, seq), PROMPT_SEQS
    assert seq % tb == 0 and tb % WKV_CHUNK == 0 and aw % LANES == 0 and batch % ns == 0
    assert len(POOL_WINDOWS) == 4 and POOL_HIST < 16 <= POOL_PAD - 8
    n_blocks = seq // tb
    gw = bw // len(POOL_WINDOWS)
    row = lambda b, c: (b, c, 0)
    const2 = lambda b, c: (0, 0)
    const3 = lambda b, c: (0, 0, 0)
    full = lambda a: pl.BlockSpec(a.shape, const2 if a.ndim == 2 else const3)
    names = ("mu", "w0", "w2", "a0", "a2", "g2", "k_k", "k_a", "r_k", "ln_w", "ln_b", "w_pool", "pool_scale", "seg")
    params = [prm[n] for n in names]
    blk = lambda: pltpu.VMEM((ns * tb, aw), F32)
    ya, yb, wkv = pl.pallas_call(
        functools.partial(_prompt_mix_kernel, n_blocks=n_blocks),
        grid=(batch // ns, n_blocks),
        in_specs=[pl.BlockSpec((ns, tb, na), row), pl.BlockSpec((ns, tb, bw), row)] + [full(a) for a in params],
        out_specs=[pl.BlockSpec((ns, tb, aw), row), pl.BlockSpec((ns, tb, bw), row),
                   pl.BlockSpec((ns, n_heads, HEAD_DIM, HEAD_DIM), lambda b, c: (b, 0, 0, 0))],
        out_shape=[jax.ShapeDtypeStruct((batch, seq, aw), F32), jax.ShapeDtypeStruct((batch, seq, bw), F32),
                   jax.ShapeDtypeStruct((batch, n_heads, HEAD_DIM, HEAD_DIM), F32)],
        scratch_shapes=[pltpu.VMEM((ns, tb + SHIFT_PAD, na), F32), pltpu.VMEM((ns, tb + POOL_PAD, bw), F32),
                        pltpu.VMEM((ns, tb + POOL_PAD, bw), F32), pltpu.VMEM((ns, tb + POOL_PAD, bw - gw), F32),
                        pltpu.VMEM((ns, tb + POOL_PAD, bw - 2 * gw), F32),
                        pltpu.VMEM((ns * (aw // LANES), LANES, LANES), F32),
                        blk(), blk(), blk(), blk(), blk(), blk(), blk(), blk()],
        compiler_params=_cparams(2),
        name="mix_prompt",
    )(pa.reshape(batch, seq, na), pb.reshape(batch, seq, bw), *params)
    return ya.reshape(batch * seq, aw), yb.reshape(batch * seq, bw), wkv


def _sample_prep_kernel(pa_ref, sh0_ref, pb_ref, hist_ref, mu_ref, w0_ref, w2_ref, a0_ref, a2_ref, g2_ref,
                        kk_ref, ka_ref, wpool_ref, pscale_ref, seg_ref,
                        r_ref, w_ref, k_ref, v_ref, a_ref, b_ref, g_ref, yb_ref, *, batch, seq):
    aw = r_ref.shape[1]
    bw = yb_ref.shape[1]
    pa = pa_ref[...]
    prev = jnp.concatenate([sh0_ref[...], pa[:(seq - 1) * batch]], axis=0)
    xs = pa + (prev - pa) * mu_ref[...]
    r, k, v, logw, a, b, g = _rwkv_prep(xs, aw, seg_ref[...], w0_ref[...], w2_ref[...], a0_ref[...], a2_ref[...],
                                        g2_ref[...], kk_ref[...], ka_ref[...])
    outs = ((r_ref, r), (w_ref, jnp.exp(logw)), (k_ref, k), (v_ref, v), (a_ref, a), (b_ref, b), (g_ref, g))
    for ref, val in outs:
        for ti in range(seq):
            ref[ti] = val[ti * batch:(ti + 1) * batch, :].T

    gw = bw // len(POOL_WINDOWS)

    def slab(i, cols):
        if i < POOL_HIST:
            return hist_ref[i * batch:(i + 1) * batch, cols]
        return pb_ref[(i - POOL_HIST) * batch:(i - POOL_HIST + 1) * batch, cols]

    for ti in range(seq):
        rows = slice(ti * batch, (ti + 1) * batch)
        for gi, w in enumerate(POOL_WINDOWS):
            cols = slice(gi * gw, (gi + 1) * gw)
            s = _pool_group_sum(lambda j: slab(POOL_HIST + ti - j, cols), w)
            cnt = float(min(w, PAST_LEN + ti + 1))
            dlt = s / cnt - pb_ref[rows, cols]
            yb_ref[rows, cols] = _dot(dlt, wpool_ref[gi]) * pscale_ref[:, cols]


def _sample_prep_call(pa, sh0, pb, hist, batch, seq, prm):
    rows = pa.shape[0]
    aw = prm["w0"].shape[1]
    bw = pb.shape[1]
    names = ("mu", "w0", "w2", "a0", "a2", "g2", "k_k", "k_a", "w_pool", "pool_scale", "seg")
    args = [pa, sh0, pb, hist] + [prm[n] for n in names]
    full = lambda a: pl.BlockSpec(a.shape, (lambda i: (0, 0)) if a.ndim == 2 else (lambda i: (0, 0, 0)))
    o = jax.ShapeDtypeStruct((seq, aw, batch), F32)
    ob = jax.ShapeDtypeStruct((rows, bw), F32)
    return pl.pallas_call(
        functools.partial(_sample_prep_kernel, batch=batch, seq=seq),
        grid=(1,),
        in_specs=[full(a) for a in args],
        out_specs=[full(o)] * 7 + [full(ob)],
        out_shape=[o] * 7 + [ob],
        compiler_params=_cparams(1),
        name="prep_sample",
    )(*args)


SAMPLE_STEPS_PER_PAIR = 8


def _sample_wkv_kernel(r_ref, w_ref, k_ref, v_ref, a_ref, b_ref, g_ref, rk_ref, lnw_ref, lnb_ref, s_ref,
                       y_ref, so_ref, y_sc):
    seq, _, batch = r_ref.shape
    hd = HEAD_DIM
    nblk = s_ref.shape[1] // LANES
    rows_per_blk = LANES // hd
    st = pl.program_id(1)
    steps_per_head = SAMPLE_STEPS_PER_PAIR // 2
    hrow = pl.multiple_of((st // steps_per_head) * hd, hd)
    vbase = hrow + (st % steps_per_head) * (nblk * rows_per_blk)
    hs = pl.ds(hrow, hd)
    for jb in range(nblk):
        cols = slice(jb * LANES, (jb + 1) * LANES)
        tl = s_ref[:, cols].T
        tiles = [tl[q * hd:(q + 1) * hd] for q in range(rows_per_blk)]
        for ti in range(seq):
            for q in range(rows_per_blk):
                vi = vbase + jb * rows_per_blk + q
                sa = jnp.sum(tiles[q] * a_ref[ti, hs, :], axis=0, keepdims=True)
                tiles[q] = (tiles[q] * w_ref[ti, hs, :] + sa * b_ref[ti, hs, :]
                            + v_ref[ti, pl.ds(vi, 1), :] * k_ref[ti, hs, :])
                y_sc[ti, pl.ds(vi, 1), :] = jnp.sum(tiles[q] * r_ref[ti, hs, :], axis=0, keepdims=True)
        so_ref[:, cols] = jnp.concatenate(tiles, axis=0).T

    @pl.when(st == SAMPLE_STEPS_PER_PAIR - 1)
    def _():
        for ti in range(seq):
            outs = []
            for hh in range(2):
                rows = slice(hh * hd, (hh + 1) * hd)
                y = y_sc[ti, rows, :]
                yc = y - jnp.mean(y, axis=0, keepdims=True)
                var = jnp.mean(yc * yc, axis=0, keepdims=True)
                yn = yc * lax.rsqrt(var + GN_EPS) * lnw_ref[rows, :] + lnb_ref[rows, :]
                rk = r_ref[ti, rows, :] * k_ref[ti, rows, :] * rk_ref[rows, :]
                bonus = jnp.sum(rk, axis=0, keepdims=True) * v_ref[ti, rows, :]
                outs.append((yn + bonus) * g_ref[ti, rows, :])
            y_ref[ti * batch:(ti + 1) * batch, :] = jnp.concatenate(outs, axis=0).T


def _sample_wkv_call(vecs, state, batch, seq, prm):
    n_heads = state.shape[1]
    aw = n_heads * HEAD_DIM
    pair_cols = 2 * HEAD_DIM * HEAD_DIM
    assert 2 * HEAD_DIM == LANES and pair_cols % (SAMPLE_STEPS_PER_PAIR * LANES) == 0
    step_cols = pair_cols // SAMPLE_STEPS_PER_PAIR
    vspec = pl.BlockSpec((seq, LANES, batch), lambda p, s: (0, p, 0))
    cspec = pl.BlockSpec((LANES, 1), lambda p, s: (p, 0))
    sspec = pl.BlockSpec((batch, step_cols), lambda p, s: (0, p * SAMPLE_STEPS_PER_PAIR + s))
    cols = [prm[n].reshape(aw, 1) for n in ("r_k", "ln_w", "ln_b")]
    y, s_out = pl.pallas_call(
        _sample_wkv_kernel,
        grid=(n_heads // 2, SAMPLE_STEPS_PER_PAIR),
        in_specs=[vspec] * 7 + [cspec] * 3 + [sspec],
        out_specs=[pl.BlockSpec((seq * batch, LANES), lambda p, s: (0, p)), sspec],
        out_shape=[jax.ShapeDtypeStruct((seq * batch, aw), F32),
                   jax.ShapeDtypeStruct((batch, n_heads * HEAD_DIM * HEAD_DIM), F32)],
        scratch_shapes=[pltpu.VMEM((seq, LANES, batch), F32)],
        compiler_params=_cparams(2),
        name="wkv_sample",
    )(*vecs, *cols, state.reshape(batch, n_heads * HEAD_DIM * HEAD_DIM))
    return y, s_out.reshape(state.shape)


def kernel(x_prompt, x_sample, state_wkv, state_shift, state_pool, c_prompt, c_sample, norm_g, w_mod, b_mod, w_ffn_in, w_ffn_out, w_in, mu_shift, w0, w2, a0, a2, g2, k_k, k_a, r_k, ln_x_w, ln_x_b, w_pool, pool_scale, w_br_a, w_br_b, w_gate, b_gate, w_out, final_g):
    bp, lp, d = x_prompt.shape
    bs, ls, _ = x_sample.shape
    depth = w_mod.shape[0]
    aw = w0.shape[1]
    a_proj = mu_shift.shape[1]
    bw = pool_scale.shape[1]

    bf = lambda w: w.astype(BF16)
    w_ffn_in_b, w_ffn_out_b = bf(w_ffn_in), bf(w_ffn_out)
    w_in_a, w_in_b = bf(w_in[:, :, :a_proj]), bf(w_in[:, :, a_proj:])
    w_gate_b, w_br_a_b, w_br_b_b, w_out_b = bf(w_gate), bf(w_br_a), bf(w_br_b), bf(w_out)
    w2_b, a2_b, g2_b, w_pool_b = bf(w2), bf(a2), bf(g2), bf(w_pool)
    head_of = jnp.arange(MXU_TILE, dtype=jnp.int32) // HEAD_DIM
    seg_ones = (head_of[:, None] == head_of[None, :]).astype(BF16)

    mod = _mod_call(jnp.concatenate([c_sample, c_prompt], axis=0), w_mod, b_mod)
    gp = _Group("prompt", bp, lp, min(1024, lp), min(1024, lp), mod_row0=bs)
    gs = _Group("sample", bs, ls, bs * ls, bs * ls, mod_row0=0)

    xp = x_prompt.reshape(bp * lp, d)
    xs = jnp.transpose(x_sample, (1, 0, 2)).reshape(ls * bs, d)
    hist_s = jnp.transpose(state_pool, (0, 2, 1, 3)).reshape(depth, POOL_HIST * bs, bw)

    wkv_p, shift_p, pool_p, wkv_s, shift_s, pool_s = [], [], [], [], [], []
    for l in range(depth):
        prm = dict(mu=mu_shift[l][None], w0=w0[l][None], w2=w2_b[l], a0=a0[l][None], a2=a2_b[l], g2=g2_b[l],
                   k_k=k_k[l][None], k_a=k_a[l][None], r_k=r_k[l].reshape(1, aw), ln_w=ln_x_w[l][None],
                   ln_b=ln_x_b[l][None], w_pool=w_pool_b[l], pool_scale=pool_scale[l][None], seg=seg_ones)
        last = l == depth - 1

        xp = _ffn_call(xp, gp, mod, l, 0, norm_g[l, 0], w_ffn_in_b[l, 0], w_ffn_out_b[l, 0])
        pa, pb = _mixin_call(xp, gp, mod, l, norm_g[l, 1], w_in_a[l], w_in_b[l])
        ya, yb, s_new = _prompt_mix_call(pa, pb, bp, lp, prm)
        xp = _mixout_call(xp, gp, mod, l, norm_g[l, 1], ya, yb, w_gate_b[l], b_gate[l], w_br_a_b[l], w_br_b_b[l],
                          w_out_b[l])
        xp = _ffn_call(xp, gp, mod, l, 2, norm_g[l, 2], w_ffn_in_b[l, 1], w_ffn_out_b[l, 1],
                       final_g if last else None)
        wkv_p.append(s_new)
        shift_p.append(pa.reshape(bp, lp, a_proj)[:, -1:])
        pool_p.append(pb.reshape(bp, lp, bw)[:, -POOL_HIST:])

        xs = _ffn_call(xs, gs, mod, l, 0, norm_g[l, 0], w_ffn_in_b[l, 0], w_ffn_out_b[l, 0])
        pa, pb = _mixin_call(xs, gs, mod, l, norm_g[l, 1], w_in_a[l], w_in_b[l])
        *vecs, yb = _sample_prep_call(pa, state_shift[l, :, 0, :], pb, hist_s[l], bs, ls, prm)
        ya, s_new = _sample_wkv_call(vecs, state_wkv[l], bs, ls, prm)
        xs = _mixout_call(xs, gs, mod, l, norm_g[l, 1], ya, yb, w_gate_b[l], b_gate[l], w_br_a_b[l], w_br_b_b[l],
                          w_out_b[l])
        xs = _ffn_call(xs, gs, mod, l, 2, norm_g[l, 2], w_ffn_in_b[l, 1], w_ffn_out_b[l, 1],
                       final_g if last else None)
        wkv_s.append(s_new)
        shift_s.append(pa.reshape(ls, bs, a_proj)[-1][:, None, :])
        pb_bm = jnp.transpose(pb.reshape(ls, bs, bw), (1, 0, 2))
        pool_s.append(jnp.concatenate([state_pool[l], pb_bm], axis=1)[:, -POOL_HIST:])

    y_prompt = xp.reshape(bp, lp, d)
    y_sample = jnp.transpose(xs.reshape(ls, bs, d), (1, 0, 2))
    return (y_prompt, y_sample, jnp.stack(wkv_p), jnp.stack(shift_p), jnp.stack(pool_p),
            jnp.stack(wkv_s), jnp.stack(shift_s), jnp.stack(pool_s))
```

```python
import functools
import math

import jax
import jax.numpy as jnp
from jax import lax
from jax.experimental import pallas as pl
from jax.experimental.pallas import tpu as pltpu

F32 = jnp.float32
BF16 = jnp.bfloat16

HEAD_DIM = 64
POOL_WINDOWS = (2, 4, 8, 16)
POOL_HIST = max(POOL_WINDOWS) - 1
N_SUB = 3
NORM_EPS = 1e-6
GN_EPS = 64e-5
PAST_LEN = 16384
DECAY_SCALE = math.exp(-0.5)

LANES = 128
MXU_TILE = 256
WKV_CHUNK = 64
VMEM_LIMIT_BYTES = 56 * 1024 * 1024


_RESIDENT = dict(pipeline_mode=pl.Buffered(1))


def _cparams(n_axes):
    return pltpu.CompilerParams(dimension_semantics=("arbitrary",) * n_axes,
                                vmem_limit_bytes=VMEM_LIMIT_BYTES)


def _sigmoid(x):
    return 1.0 / (1.0 + jnp.exp(-x))


def _dot(a, b):
    return jnp.dot(a.astype(BF16), b.astype(BF16), preferred_element_type=F32)


def _dot_nt(a, b):
    return lax.dot_general(a.astype(BF16), b.astype(BF16), (((1,), (1,)), ((), ())),
                           preferred_element_type=F32)


def _dot_tn(a, b):
    return lax.dot_general(a.astype(BF16), b.astype(BF16), (((0,), (0,)), ((), ())),
                           preferred_element_type=F32)


def _seg_sum(x, seg_ones):
    n, width = x.shape
    tw = seg_ones.shape[0]
    outs = []
    for c0 in range(0, width, tw):
        xc = x[:, c0:c0 + tw]
        hi = xc.astype(BF16)
        lo = (xc - hi.astype(F32)).astype(BF16)
        both = jnp.dot(jnp.concatenate([hi, lo], axis=0), seg_ones, preferred_element_type=F32)
        outs.append(both[:n] + both[n:])
    return jnp.concatenate(outs, axis=1)


def _rms(x):
    return x * lax.rsqrt(jnp.mean(x * x, axis=-1, keepdims=True) + NORM_EPS)


SUBLANES = 8


def _rows(ref, tm, seq_rows):
    if seq_rows is not None:
        row0, per_seq = seq_rows
        return ref[pl.ds((row0 + pl.program_id(0) // per_seq) % SUBLANES, 1), :]
    m = ref[...]
    return m if m.shape[0] == tm else jnp.concatenate([m] * (tm // m.shape[0]), axis=0)


def _modnorm(x, g, shift, scale):
    return _rms(x) * g * (1.0 + scale) + shift


def _mod_kernel(c_ref, w_ref, b_ref, o_ref):
    c = c_ref[...]
    o_ref[...] = _dot(c * _sigmoid(c), w_ref[...]) + b_ref[...]


def _mod_call(c_all, w_mod, b_mod):
    depth, d, n9 = w_mod.shape
    rows = c_all.shape[0]
    n_sub9 = n9 // d
    tn = d
    return pl.pallas_call(
        _mod_kernel,
        grid=(depth, n_sub9),
        in_specs=[pl.BlockSpec((rows, d), lambda l, j: (0, 0)),
                  pl.BlockSpec((None, d, tn), lambda l, j: (l, 0, j)),
                  pl.BlockSpec((None, 1, tn), lambda l, j: (l, 0, j))],
        out_specs=pl.BlockSpec((None, None, rows, tn), lambda l, j: (l, j, 0, 0)),
        out_shape=jax.ShapeDtypeStruct((depth, n_sub9, rows, d), F32),
        compiler_params=_cparams(2),
        name="mod",
    )(c_all, w_mod, b_mod.reshape(depth, 1, n9))


class _Group:
    def __init__(self, kind, batch, seq, tm, ffn_tm, mod_row0):
        self.kind, self.batch, self.seq, self.tm, self.ffn_tm, self.mod_row0 = kind, batch, seq, tm, ffn_tm, mod_row0
        self.rows = batch * seq
        for t in (tm, ffn_tm):
            assert self.rows % t == 0
            if kind == "prompt":
                assert seq % t == 0

    def seq_rows(self, tm=None):
        return (self.mod_row0, self.seq // (tm or self.tm)) if self.kind == "prompt" else None

    def mod_spec(self, d, layer, idx, tm=None):
        if self.kind == "prompt":
            row0, per_seq = self.seq_rows(tm)
            return pl.BlockSpec((None, None, SUBLANES, d),
                                lambda i, *_: (layer, idx, (row0 + i // per_seq) // SUBLANES, 0))
        return pl.BlockSpec((None, None, self.batch, d), lambda i, *_: (layer, idx, 0, 0))


FFN_CHUNK = 2 * MXU_TILE


def _ffn_kernel(x_ref, sh_ref, sc_ref, gt_ref, g_ref, win_ref, wo_ref, *rest, final, seq_rows):
    if final:
        fg_ref, o_ref, acc_sc = rest
    else:
        o_ref, acc_sc = rest
    tm = x_ref.shape[0]
    d_ff = wo_ref.shape[0]
    x = x_ref[...]
    u = _modnorm(x, g_ref[...], _rows(sh_ref, tm, seq_rows), _rows(sc_ref, tm, seq_rows)).astype(BF16)
    for c0 in range(0, d_ff, FFN_CHUNK):
        c1 = min(c0 + FFN_CHUNK, d_ff)
        hg = jnp.dot(u, win_ref[:, c0:c1], preferred_element_type=F32)
        hu = jnp.dot(u, win_ref[:, d_ff + c0:d_ff + c1], preferred_element_type=F32)
        act = (hg * _sigmoid(hg) * hu).astype(BF16)
        part = jnp.dot(act, wo_ref[c0:c1, :], preferred_element_type=F32)
        if c0 == 0:
            acc_sc[...] = part
        else:
            acc_sc[...] += part
    y = x + 0.5 * _rows(gt_ref, tm, seq_rows) * acc_sc[...]
    if final:
        y = _rms(y) * fg_ref[...]
    o_ref[...] = y


def _ffn_call(x, grp, mod, layer, sub, norm_g, w_in, w_out, final_g=None):
    rows, d = x.shape
    d_ff = w_out.shape[2]
    assert d_ff % LANES == 0 and FFN_CHUNK % LANES == 0
    tm = grp.ffn_tm
    final = final_g is not None
    row = lambda i: (i, 0)
    const = lambda i: (0, 0)
    half = lambda i: (layer, sub // 2, 0, 0)
    in_specs = [pl.BlockSpec((tm, d), row),
                grp.mod_spec(d, layer, 3 * sub, tm), grp.mod_spec(d, layer, 3 * sub + 1, tm),
                grp.mod_spec(d, layer, 3 * sub + 2, tm),
                pl.BlockSpec((1, d), const),
                pl.BlockSpec((None, None, d, 2 * d_ff), half, **_RESIDENT),
                pl.BlockSpec((None, None, d_ff, d), half, **_RESIDENT)]
    args = [x, mod, mod, mod, norm_g.reshape(1, d), w_in, w_out]
    if final:
        in_specs.append(pl.BlockSpec((1, d), const))
        args.append(final_g.reshape(1, d))
    return pl.pallas_call(
        functools.partial(_ffn_kernel, final=final, seq_rows=grp.seq_rows(tm)),
        grid=(rows // tm,),
        in_specs=in_specs,
        out_specs=pl.BlockSpec((tm, d), row),
        out_shape=jax.ShapeDtypeStruct((rows, d), F32),
        scratch_shapes=[pltpu.VMEM((tm, d), F32)],
        compiler_params=_cparams(1),
        name=f"ffn_{grp.kind}",
    )(*args)


def _mixin_kernel(x_ref, sh_ref, sc_ref, g_ref, wa_ref, wb_ref, pa_ref, pb_ref, *, seq_rows):
    tm = x_ref.shape[0]
    u = _modnorm(x_ref[...], g_ref[...], _rows(sh_ref, tm, seq_rows), _rows(sc_ref, tm, seq_rows)).astype(BF16)
    pa_ref[...] = jnp.dot(u, wa_ref[...], preferred_element_type=F32)
    pb_ref[...] = jnp.dot(u, wb_ref[...], preferred_element_type=F32)


def _mixin_call(x, grp, mod, layer, norm_g, w_a, w_b):
    rows, d = x.shape
    tm = grp.tm
    na, nb = w_a.shape[1], w_b.shape[1]
    row = lambda i: (i, 0)
    const = lambda i: (0, 0)
    m = mod
    return pl.pallas_call(
        functools.partial(_mixin_kernel, seq_rows=grp.seq_rows()),
        grid=(rows // tm,),
        in_specs=[pl.BlockSpec((tm, d), row), grp.mod_spec(d, layer, 3), grp.mod_spec(d, layer, 4),
                  pl.BlockSpec((1, d), const), pl.BlockSpec((d, na), const, **_RESIDENT),
                  pl.BlockSpec((d, nb), const, **_RESIDENT)],
        out_specs=[pl.BlockSpec((tm, na), row), pl.BlockSpec((tm, nb), row)],
        out_shape=[jax.ShapeDtypeStruct((rows, na), F32), jax.ShapeDtypeStruct((rows, nb), F32)],
        compiler_params=_cparams(1),
        name=f"mixin_{grp.kind}",
    )(x, m, m, norm_g.reshape(1, d), w_a, w_b)


def _mixout_kernel(x_ref, sh_ref, sc_ref, gt_ref, g_ref, ya_ref, yb_ref, wgate_ref, bgate_ref,
                   wbra_ref, wbrb_ref, wout_ref, o_ref, *, seq_rows):
    tm, d = x_ref.shape
    x = x_ref[...]
    u = _modnorm(x, g_ref[...], _rows(sh_ref, tm, seq_rows), _rows(sc_ref, tm, seq_rows)).astype(BF16)
    gates = _sigmoid(jnp.dot(u, wgate_ref[...], preferred_element_type=F32) + bgate_ref[...])
    ma = _dot(ya_ref[...], wbra_ref[...])
    mb = _dot(yb_ref[...], wbrb_ref[...])
    merged = gates[:, :d] * ma + gates[:, d:] * mb
    o_ref[...] = x + _rows(gt_ref, tm, seq_rows) * _dot(merged, wout_ref[...])


def _mixout_call(x, grp, mod, layer, norm_g, ya, yb, w_gate, b_gate, w_br_a, w_br_b, w_out):
    rows, d = x.shape
    tm = grp.tm
    wa, wb = ya.shape[1], yb.shape[1]
    row = lambda i: (i, 0)
    const = lambda i: (0, 0)
    lyr = lambda i: (layer, 0, 0)
    m = mod
    return pl.pallas_call(
        functools.partial(_mixout_kernel, seq_rows=grp.seq_rows()),
        grid=(rows // tm,),
        in_specs=[pl.BlockSpec((tm, d), row), grp.mod_spec(d, layer, 3), grp.mod_spec(d, layer, 4),
                  grp.mod_spec(d, layer, 5), pl.BlockSpec((1, d), const),
                  pl.BlockSpec((tm, wa), row), pl.BlockSpec((tm, wb), row),
                  pl.BlockSpec((None, d, 2 * d), lyr, **_RESIDENT), pl.BlockSpec((None, 1, 2 * d), lyr),
                  pl.BlockSpec((None, wa, d), lyr, **_RESIDENT), pl.BlockSpec((None, wb, d), lyr, **_RESIDENT),
                  pl.BlockSpec((None, d, d), lyr, **_RESIDENT)],
        out_specs=pl.BlockSpec((tm, d), row),
        out_shape=jax.ShapeDtypeStruct((rows, d), F32),
        compiler_params=_cparams(1),
        name=f"mixout_{grp.kind}",
    )(x, m, m, m, norm_g.reshape(1, d), ya, yb, w_gate, b_gate.reshape(-1, 1, 2 * d), w_br_a, w_br_b, w_out)


def _rwkv_prep(xs, aw, seg_ones, w0, w2, a0, a2, g2, k_k, k_a):
    n_dec, n_aaa = w2.shape[0], a2.shape[0]
    r = xs[:, 0:aw]
    k = xs[:, aw:2 * aw]
    v = xs[:, 2 * aw:3 * aw]
    o = 3 * aw
    wl = xs[:, o:o + n_dec]
    al = xs[:, o + n_dec:o + n_dec + n_aaa]
    gl = xs[:, o + n_dec + n_aaa:]
    logw = -DECAY_SCALE * _sigmoid(w0 + _dot(jnp.tanh(wl), w2))
    asig = _sigmoid(a0 + _dot(al, a2))
    g = _dot(_sigmoid(gl), g2)
    kk = k * k_k
    kk = kk * lax.rsqrt(jnp.maximum(_seg_sum(kk * kk, seg_ones), 1e-24))
    k = k * (1.0 + (asig - 1.0) * k_a)
    return r, k, v, logw, -kk, kk * asig, g


def _pool_group_sum(read_rows, w):
    s = read_rows(0)
    for j in range(1, w):
        s = s + read_rows(j)
    return s


PROMPT_BLOCK = 128
PROMPT_SEQS = 4
SHIFT_PAD = 8
POOL_PAD = 24


def _pool_window_sums(xbuf, e2, e4, e8, s, tb):
    gw = xbuf.shape[2] // len(POOL_WINDOWS)
    lo, hi = POOL_PAD - 16, POOL_PAD + tb
    e2[s, lo:hi, :] = xbuf[s, lo:hi, :] + xbuf[s, lo - 1:hi - 1, :]
    e4[s, lo:hi, :] = e2[s, lo:hi, gw:] + e2[s, lo - 2:hi - 2, gw:]
    e8[s, lo:hi, :] = e4[s, lo:hi, gw:] + e4[s, lo - 4:hi - 4, gw:]
    blk = slice(POOL_PAD, POOL_PAD + tb)
    back8 = slice(POOL_PAD - 8, POOL_PAD - 8 + tb)
    return (e2[s, blk, 0:gw], e4[s, blk, 0:gw], e8[s, blk, 0:gw], e8[s, blk, gw:] + e8[s, back8, gw:])


def _prompt_mix_kernel(pa_ref, pb_ref, mu_ref, w0_ref, w2_ref, a0_ref, a2_ref, g2_ref, kk_ref, ka_ref, rk_ref,
                       lnw_ref, lnb_ref, wpool_ref, pscale_ref, seg_ref,
                       ya_ref, yb_ref, wkv_ref,
                       pbuf, xbuf, e2, e4, e8, s_sc, r_sc, k_sc, v_sc, a_sc, b_sc, lw_sc, g_sc, y_sc, *, n_blocks):
    ns, tb, aw = ya_ref.shape
    bw = yb_ref.shape[2]
    t = WKV_CHUNK
    n_pairs = aw // LANES
    blk = pl.program_id(1)
    seqs = range(ns)

    @pl.when(blk == 0)
    def _():
        pbuf[:, 0:SHIFT_PAD, :] = jnp.zeros((ns, SHIFT_PAD, pbuf.shape[2]), F32)
        xbuf[:, 0:POOL_PAD, :] = jnp.zeros((ns, POOL_PAD, bw), F32)
        e2[:, 0:POOL_PAD, :] = jnp.zeros((ns, POOL_PAD, e2.shape[2]), F32)
        e4[:, 0:POOL_PAD, :] = jnp.zeros((ns, POOL_PAD, e4.shape[2]), F32)
        s_sc[...] = jnp.zeros_like(s_sc)

    xs = []
    for s in seqs:
        pa = pa_ref[s]
        pbuf[s, SHIFT_PAD:SHIFT_PAD + tb, :] = pa
        prev = pbuf[s, SHIFT_PAD - 1:SHIFT_PAD - 1 + tb, :]
        xs.append(pa + (prev - pa) * mu_ref[...])
        pbuf[s, 0:SHIFT_PAD, :] = pbuf[s, tb:tb + SHIFT_PAD, :]
    xs = jnp.concatenate(xs, axis=0)

    seg_ones = seg_ref[...]
    r, k, v, logw, a, b, g = _rwkv_prep(xs, aw, seg_ones, w0_ref[...], w2_ref[...], a0_ref[...], a2_ref[...],
                                        g2_ref[...], kk_ref[...], ka_ref[...])
    r_sc[...] = r
    k_sc[...] = k
    v_sc[...] = v
    a_sc[...] = a
    b_sc[...] = b
    lw_sc[...] = logw
    g_sc[...] = g

    t2 = 2 * t
    ri = lax.broadcasted_iota(jnp.int32, (t, t), 0)
    ci = lax.broadcasted_iota(jnp.int32, (t, t), 1)
    tri_incl = (ri >= ci).astype(BF16)
    rs = lax.broadcasted_iota(jnp.int32, (t2, t2), 0)
    cs = lax.broadcasted_iota(jnp.int32, (t2, t2), 1)
    strict = cs < rs
    rs2 = lax.broadcasted_iota(jnp.int32, (t2, 2 * t2), 0)
    cs2 = lax.broadcasted_iota(jnp.int32, (t2, 2 * t2), 1)
    incl2 = jnp.where(cs2 >= t2, cs2 - t2, cs2) <= rs2
    lane = lax.broadcasted_iota(jnp.int32, (1, LANES), 1)
    head_mask = (lane < HEAD_DIM, lane >= HEAD_DIM)
    units = [(s, p) for s in seqs for p in range(n_pairs)]
    n_levels = int(math.log2(t))

    def stack(x):
        return jnp.concatenate([jnp.where(head_mask[0], x, 0.0), jnp.where(head_mask[1], x, 0.0)], axis=0)

    def chunk(ci_, carry):
        at, rt, bt, kt, vv, p_end, sls = [], [], [], [], [], [], []
        for s in seqs:
            sl = pl.ds(pl.multiple_of(s * tb + ci_ * t, t), t)
            lw = lw_sc[sl, :]
            hi = lw.astype(BF16)
            lo = (lw - hi.astype(F32)).astype(BF16)
            c = (jnp.dot(tri_incl, hi, preferred_element_type=F32)
                 + jnp.dot(tri_incl, lo, preferred_element_type=F32))
            e_in = jnp.exp(c)
            e_neg = jnp.exp(-c)
            p_end.append(e_in[t - 1:t, :])
            at.append(a_sc[sl, :] * jnp.exp(c - lw))
            rt.append(r_sc[sl, :] * e_in)
            bt.append(b_sc[sl, :] * e_neg)
            kt.append(k_sc[sl, :] * e_neg)
            vv.append(v_sc[sl, :])
            sls.append(sl)
        cols = [slice(p * LANES, (p + 1) * LANES) for p in range(n_pairs)]
        un = range(len(units))
        pe = [p_end[s][:, cols[p]] for s, p in units]
        a2 = [stack(at[s][:, cols[p]]).astype(BF16) for s, p in units]
        r2 = [stack(rt[s][:, cols[p]]).astype(BF16) for s, p in units]
        v2b = [stack(vv[s][:, cols[p]]).astype(BF16) for s, p in units]
        bk = [jnp.concatenate([stack(bt[s][:, cols[p]]), stack(kt[s][:, cols[p]])], axis=0) for s, p in units]
        g_ = [_dot_nt(jnp.concatenate([a2[i], r2[i]], axis=0), bk[i]) for i in un]
        n = [jnp.where(strict, g_[i][:t2, :t2], 0.0) for i in un]
        ak = [jnp.where(strict, g_[i][:t2, t2:], 0.0).astype(BF16) for i in un]
        rbk = [jnp.where(incl2, g_[i][t2:, :], 0.0).astype(BF16) for i in un]
        xm = n
        pw = [_dot(n[i], n[i]) for i in un]
        for _ in range(n_levels - 2):
            res = [_dot(jnp.concatenate([pw[i], xm[i]], axis=0), pw[i]) for i in un]
            xm = [xm[i] + pw[i] + res[i][t2:] for i in un]
            pw = [res[i][:t2] for i in un]
        res = [_dot(xm[i], pw[i]) for i in un]
        xm = [(xm[i] + pw[i] + res[i]).astype(BF16) for i in un]
        s_old = [s_sc[s * n_pairs + p] for s, p in units]
        s_bf = [s_old[i].astype(BF16) for i in un]
        rhs = [_dot_nt(a2[i], s_bf[i]) for i in un]
        akv = [jnp.dot(ak[i], v2b[i], preferred_element_type=F32) for i in un]
        rhs = [rhs[i] + akv[i] for i in un]
        u = [rhs[i] + _dot(xm[i], rhs[i]) for i in un]
        uv = [jnp.concatenate([u[i].astype(BF16), v2b[i]], axis=0) for i in un]
        ys = [_dot_nt(r2[i], s_bf[i]) for i in un]
        yi = [jnp.dot(rbk[i], uv[i], preferred_element_type=F32) for i in un]
        sn = [_dot_tn(uv[i], bk[i] * pe[i]) for i in un]
        for i, (s, p) in enumerate(units):
            y2 = ys[i] + yi[i]
            y_sc[sls[s], cols[p]] = y2[:t] + y2[t:]
            s_sc[s * n_pairs + p] = s_old[i] * pe[i] + sn[i]
        return carry

    lax.fori_loop(0, tb // t, chunk, 0)

    y = y_sc[...]
    inv_n = 1.0 / HEAD_DIM
    mean = _seg_sum(y, seg_ones) * inv_n
    yc = y - mean
    var = _seg_sum(yc * yc, seg_ones) * inv_n
    yn = yc * lax.rsqrt(var + GN_EPS) * lnw_ref[...] + lnb_ref[...]
    bonus = _seg_sum(r_sc[...] * k_sc[...] * rk_ref[...], seg_ones) * v_sc[...]
    ya = (yn + bonus) * g_sc[...]
    for s in seqs:
        ya_ref[s] = ya[s * tb:(s + 1) * tb]

    pos = blk * tb + lax.broadcasted_iota(jnp.int32, (tb, 1), 0)
    gw = bw // len(POOL_WINDOWS)
    for s in seqs:
        pb = pb_ref[s]
        xbuf[s, POOL_PAD:POOL_PAD + tb, :] = pb
        sums = _pool_window_sums(xbuf, e2, e4, e8, s, tb)
        for gi, w in enumerate(POOL_WINDOWS):
            cols = slice(gi * gw, (gi + 1) * gw)
            cnt = jnp.minimum(w, pos + 1).astype(F32)
            dlt = sums[gi] / cnt - pb[:, cols]
            yb_ref[s, :, cols] = _dot(dlt, wpool_ref[gi]) * pscale_ref[:, cols]
        xbuf[s, POOL_PAD - 16:POOL_PAD, :] = xbuf[s, tb + POOL_PAD - 16:tb + POOL_PAD, :]

    @pl.when(blk == n_blocks - 1)
    def _():
        for s in seqs:
            for p in range(n_pairs):
                s_p = s_sc[s * n_pairs + p]
                for h in range(2):
                    o = h * HEAD_DIM
                    wkv_ref[s, 2 * p + h] = s_p[o:o + HEAD_DIM, o:o + HEAD_DIM]


def _prompt_mix_call(pa, pb, batch, seq, prm):
    na = pa.shape[1]
    bw = pb.shape[1]
    aw = prm["w0"].shape[1]
    n_heads = aw // HEAD_DIM
    tb, ns = min(PROMPT_BLOCK, seq), PROMPT_SEQS
    assert seq % tb == 0 and tb % WKV_CHUNK == 0 and aw % LANES == 0 and batch % ns == 0
    assert len(POOL_WINDOWS) == 4 and POOL_HIST < 16 <= POOL_PAD - 8
    n_blocks = seq // tb
    gw = bw // len(POOL_WINDOWS)
    row = lambda b, c: (b, c, 0)
    const2 = lambda b, c: (0, 0)
    const3 = lambda b, c: (0, 0, 0)
    full = lambda a: pl.BlockSpec(a.shape, const2 if a.ndim == 2 else const3)
    names = ("mu", "w0", "w2", "a0", "a2", "g2", "k_k", "k_a", "r_k", "ln_w", "ln_b", "w_pool", "pool_scale", "seg")
    params = [prm[n] for n in names]
    blk = lambda: pltpu.VMEM((ns * tb, aw), F32)
    ya, yb, wkv = pl.pallas_call(
        functools.partial(_prompt_mix_kernel, n_blocks=n_blocks),
        grid=(batch // ns, n_blocks),
        in_specs=[pl.BlockSpec((ns, tb, na), row), pl.BlockSpec((ns, tb, bw), row)] + [full(a) for a in params],
        out_specs=[pl.BlockSpec((ns, tb, aw), row), pl.BlockSpec((ns, tb, bw), row),
                   pl.BlockSpec((ns, n_heads, HEAD_DIM, HEAD_DIM), lambda b, c: (b, 0, 0, 0))],
        out_shape=[jax.ShapeDtypeStruct((batch, seq, aw), F32), jax.ShapeDtypeStruct((batch, seq, bw), F32),
                   jax.ShapeDtypeStruct((batch, n_heads, HEAD_DIM, HEAD_DIM), F32)],
        scratch_shapes=[pltpu.VMEM((ns, tb + SHIFT_PAD, na), F32), pltpu.VMEM((ns, tb + POOL_PAD, bw), F32),
                        pltpu.VMEM((ns, tb + POOL_PAD, bw), F32), pltpu.VMEM((ns, tb + POOL_PAD, bw - gw), F32),
                        pltpu.VMEM((ns, tb + POOL_PAD, bw - 2 * gw), F32),
                        pltpu.VMEM((ns * (aw // LANES), LANES, LANES), F32),
                        blk(), blk(), blk(), blk(), blk(), blk(), blk(), blk()],
        compiler_params=_cparams(2),
        name="mix_prompt",
    )(pa.reshape(batch, seq, na), pb.reshape(batch, seq, bw), *params)
    return ya.reshape(batch * seq, aw), yb.reshape(batch * seq, bw), wkv


def _sample_prep_kernel(pa_ref, sh0_ref, pb_ref, hist_ref, mu_ref, w0_ref, w2_ref, a0_ref, a2_ref, g2_ref,
                        kk_ref, ka_ref, wpool_ref, pscale_ref, seg_ref,
                        r_ref, w_ref, k_ref, v_ref, a_ref, b_ref, g_ref, yb_ref, *, batch, seq):
    aw = r_ref.shape[1]
    bw = yb_ref.shape[1]
    pa = pa_ref[...]
    prev = jnp.concatenate([sh0_ref[...], pa[:(seq - 1) * batch]], axis=0)
    xs = pa + (prev - pa) * mu_ref[...]
    r, k, v, logw, a, b, g = _rwkv_prep(xs, aw, seg_ref[...], w0_ref[...], w2_ref[...], a0_ref[...], a2_ref[...],
                                        g2_ref[...], kk_ref[...], ka_ref[...])
    outs = ((r_ref, r), (w_ref, jnp.exp(logw)), (k_ref, k), (v_ref, v), (a_ref, a), (b_ref, b), (g_ref, g))
    for ref, val in outs:
        for ti in range(seq):
            ref[ti] = val[ti * batch:(ti + 1) * batch, :].T

    gw = bw // len(POOL_WINDOWS)

    def slab(i, cols):
        if i < POOL_HIST:
            return hist_ref[i * batch:(i + 1) * batch, cols]
        return pb_ref[(i - POOL_HIST) * batch:(i - POOL_HIST + 1) * batch, cols]

    for ti in range(seq):
        rows = slice(ti * batch, (ti + 1) * batch)
        for gi, w in enumerate(POOL_WINDOWS):
            cols = slice(gi * gw, (gi + 1) * gw)
            s = _pool_group_sum(lambda j: slab(POOL_HIST + ti - j, cols), w)
            cnt = float(min(w, PAST_LEN + ti + 1))
            dlt = s / cnt - pb_ref[rows, cols]
            yb_ref[rows, cols] = _dot(dlt, wpool_ref[gi]) * pscale_ref[:, cols]


def _sample_prep_call(pa, sh0, pb, hist, batch, seq, prm):
    rows = pa.shape[0]
    aw = prm["w0"].shape[1]
    bw = pb.shape[1]
    names = ("mu", "w0", "w2", "a0", "a2", "g2", "k_k", "k_a", "w_pool", "pool_scale", "seg")
    args = [pa, sh0, pb, hist] + [prm[n] for n in names]
    full = lambda a: pl.BlockSpec(a.shape, (lambda i: (0, 0)) if a.ndim == 2 else (lambda i: (0, 0, 0)))
    o = jax.ShapeDtypeStruct((seq, aw, batch), F32)
    ob = jax.ShapeDtypeStruct((rows, bw), F32)
    return pl.pallas_call(
        functools.partial(_sample_prep_kernel, batch=batch, seq=seq),
        grid=(1,),
        in_specs=[full(a) for a in args],
        out_specs=[full(o)] * 7 + [full(ob)],
        out_shape=[o] * 7 + [ob],
        compiler_params=_cparams(1),
        name="prep_sample",
    )(*args)


SAMPLE_VROWS = 16


def _sample_wkv_kernel(r_ref, w_ref, k_ref, v_ref, a_ref, b_ref, g_ref, rk_ref, lnw_ref, lnb_ref, s_ref, all_ref,
                       y_ref, so_ref, y_sc):
    del all_ref
    seq, hd, batch = r_ref.shape
    vb = s_ref.shape[0]
    h, vs = pl.program_id(0), pl.program_id(1)
    hrow = pl.multiple_of((h % 2) * hd, hd)
    for q in range(vb):
        tile = s_ref[q]
        for ti in range(seq):
            sa = jnp.sum(tile * a_ref[ti], axis=0, keepdims=True)
            tile = tile * w_ref[ti] + sa * b_ref[ti] + v_ref[ti, pl.ds(vs * vb + q, 1), :] * k_ref[ti]
            y_sc[ti, pl.ds(hrow + vs * vb + q, 1), :] = jnp.sum(tile * r_ref[ti], axis=0, keepdims=True)
        so_ref[q] = tile

    @pl.when(vs == pl.num_programs(1) - 1)
    def _():
        for ti in range(seq):
            rows = pl.ds(hrow, hd)
            y = y_sc[ti, rows, :]
            yc = y - jnp.mean(y, axis=0, keepdims=True)
            var = jnp.mean(yc * yc, axis=0, keepdims=True)
            yn = yc * lax.rsqrt(var + GN_EPS) * lnw_ref[...] + lnb_ref[...]
            bonus = jnp.sum(r_ref[ti] * k_ref[ti] * rk_ref[...], axis=0, keepdims=True) * v_ref[ti]
            y_sc[ti, rows, :] = (yn + bonus) * g_ref[ti]

    @pl.when((vs == pl.num_programs(1) - 1) & (h % 2 == 1))
    def _():
        for ti in range(seq):
            y_ref[ti * batch:(ti + 1) * batch, :] = y_sc[ti].T


def _sample_wkv_call(vecs, state_t, new_states, layer, batch, seq, prm):
    n_heads, hd = state_t.shape[1:3]
    aw = n_heads * hd
    vb = SAMPLE_VROWS
    assert 2 * hd == LANES and hd % vb == 0
    vspec = pl.BlockSpec((seq, hd, batch), lambda h, s: (0, h, 0))
    cspec = pl.BlockSpec((hd, 1), lambda h, s: (h, 0))
    sspec = pl.BlockSpec((None, None, vb, hd, batch), lambda h, s: (layer, h, s, 0, 0))
    cols = [prm[n].reshape(aw, 1) for n in ("r_k", "ln_w", "ln_b")]
    args = [*vecs, *cols, state_t, new_states]
    return pl.pallas_call(
        _sample_wkv_kernel,
        grid=(n_heads, hd // vb),
        in_specs=[vspec] * 7 + [cspec] * 3 + [sspec, pl.BlockSpec(memory_space=pl.ANY)],
        out_specs=[pl.BlockSpec((seq * batch, LANES), lambda h, s: (0, h // 2)), sspec],
        out_shape=[jax.ShapeDtypeStruct((seq * batch, aw), F32), jax.ShapeDtypeStruct(state_t.shape, F32)],
        scratch_shapes=[pltpu.VMEM((seq, LANES, batch), F32)],
        input_output_aliases={len(args) - 1: 1},
        compiler_params=_cparams(2),
        name="wkv_sample",
    )(*args)


def kernel(x_prompt, x_sample, state_wkv, state_shift, state_pool, c_prompt, c_sample, norm_g, w_mod, b_mod, w_ffn_in, w_ffn_out, w_in, mu_shift, w0, w2, a0, a2, g2, k_k, k_a, r_k, ln_x_w, ln_x_b, w_pool, pool_scale, w_br_a, w_br_b, w_gate, b_gate, w_out, final_g):
    bp, lp, d = x_prompt.shape
    bs, ls, _ = x_sample.shape
    depth = w_mod.shape[0]
    aw = w0.shape[1]
    a_proj = mu_shift.shape[1]
    bw = pool_scale.shape[1]

    bf = lambda w: w.astype(BF16)
    w_ffn_in_b, w_ffn_out_b = bf(w_ffn_in), bf(w_ffn_out)
    w_in_a, w_in_b = bf(w_in[:, :, :a_proj]), bf(w_in[:, :, a_proj:])
    w_gate_b, w_br_a_b, w_br_b_b, w_out_b = bf(w_gate), bf(w_br_a), bf(w_br_b), bf(w_out)
    w2_b, a2_b, g2_b, w_pool_b = bf(w2), bf(a2), bf(g2), bf(w_pool)
    head_of = jnp.arange(MXU_TILE, dtype=jnp.int32) // HEAD_DIM
    seg_ones = (head_of[:, None] == head_of[None, :]).astype(BF16)

    mod = _mod_call(jnp.concatenate([c_sample, c_prompt], axis=0), w_mod, b_mod)
    gp = _Group("prompt", bp, lp, min(1024, lp), min(1024, lp), mod_row0=bs)
    gs = _Group("sample", bs, ls, bs * ls, bs * ls, mod_row0=0)

    xp = x_prompt.reshape(bp * lp, d)
    xs = jnp.transpose(x_sample, (1, 0, 2)).reshape(ls * bs, d)
    hist_s = jnp.transpose(state_pool, (0, 2, 1, 3)).reshape(depth, POOL_HIST * bs, bw)
    state_t = jnp.transpose(state_wkv, (0, 2, 3, 4, 1))
    wkv_s = jnp.zeros_like(state_t)

    wkv_p, shift_p, pool_p, shift_s, pool_s = [], [], [], [], []
    for l in range(depth):
        prm = dict(mu=mu_shift[l][None], w0=w0[l][None], w2=w2_b[l], a0=a0[l][None], a2=a2_b[l], g2=g2_b[l],
                   k_k=k_k[l][None], k_a=k_a[l][None], r_k=r_k[l].reshape(1, aw), ln_w=ln_x_w[l][None],
                   ln_b=ln_x_b[l][None], w_pool=w_pool_b[l], pool_scale=pool_scale[l][None], seg=seg_ones)
        last = l == depth - 1

        xp = _ffn_call(xp, gp, mod, l, 0, norm_g[l, 0], w_ffn_in_b, w_ffn_out_b)
        pa, pb = _mixin_call(xp, gp, mod, l, norm_g[l, 1], w_in_a[l], w_in_b[l])
        ya, yb, s_new = _prompt_mix_call(pa, pb, bp, lp, prm)
        xp = _mixout_call(xp, gp, mod, l, norm_g[l, 1], ya, yb, w_gate_b, b_gate, w_br_a_b, w_br_b_b, w_out_b)
        xp = _ffn_call(xp, gp, mod, l, 2, norm_g[l, 2], w_ffn_in_b, w_ffn_out_b, final_g if last else None)
        wkv_p.append(s_new)
        shift_p.append(pa.reshape(bp, lp, a_proj)[:, -1:])
        pool_p.append(pb.reshape(bp, lp, bw)[:, -POOL_HIST:])

        xs = _ffn_call(xs, gs, mod, l, 0, norm_g[l, 0], w_ffn_in_b, w_ffn_out_b)
        pa, pb = _mixin_call(xs, gs, mod, l, norm_g[l, 1], w_in_a[l], w_in_b[l])
        *vecs, yb = _sample_prep_call(pa, state_shift[l, :, 0, :], pb, hist_s[l], bs, ls, prm)
        ya, wkv_s = _sample_wkv_call(vecs, state_t, wkv_s, l, bs, ls, prm)
        xs = _mixout_call(xs, gs, mod, l, norm_g[l, 1], ya, yb, w_gate_b, b_gate, w_br_a_b, w_br_b_b, w_out_b)
        xs = _ffn_call(xs, gs, mod, l, 2, norm_g[l, 2], w_ffn_in_b, w_ffn_out_b, final_g if last else None)
        shift_s.append(pa.reshape(ls, bs, a_proj)[-1][:, None, :])
        pb_bm = jnp.transpose(pb.reshape(ls, bs, bw), (1, 0, 2))
        pool_s.append(jnp.concatenate([state_pool[l], pb_bm], axis=1)[:, -POOL_HIST:])

    y_prompt = xp.reshape(bp, lp, d)
    y_sample = jnp.transpose(xs.reshape(ls, bs, d), (1, 0, 2))
    return (y_prompt, y_sample, jnp.stack(wkv_p), jnp.stack(shift_p), jnp.stack(pool_p),
            jnp.transpose(wkv_s, (0, 4, 1, 2, 3)), jnp.stack(shift_s), jnp.stack(pool_s))
```

```python
import functools
import math

import jax
import jax.numpy as jnp
from jax import lax
from jax.experimental import pallas as pl
from jax.experimental.pallas import tpu as pltpu

F32 = jnp.float32
BF16 = jnp.bfloat16

HEAD_DIM = 64
POOL_WINDOWS = (2, 4, 8, 16)
POOL_HIST = max(POOL_WINDOWS) - 1
N_SUB = 3
NORM_EPS = 1e-6
GN_EPS = 64e-5
PAST_LEN = 16384
DECAY_SCALE = math.exp(-0.5)

LANES = 128
MXU_TILE = 256
WKV_CHUNK = 64
VMEM_LIMIT_BYTES = 56 * 1024 * 1024


_RESIDENT = dict(pipeline_mode=pl.Buffered(1))


def _cparams(n_axes):
    return pltpu.CompilerParams(dimension_semantics=("arbitrary",) * n_axes,
                                vmem_limit_bytes=VMEM_LIMIT_BYTES)


def _sigmoid(x):
    return 1.0 / (1.0 + jnp.exp(-x))


def _dot(a, b):
    return jnp.dot(a.astype(BF16), b.astype(BF16), preferred_element_type=F32)


def _dot_nt(a, b):
    return lax.dot_general(a.astype(BF16), b.astype(BF16), (((1,), (1,)), ((), ())),
                           preferred_element_type=F32)


def _dot_tn(a, b):
    return lax.dot_general(a.astype(BF16), b.astype(BF16), (((0,), (0,)), ((), ())),
                           preferred_element_type=F32)


def _seg_sum(x, seg_ones):
    n, width = x.shape
    tw = seg_ones.shape[0]
    outs = []
    for c0 in range(0, width, tw):
        xc = x[:, c0:c0 + tw]
        hi = xc.astype(BF16)
        lo = (xc - hi.astype(F32)).astype(BF16)
        both = jnp.dot(jnp.concatenate([hi, lo], axis=0), seg_ones, preferred_element_type=F32)
        outs.append(both[:n] + both[n:])
    return jnp.concatenate(outs, axis=1)


def _rms(x):
    return x * lax.rsqrt(jnp.mean(x * x, axis=-1, keepdims=True) + NORM_EPS)


SUBLANES = 8


def _rows(ref, tm, seq_rows):
    if seq_rows is not None:
        row0, per_seq = seq_rows
        return ref[pl.ds((row0 + pl.program_id(0) // per_seq) % SUBLANES, 1), :]
    m = ref[...]
    return m if m.shape[0] == tm else jnp.concatenate([m] * (tm // m.shape[0]), axis=0)


def _modnorm(x, g, shift, scale):
    return _rms(x) * g * (1.0 + scale) + shift


def _mod_kernel(c_ref, w_ref, b_ref, o_ref):
    c = c_ref[...]
    o_ref[...] = _dot(c * _sigmoid(c), w_ref[...]) + b_ref[...]


def _mod_call(c_all, w_mod, b_mod):
    depth, d, n9 = w_mod.shape
    rows = c_all.shape[0]
    n_sub9 = n9 // d
    tn = d
    return pl.pallas_call(
        _mod_kernel,
        grid=(depth, n_sub9),
        in_specs=[pl.BlockSpec((rows, d), lambda l, j: (0, 0)),
                  pl.BlockSpec((None, d, tn), lambda l, j: (l, 0, j)),
                  pl.BlockSpec((None, 1, tn), lambda l, j: (l, 0, j))],
        out_specs=pl.BlockSpec((None, None, rows, tn), lambda l, j: (l, j, 0, 0)),
        out_shape=jax.ShapeDtypeStruct((depth, n_sub9, rows, d), F32),
        compiler_params=_cparams(2),
        name="mod",
    )(c_all, w_mod, b_mod.reshape(depth, 1, n9))


class _Group:
    def __init__(self, kind, batch, seq, tm, ffn_tm, mod_row0):
        self.kind, self.batch, self.seq, self.tm, self.ffn_tm, self.mod_row0 = kind, batch, seq, tm, ffn_tm, mod_row0
        self.rows = batch * seq
        for t in (tm, ffn_tm):
            assert self.rows % t == 0
            if kind == "prompt":
                assert seq % t == 0

    def seq_rows(self, tm=None):
        return (self.mod_row0, self.seq // (tm or self.tm)) if self.kind == "prompt" else None

    def mod_spec(self, d, layer, idx, tm=None):
        if self.kind == "prompt":
            row0, per_seq = self.seq_rows(tm)
            return pl.BlockSpec((None, None, SUBLANES, d),
                                lambda i, *_: (layer, idx, (row0 + i // per_seq) // SUBLANES, 0))
        return pl.BlockSpec((None, None, self.batch, d), lambda i, *_: (layer, idx, 0, 0))


FFN_CHUNK = 2 * MXU_TILE


def _ffn_kernel(x_ref, sh_ref, sc_ref, gt_ref, g_ref, win_ref, wo_ref, *rest, final, seq_rows):
    if final:
        fg_ref, o_ref, acc_sc = rest
    else:
        o_ref, acc_sc = rest
    tm = x_ref.shape[0]
    d_ff = wo_ref.shape[0]
    x = x_ref[...]
    u = _modnorm(x, g_ref[...], _rows(sh_ref, tm, seq_rows), _rows(sc_ref, tm, seq_rows)).astype(BF16)
    for c0 in range(0, d_ff, FFN_CHUNK):
        c1 = min(c0 + FFN_CHUNK, d_ff)
        hg = jnp.dot(u, win_ref[:, c0:c1], preferred_element_type=F32)
        hu = jnp.dot(u, win_ref[:, d_ff + c0:d_ff + c1], preferred_element_type=F32)
        act = (hg * _sigmoid(hg) * hu).astype(BF16)
        part = jnp.dot(act, wo_ref[c0:c1, :], preferred_element_type=F32)
        if c0 == 0:
            acc_sc[...] = part
        else:
            acc_sc[...] += part
    y = x + 0.5 * _rows(gt_ref, tm, seq_rows) * acc_sc[...]
    if final:
        y = _rms(y) * fg_ref[...]
    o_ref[...] = y


def _ffn_call(x, grp, mod, layer, sub, norm_g, w_in, w_out, final_g=None):
    rows, d = x.shape
    d_ff = w_out.shape[2]
    assert d_ff % LANES == 0 and FFN_CHUNK % LANES == 0
    tm = grp.ffn_tm
    final = final_g is not None
    row = lambda i: (i, 0)
    const = lambda i: (0, 0)
    half = lambda i: (layer, sub // 2, 0, 0)
    in_specs = [pl.BlockSpec((tm, d), row),
                grp.mod_spec(d, layer, 3 * sub, tm), grp.mod_spec(d, layer, 3 * sub + 1, tm),
                grp.mod_spec(d, layer, 3 * sub + 2, tm),
                pl.BlockSpec((1, d), const),
                pl.BlockSpec((None, None, d, 2 * d_ff), half, **_RESIDENT),
                pl.BlockSpec((None, None, d_ff, d), half, **_RESIDENT)]
    args = [x, mod, mod, mod, norm_g.reshape(1, d), w_in, w_out]
    if final:
        in_specs.append(pl.BlockSpec((1, d), const))
        args.append(final_g.reshape(1, d))
    return pl.pallas_call(
        functools.partial(_ffn_kernel, final=final, seq_rows=grp.seq_rows(tm)),
        grid=(rows // tm,),
        in_specs=in_specs,
        out_specs=pl.BlockSpec((tm, d), row),
        out_shape=jax.ShapeDtypeStruct((rows, d), F32),
        scratch_shapes=[pltpu.VMEM((tm, d), F32)],
        compiler_params=_cparams(1),
        name=f"ffn_{grp.kind}",
    )(*args)


def _mixin_kernel(x_ref, sh_ref, sc_ref, g_ref, wa_ref, wb_ref, pa_ref, pb_ref, *, seq_rows):
    tm = x_ref.shape[0]
    u = _modnorm(x_ref[...], g_ref[...], _rows(sh_ref, tm, seq_rows), _rows(sc_ref, tm, seq_rows)).astype(BF16)
    pa_ref[...] = jnp.dot(u, wa_ref[...], preferred_element_type=F32)
    pb_ref[...] = jnp.dot(u, wb_ref[...], preferred_element_type=F32)


def _mixin_call(x, grp, mod, layer, norm_g, w_a, w_b):
    rows, d = x.shape
    tm = grp.tm
    na, nb = w_a.shape[1], w_b.shape[1]
    row = lambda i: (i, 0)
    const = lambda i: (0, 0)
    m = mod
    return pl.pallas_call(
        functools.partial(_mixin_kernel, seq_rows=grp.seq_rows()),
        grid=(rows // tm,),
        in_specs=[pl.BlockSpec((tm, d), row), grp.mod_spec(d, layer, 3), grp.mod_spec(d, layer, 4),
                  pl.BlockSpec((1, d), const), pl.BlockSpec((d, na), const, **_RESIDENT),
                  pl.BlockSpec((d, nb), const, **_RESIDENT)],
        out_specs=[pl.BlockSpec((tm, na), row), pl.BlockSpec((tm, nb), row)],
        out_shape=[jax.ShapeDtypeStruct((rows, na), F32), jax.ShapeDtypeStruct((rows, nb), F32)],
        compiler_params=_cparams(1),
        name=f"mixin_{grp.kind}",
    )(x, m, m, norm_g.reshape(1, d), w_a, w_b)


def _mixout_kernel(x_ref, sh_ref, sc_ref, gt_ref, g_ref, ya_ref, yb_ref, wgate_ref, bgate_ref,
                   wbra_ref, wbrb_ref, wout_ref, o_ref, *, seq_rows):
    tm, d = x_ref.shape
    x = x_ref[...]
    u = _modnorm(x, g_ref[...], _rows(sh_ref, tm, seq_rows), _rows(sc_ref, tm, seq_rows)).astype(BF16)
    gates = _sigmoid(jnp.dot(u, wgate_ref[...], preferred_element_type=F32) + bgate_ref[...])
    ma = _dot(ya_ref[...], wbra_ref[...])
    mb = _dot(yb_ref[...], wbrb_ref[...])
    merged = gates[:, :d] * ma + gates[:, d:] * mb
    o_ref[...] = x + _rows(gt_ref, tm, seq_rows) * _dot(merged, wout_ref[...])


def _mixout_call(x, grp, mod, layer, norm_g, ya, yb, w_gate, b_gate, w_br_a, w_br_b, w_out):
    rows, d = x.shape
    tm = grp.tm
    wa, wb = ya.shape[1], yb.shape[1]
    row = lambda i: (i, 0)
    const = lambda i: (0, 0)
    lyr = lambda i: (layer, 0, 0)
    m = mod
    return pl.pallas_call(
        functools.partial(_mixout_kernel, seq_rows=grp.seq_rows()),
        grid=(rows // tm,),
        in_specs=[pl.BlockSpec((tm, d), row), grp.mod_spec(d, layer, 3), grp.mod_spec(d, layer, 4),
                  grp.mod_spec(d, layer, 5), pl.BlockSpec((1, d), const),
                  pl.BlockSpec((tm, wa), row), pl.BlockSpec((tm, wb), row),
                  pl.BlockSpec((None, d, 2 * d), lyr, **_RESIDENT), pl.BlockSpec((None, 1, 2 * d), lyr),
                  pl.BlockSpec((None, wa, d), lyr, **_RESIDENT), pl.BlockSpec((None, wb, d), lyr, **_RESIDENT),
                  pl.BlockSpec((None, d, d), lyr, **_RESIDENT)],
        out_specs=pl.BlockSpec((tm, d), row),
        out_shape=jax.ShapeDtypeStruct((rows, d), F32),
        compiler_params=_cparams(1),
        name=f"mixout_{grp.kind}",
    )(x, m, m, m, norm_g.reshape(1, d), ya, yb, w_gate, b_gate.reshape(-1, 1, 2 * d), w_br_a, w_br_b, w_out)


def _rwkv_prep(xs, aw, seg_ones, w0, w2, a0, a2, g2, k_k, k_a):
    n_dec, n_aaa = w2.shape[0], a2.shape[0]
    r = xs[:, 0:aw]
    k = xs[:, aw:2 * aw]
    v = xs[:, 2 * aw:3 * aw]
    o = 3 * aw
    wl = xs[:, o:o + n_dec]
    al = xs[:, o + n_dec:o + n_dec + n_aaa]
    gl = xs[:, o + n_dec + n_aaa:]
    logw = -DECAY_SCALE * _sigmoid(w0 + _dot(jnp.tanh(wl), w2))
    asig = _sigmoid(a0 + _dot(al, a2))
    g = _dot(_sigmoid(gl), g2)
    kk = k * k_k
    kk = kk * lax.rsqrt(jnp.maximum(_seg_sum(kk * kk, seg_ones), 1e-24))
    k = k * (1.0 + (asig - 1.0) * k_a)
    return r, k, v, logw, -kk, kk * asig, g


def _pool_group_sum(read_rows, w):
    s = read_rows(0)
    for j in range(1, w):
        s = s + read_rows(j)
    return s


PROMPT_BLOCK = 256
PROMPT_SEQS = 2
SHIFT_PAD = 8
POOL_PAD = 24


def _pool_window_sums(xbuf, e2, e4, e8, s, tb):
    gw = xbuf.shape[2] // len(POOL_WINDOWS)
    lo, hi = POOL_PAD - 16, POOL_PAD + tb
    e2[s, lo:hi, :] = xbuf[s, lo:hi, :] + xbuf[s, lo - 1:hi - 1, :]
    e4[s, lo:hi, :] = e2[s, lo:hi, gw:] + e2[s, lo - 2:hi - 2, gw:]
    e8[s, lo:hi, :] = e4[s, lo:hi, gw:] + e4[s, lo - 4:hi - 4, gw:]
    blk = slice(POOL_PAD, POOL_PAD + tb)
    back8 = slice(POOL_PAD - 8, POOL_PAD - 8 + tb)
    return (e2[s, blk, 0:gw], e4[s, blk, 0:gw], e8[s, blk, 0:gw], e8[s, blk, gw:] + e8[s, back8, gw:])


def _prompt_mix_kernel(pa_ref, pb_ref, mu_ref, w0_ref, w2_ref, a0_ref, a2_ref, g2_ref, kk_ref, ka_ref, rk_ref,
                       lnw_ref, lnb_ref, wpool_ref, pscale_ref, seg_ref,
                       ya_ref, yb_ref, wkv_ref,
                       pbuf, xbuf, e2, e4, e8, s_sc, r_sc, k_sc, v_sc, a_sc, b_sc, lw_sc, g_sc, y_sc, *, n_blocks):
    ns, tb, aw = ya_ref.shape
    bw = yb_ref.shape[2]
    t = WKV_CHUNK
    n_chunks = tb // t
    n_pairs = aw // LANES
    blk = pl.program_id(1)
    seqs = range(ns)

    @pl.when(blk == 0)
    def _():
        pbuf[:, 0:SHIFT_PAD, :] = jnp.zeros((ns, SHIFT_PAD, pbuf.shape[2]), F32)
        xbuf[:, 0:POOL_PAD, :] = jnp.zeros((ns, POOL_PAD, bw), F32)
        e2[:, 0:POOL_PAD, :] = jnp.zeros((ns, POOL_PAD, e2.shape[2]), F32)
        e4[:, 0:POOL_PAD, :] = jnp.zeros((ns, POOL_PAD, e4.shape[2]), F32)
        s_sc[...] = jnp.zeros_like(s_sc)

    for s in seqs:
        pbuf[s, SHIFT_PAD:SHIFT_PAD + tb, :] = pa_ref[s]
    seg_ones = seg_ref[...]
    chunk_rows = lambda c: slice(c * ns * t, (c + 1) * ns * t)
    unit_rows = lambda c, s: slice((c * ns + s) * t, (c * ns + s + 1) * t)

    def prep(c):
        xs = []
        for s in seqs:
            lo = SHIFT_PAD + c * t
            pa = pbuf[s, lo:lo + t, :]
            xs.append(pa + (pbuf[s, lo - 1:lo - 1 + t, :] - pa) * mu_ref[...])
        r, k, v, logw, a, b, g = _rwkv_prep(jnp.concatenate(xs, axis=0), aw, seg_ones, w0_ref[...], w2_ref[...],
                                            a0_ref[...], a2_ref[...], g2_ref[...], kk_ref[...], ka_ref[...])
        for ref, val in ((r_sc, r), (k_sc, k), (v_sc, v), (a_sc, a), (b_sc, b), (lw_sc, logw), (g_sc, g)):
            ref[chunk_rows(c), :] = val

    def finish(c):
        rows = chunk_rows(c)
        y = y_sc[rows, :]
        inv_n = 1.0 / HEAD_DIM
        yc = y - _seg_sum(y, seg_ones) * inv_n
        var = _seg_sum(yc * yc, seg_ones) * inv_n
        yn = yc * lax.rsqrt(var + GN_EPS) * lnw_ref[...] + lnb_ref[...]
        bonus = _seg_sum(r_sc[rows, :] * k_sc[rows, :] * rk_ref[...], seg_ones) * v_sc[rows, :]
        ya = (yn + bonus) * g_sc[rows, :]
        for s in seqs:
            ya_ref[s, c * t:(c + 1) * t, :] = ya[s * t:(s + 1) * t]

    def pool():
        pos = blk * tb + lax.broadcasted_iota(jnp.int32, (tb, 1), 0)
        gw = bw // len(POOL_WINDOWS)
        for s in seqs:
            pb = pb_ref[s]
            xbuf[s, POOL_PAD:POOL_PAD + tb, :] = pb
            sums = _pool_window_sums(xbuf, e2, e4, e8, s, tb)
            for gi, w in enumerate(POOL_WINDOWS):
                cols = slice(gi * gw, (gi + 1) * gw)
                cnt = jnp.minimum(w, pos + 1).astype(F32)
                dlt = sums[gi] / cnt - pb[:, cols]
                yb_ref[s, :, cols] = _dot(dlt, wpool_ref[gi]) * pscale_ref[:, cols]
            xbuf[s, POOL_PAD - 16:POOL_PAD, :] = xbuf[s, tb + POOL_PAD - 16:tb + POOL_PAD, :]

    t2 = 2 * t
    ri = lax.broadcasted_iota(jnp.int32, (t, t), 0)
    ci = lax.broadcasted_iota(jnp.int32, (t, t), 1)
    tri_incl = (ri >= ci).astype(BF16)
    rs = lax.broadcasted_iota(jnp.int32, (t2, t2), 0)
    cs = lax.broadcasted_iota(jnp.int32, (t2, t2), 1)
    strict = cs < rs
    rs2 = lax.broadcasted_iota(jnp.int32, (t2, 2 * t2), 0)
    cs2 = lax.broadcasted_iota(jnp.int32, (t2, 2 * t2), 1)
    incl2 = jnp.where(cs2 >= t2, cs2 - t2, cs2) <= rs2
    lane = lax.broadcasted_iota(jnp.int32, (1, LANES), 1)
    head_mask = (lane < HEAD_DIM, lane >= HEAD_DIM)
    units = [(s, p) for s in seqs for p in range(n_pairs)]
    un = range(len(units))
    cols = [slice(p * LANES, (p + 1) * LANES) for p in range(n_pairs)]
    n_levels = int(math.log2(t))

    def stack(x):
        return jnp.concatenate([jnp.where(head_mask[0], x, 0.0), jnp.where(head_mask[1], x, 0.0)], axis=0)

    def recur(c, between):
        at, rt, bt, kt, vv, p_end = [], [], [], [], [], []
        for s in seqs:
            sl = unit_rows(c, s)
            lw = lw_sc[sl, :]
            hi = lw.astype(BF16)
            lo = (lw - hi.astype(F32)).astype(BF16)
            cum = (jnp.dot(tri_incl, hi, preferred_element_type=F32)
                   + jnp.dot(tri_incl, lo, preferred_element_type=F32))
            e_in = jnp.exp(cum)
            e_neg = jnp.exp(-cum)
            p_end.append(e_in[t - 1:t, :])
            at.append(a_sc[sl, :] * jnp.exp(cum - lw))
            rt.append(r_sc[sl, :] * e_in)
            bt.append(b_sc[sl, :] * e_neg)
            kt.append(k_sc[sl, :] * e_neg)
            vv.append(v_sc[sl, :])
        pe = [p_end[s][:, cols[p]] for s, p in units]
        a2 = [stack(at[s][:, cols[p]]).astype(BF16) for s, p in units]
        r2 = [stack(rt[s][:, cols[p]]).astype(BF16) for s, p in units]
        v2b = [stack(vv[s][:, cols[p]]).astype(BF16) for s, p in units]
        bk = [jnp.concatenate([stack(bt[s][:, cols[p]]), stack(kt[s][:, cols[p]])], axis=0).astype(BF16)
              for s, p in units]
        g_ = [_dot_nt(jnp.concatenate([a2[i], r2[i]], axis=0), bk[i]) for i in un]
        n = [jnp.where(strict, g_[i][:t2, :t2], 0.0) for i in un]
        ak = [jnp.where(strict, g_[i][:t2, t2:], 0.0).astype(BF16) for i in un]
        rbk = [jnp.where(incl2, g_[i][t2:, :], 0.0).astype(BF16) for i in un]
        xm = n
        pw = [_dot(n[i], n[i]) for i in un]
        for _ in range(n_levels - 2):
            res = [_dot(jnp.concatenate([pw[i], xm[i]], axis=0), pw[i]) for i in un]
            xm = [xm[i] + pw[i] + res[i][t2:] for i in un]
            pw = [res[i][:t2] for i in un]
        res = [_dot(xm[i], pw[i]) for i in un]
        xm = [(xm[i] + pw[i] + res[i]).astype(BF16) for i in un]
        between()
        s_old = [s_sc[s * n_pairs + p] for s, p in units]
        s_bf = [s_old[i].astype(BF16) for i in un]
        rhs = [_dot_nt(a2[i], s_bf[i]) for i in un]
        akv = [jnp.dot(ak[i], v2b[i], preferred_element_type=F32) for i in un]
        rhs = [rhs[i] + akv[i] for i in un]
        u = [rhs[i] + _dot(xm[i], rhs[i]) for i in un]
        uv = [jnp.concatenate([u[i].astype(BF16), v2b[i]], axis=0) for i in un]
        ys = [_dot_nt(r2[i], s_bf[i]) for i in un]
        yi = [jnp.dot(rbk[i], uv[i], preferred_element_type=F32) for i in un]
        sn = [_dot_tn(uv[i], bk[i]) for i in un]
        for i, (s, p) in enumerate(units):
            y2 = ys[i] + yi[i]
            y_sc[unit_rows(c, s), cols[p]] = y2[:t] + y2[t:]
            s_sc[s * n_pairs + p] = (s_old[i] + sn[i]) * pe[i]

    prep(0)
    for c in range(n_chunks):
        def between(c=c):
            if c + 1 < n_chunks:
                prep(c + 1)
            if c == 0:
                pool()
            else:
                finish(c - 1)
        recur(c, between)
    finish(n_chunks - 1)
    for s in seqs:
        pbuf[s, 0:SHIFT_PAD, :] = pbuf[s, tb:tb + SHIFT_PAD, :]

    @pl.when(blk == n_blocks - 1)
    def _():
        for s in seqs:
            for p in range(n_pairs):
                s_p = s_sc[s * n_pairs + p]
                for h in range(2):
                    o = h * HEAD_DIM
                    wkv_ref[s, 2 * p + h] = s_p[o:o + HEAD_DIM, o:o + HEAD_DIM]


def _prompt_mix_call(pa, pb, batch, seq, prm):
    na = pa.shape[1]
    bw = pb.shape[1]
    aw = prm["w0"].shape[1]
    n_heads = aw // HEAD_DIM
    tb, ns = min(PROMPT_BLOCK, seq), PROMPT_SEQS
    assert seq % tb == 0 and tb % WKV_CHUNK == 0 and aw % LANES == 0 and batch % ns == 0
    assert len(POOL_WINDOWS) == 4 and POOL_HIST < 16 <= POOL_PAD - 8
    n_blocks = seq // tb
    gw = bw // len(POOL_WINDOWS)
    row = lambda b, c: (b, c, 0)
    const2 = lambda b, c: (0, 0)
    const3 = lambda b, c: (0, 0, 0)
    full = lambda a: pl.BlockSpec(a.shape, const2 if a.ndim == 2 else const3)
    names = ("mu", "w0", "w2", "a0", "a2", "g2", "k_k", "k_a", "r_k", "ln_w", "ln_b", "w_pool", "pool_scale", "seg")
    params = [prm[n] for n in names]
    blk = lambda: pltpu.VMEM((ns * tb, aw), F32)
    ya, yb, wkv = pl.pallas_call(
        functools.partial(_prompt_mix_kernel, n_blocks=n_blocks),
        grid=(batch // ns, n_blocks),
        in_specs=[pl.BlockSpec((ns, tb, na), row), pl.BlockSpec((ns, tb, bw), row)] + [full(a) for a in params],
        out_specs=[pl.BlockSpec((ns, tb, aw), row), pl.BlockSpec((ns, tb, bw), row),
                   pl.BlockSpec((ns, n_heads, HEAD_DIM, HEAD_DIM), lambda b, c: (b, 0, 0, 0))],
        out_shape=[jax.ShapeDtypeStruct((batch, seq, aw), F32), jax.ShapeDtypeStruct((batch, seq, bw), F32),
                   jax.ShapeDtypeStruct((batch, n_heads, HEAD_DIM, HEAD_DIM), F32)],
        scratch_shapes=[pltpu.VMEM((ns, tb + SHIFT_PAD, na), F32), pltpu.VMEM((ns, tb + POOL_PAD, bw), F32),
                        pltpu.VMEM((ns, tb + POOL_PAD, bw), F32), pltpu.VMEM((ns, tb + POOL_PAD, bw - gw), F32),
                        pltpu.VMEM((ns, tb + POOL_PAD, bw - 2 * gw), F32),
                        pltpu.VMEM((ns * (aw // LANES), LANES, LANES), F32),
                        blk(), blk(), blk(), blk(), blk(), blk(), blk(), blk()],
        compiler_params=_cparams(2),
        name="mix_prompt",
    )(pa.reshape(batch, seq, na), pb.reshape(batch, seq, bw), *params)
    return ya.reshape(batch * seq, aw), yb.reshape(batch * seq, bw), wkv


def _sample_prep_kernel(pa_ref, sh0_ref, pb_ref, hist_ref, mu_ref, w0_ref, w2_ref, a0_ref, a2_ref, g2_ref,
                        kk_ref, ka_ref, wpool_ref, pscale_ref, seg_ref,
                        r_ref, w_ref, k_ref, v_ref, a_ref, b_ref, g_ref, yb_ref, *, batch, seq):
    aw = r_ref.shape[1]
    bw = yb_ref.shape[1]
    pa = pa_ref[...]
    prev = jnp.concatenate([sh0_ref[...], pa[:(seq - 1) * batch]], axis=0)
    xs = pa + (prev - pa) * mu_ref[...]
    r, k, v, logw, a, b, g = _rwkv_prep(xs, aw, seg_ref[...], w0_ref[...], w2_ref[...], a0_ref[...], a2_ref[...],
                                        g2_ref[...], kk_ref[...], ka_ref[...])
    outs = ((r_ref, r), (w_ref, jnp.exp(logw)), (k_ref, k), (v_ref, v), (a_ref, a), (b_ref, b), (g_ref, g))
    for ref, val in outs:
        for ti in range(seq):
            ref[ti] = val[ti * batch:(ti + 1) * batch, :].T

    gw = bw // len(POOL_WINDOWS)

    def slab(i, cols):
        if i < POOL_HIST:
            return hist_ref[i * batch:(i + 1) * batch, cols]
        return pb_ref[(i - POOL_HIST) * batch:(i - POOL_HIST + 1) * batch, cols]

    for ti in range(seq):
        rows = slice(ti * batch, (ti + 1) * batch)
        for gi, w in enumerate(POOL_WINDOWS):
            cols = slice(gi * gw, (gi + 1) * gw)
            s = _pool_group_sum(lambda j: slab(POOL_HIST + ti - j, cols), w)
            cnt = float(min(w, PAST_LEN + ti + 1))
            dlt = s / cnt - pb_ref[rows, cols]
            yb_ref[rows, cols] = _dot(dlt, wpool_ref[gi]) * pscale_ref[:, cols]


def _sample_prep_call(pa, sh0, pb, hist, batch, seq, prm):
    rows = pa.shape[0]
    aw = prm["w0"].shape[1]
    bw = pb.shape[1]
    names = ("mu", "w0", "w2", "a0", "a2", "g2", "k_k", "k_a", "w_pool", "pool_scale", "seg")
    args = [pa, sh0, pb, hist] + [prm[n] for n in names]
    full = lambda a: pl.BlockSpec(a.shape, (lambda i: (0, 0)) if a.ndim == 2 else (lambda i: (0, 0, 0)))
    o = jax.ShapeDtypeStruct((seq, aw, batch), F32)
    ob = jax.ShapeDtypeStruct((rows, bw), F32)
    return pl.pallas_call(
        functools.partial(_sample_prep_kernel, batch=batch, seq=seq),
        grid=(1,),
        in_specs=[full(a) for a in args],
        out_specs=[full(o)] * 7 + [full(ob)],
        out_shape=[o] * 7 + [ob],
        compiler_params=_cparams(1),
        name="prep_sample",
    )(*args)


SAMPLE_VROWS = 16


def _sample_wkv_kernel(r_ref, w_ref, k_ref, v_ref, a_ref, b_ref, g_ref, rk_ref, lnw_ref, lnb_ref, s_ref, all_ref,
                       y_ref, so_ref, y_sc):
    del all_ref
    seq, hd, batch = r_ref.shape
    vb = s_ref.shape[0]
    h, vs = pl.program_id(0), pl.program_id(1)
    hrow = pl.multiple_of((h % 2) * hd, hd)
    for q in range(vb):
        tile = s_ref[q]
        for ti in range(seq):
            sa = jnp.sum(tile * a_ref[ti], axis=0, keepdims=True)
            tile = tile * w_ref[ti] + sa * b_ref[ti] + v_ref[ti, pl.ds(vs * vb + q, 1), :] * k_ref[ti]
            y_sc[ti, pl.ds(hrow + vs * vb + q, 1), :] = jnp.sum(tile * r_ref[ti], axis=0, keepdims=True)
        so_ref[q] = tile

    @pl.when(vs == pl.num_programs(1) - 1)
    def _():
        for ti in range(seq):
            rows = pl.ds(hrow, hd)
            y = y_sc[ti, rows, :]
            yc = y - jnp.mean(y, axis=0, keepdims=True)
            var = jnp.mean(yc * yc, axis=0, keepdims=True)
            yn = yc * lax.rsqrt(var + GN_EPS) * lnw_ref[...] + lnb_ref[...]
            bonus = jnp.sum(r_ref[ti] * k_ref[ti] * rk_ref[...], axis=0, keepdims=True) * v_ref[ti]
            y_sc[ti, rows, :] = (yn + bonus) * g_ref[ti]

    @pl.when((vs == pl.num_programs(1) - 1) & (h % 2 == 1))
    def _():
        for ti in range(seq):
            y_ref[ti * batch:(ti + 1) * batch, :] = y_sc[ti].T


def _sample_wkv_call(vecs, state_t, new_states, layer, batch, seq, prm):
    n_heads, hd = state_t.shape[1:3]
    aw = n_heads * hd
    vb = SAMPLE_VROWS
    assert 2 * hd == LANES and hd % vb == 0
    vspec = pl.BlockSpec((seq, hd, batch), lambda h, s: (0, h, 0))
    cspec = pl.BlockSpec((hd, 1), lambda h, s: (h, 0))
    sspec = pl.BlockSpec((None, None, vb, hd, batch), lambda h, s: (layer, h, s, 0, 0))
    cols = [prm[n].reshape(aw, 1) for n in ("r_k", "ln_w", "ln_b")]
    args = [*vecs, *cols, state_t, new_states]
    return pl.pallas_call(
        _sample_wkv_kernel,
        grid=(n_heads, hd // vb),
        in_specs=[vspec] * 7 + [cspec] * 3 + [sspec, pl.BlockSpec(memory_space=pl.ANY)],
        out_specs=[pl.BlockSpec((seq * batch, LANES), lambda h, s: (0, h // 2)), sspec],
        out_shape=[jax.ShapeDtypeStruct((seq * batch, aw), F32), jax.ShapeDtypeStruct(state_t.shape, F32)],
        scratch_shapes=[pltpu.VMEM((seq, LANES, batch), F32)],
        input_output_aliases={len(args) - 1: 1},
        compiler_params=_cparams(2),
        name="wkv_sample",
    )(*args)


def kernel(x_prompt, x_sample, state_wkv, state_shift, state_pool, c_prompt, c_sample, norm_g, w_mod, b_mod, w_ffn_in, w_ffn_out, w_in, mu_shift, w0, w2, a0, a2, g2, k_k, k_a, r_k, ln_x_w, ln_x_b, w_pool, pool_scale, w_br_a, w_br_b, w_gate, b_gate, w_out, final_g):
    bp, lp, d = x_prompt.shape
    bs, ls, _ = x_sample.shape
    depth = w_mod.shape[0]
    aw = w0.shape[1]
    a_proj = mu_shift.shape[1]
    bw = pool_scale.shape[1]

    bf = lambda w: w.astype(BF16)
    w_ffn_in_b, w_ffn_out_b = bf(w_ffn_in), bf(w_ffn_out)
    w_in_a, w_in_b = bf(w_in[:, :, :a_proj]), bf(w_in[:, :, a_proj:])
    w_gate_b, w_br_a_b, w_br_b_b, w_out_b = bf(w_gate), bf(w_br_a), bf(w_br_b), bf(w_out)
    w2_b, a2_b, g2_b, w_pool_b = bf(w2), bf(a2), bf(g2), bf(w_pool)
    head_of = jnp.arange(MXU_TILE, dtype=jnp.int32) // HEAD_DIM
    seg_ones = (head_of[:, None] == head_of[None, :]).astype(BF16)

    mod = _mod_call(jnp.concatenate([c_sample, c_prompt], axis=0), w_mod, b_mod)
    gp = _Group("prompt", bp, lp, min(1024, lp), min(1024, lp), mod_row0=bs)
    gs = _Group("sample", bs, ls, bs * ls, bs * ls, mod_row0=0)

    xp = x_prompt.reshape(bp * lp, d)
    xs = jnp.transpose(x_sample, (1, 0, 2)).reshape(ls * bs, d)
    hist_s = jnp.transpose(state_pool, (0, 2, 1, 3)).reshape(depth, POOL_HIST * bs, bw)
    state_t = jnp.transpose(state_wkv, (0, 2, 3, 4, 1))
    wkv_s = jnp.zeros_like(state_t)

    wkv_p, shift_p, pool_p, shift_s, pool_s = [], [], [], [], []
    for l in range(depth):
        prm = dict(mu=mu_shift[l][None], w0=w0[l][None], w2=w2_b[l], a0=a0[l][None], a2=a2_b[l], g2=g2_b[l],
                   k_k=k_k[l][None], k_a=k_a[l][None], r_k=r_k[l].reshape(1, aw), ln_w=ln_x_w[l][None],
                   ln_b=ln_x_b[l][None], w_pool=w_pool_b[l], pool_scale=pool_scale[l][None], seg=seg_ones)
        last = l == depth - 1

        xp = _ffn_call(xp, gp, mod, l, 0, norm_g[l, 0], w_ffn_in_b, w_ffn_out_b)
        pa, pb = _mixin_call(xp, gp, mod, l, norm_g[l, 1], w_in_a[l], w_in_b[l])
        ya, yb, s_new = _prompt_mix_call(pa, pb, bp, lp, prm)
        xp = _mixout_call(xp, gp, mod, l, norm_g[l, 1], ya, yb, w_gate_b, b_gate, w_br_a_b, w_br_b_b, w_out_b)
        xp = _ffn_call(xp, gp, mod, l, 2, norm_g[l, 2], w_ffn_in_b, w_ffn_out_b, final_g if last else None)
        wkv_p.append(s_new)
        shift_p.append(pa.reshape(bp, lp, a_proj)[:, -1:])
        pool_p.append(pb.reshape(bp, lp, bw)[:, -POOL_HIST:])

        xs = _ffn_call(xs, gs, mod, l, 0, norm_g[l, 0], w_ffn_in_b, w_ffn_out_b)
        pa, pb = _mixin_call(xs, gs, mod, l, norm_g[l, 1], w_in_a[l], w_in_b[l])
        *vecs, yb = _sample_prep_call(pa, state_shift[l, :, 0, :], pb, hist_s[l], bs, ls, prm)
        ya, wkv_s = _sample_wkv_call(vecs, state_t, wkv_s, l, bs, ls, prm)
        xs = _mixout_call(xs, gs, mod, l, norm_g[l, 1], ya, yb, w_gate_b, b_gate, w_br_a_b, w_br_b_b, w_out_b)
        xs = _ffn_call(xs, gs, mod, l, 2, norm_g[l, 2], w_ffn_in_b, w_ffn_out_b, final_g if last else None)
        shift_s.append(pa.reshape(ls, bs, a_proj)[-1][:, None, :])
        pb_bm = jnp.transpose(pb.reshape(ls, bs, bw), (1, 0, 2))
        pool_s.append(jnp.concatenate([state_pool[l], pb_bm], axis=1)[:, -POOL_HIST:])

    y_prompt = xp.reshape(bp, lp, d)
    y_sample = jnp.transpose(xs.reshape(ls, bs, d), (1, 0, 2))
    return (y_prompt, y_sample, jnp.stack(wkv_p), jnp.stack(shift_p), jnp.stack(pool_p),
            jnp.transpose(wkv_s, (0, 4, 1, 2, 3)), jnp.stack(shift_s), jnp.stack(pool_s))
```

```python
import functools
import math

import jax
import jax.numpy as jnp
from jax import lax
from jax.experimental import pallas as pl
from jax.experimental.pallas import tpu as pltpu

F32 = jnp.float32
BF16 = jnp.bfloat16

HEAD_DIM = 64
POOL_WINDOWS = (2, 4, 8, 16)
POOL_HIST = max(POOL_WINDOWS) - 1
N_SUB = 3
NORM_EPS = 1e-6
GN_EPS = 64e-5
PAST_LEN = 16384
DECAY_SCALE = math.exp(-0.5)

LANES = 128
MXU_TILE = 256
ROW_TILE = 1024
WKV_CHUNK = 64
VMEM_LIMIT_BYTES = 56 * 1024 * 1024


_RESIDENT = dict(pipeline_mode=pl.Buffered(1))


def _cparams(n_axes, **kw):
    return pltpu.CompilerParams(dimension_semantics=("arbitrary",) * n_axes,
                                vmem_limit_bytes=VMEM_LIMIT_BYTES, **kw)


def _sigmoid(x):
    return 1.0 / (1.0 + jnp.exp(-x))


def _dot(a, b):
    return jnp.dot(a.astype(BF16), b.astype(BF16), preferred_element_type=F32)


def _dot_nt(a, b):
    return lax.dot_general(a.astype(BF16), b.astype(BF16), (((1,), (1,)), ((), ())),
                           preferred_element_type=F32)


def _dot_tn(a, b):
    return lax.dot_general(a.astype(BF16), b.astype(BF16), (((0,), (0,)), ((), ())),
                           preferred_element_type=F32)


def _seg_sum(x, seg_ones):
    n, width = x.shape
    tw = seg_ones.shape[0]
    outs = []
    for c0 in range(0, width, tw):
        xc = x[:, c0:c0 + tw]
        hi = xc.astype(BF16)
        lo = (xc - hi.astype(F32)).astype(BF16)
        both = jnp.dot(jnp.concatenate([hi, lo], axis=0), seg_ones, preferred_element_type=F32)
        outs.append(both[:n] + both[n:])
    return jnp.concatenate(outs, axis=1)


def _rms(x):
    return x * lax.rsqrt(jnp.mean(x * x, axis=-1, keepdims=True) + NORM_EPS)


SUBLANES = 8


def _rows(ref, tm, seq_row):
    if seq_row is not None:
        return ref[pl.ds(seq_row, 1), :]
    m = ref[...]
    return m if m.shape[0] == tm else jnp.concatenate([m] * (tm // m.shape[0]), axis=0)


def _modnorm(x, g, shift, scale):
    return _rms(x) * g * (1.0 + scale) + shift


def _mod_kernel(c_ref, w_ref, b_ref, o_ref):
    c = c_ref[...]
    o_ref[...] = _dot(c * _sigmoid(c), w_ref[...]) + b_ref[...]


def _mod_call(c_all, w_mod, b_mod):
    depth, d, n9 = w_mod.shape
    rows = c_all.shape[0]
    n_sub9 = n9 // d
    tn = d
    return pl.pallas_call(
        _mod_kernel,
        grid=(depth, n_sub9),
        in_specs=[pl.BlockSpec((rows, d), lambda l, j: (0, 0)),
                  pl.BlockSpec((None, d, tn), lambda l, j: (l, 0, j)),
                  pl.BlockSpec((None, 1, tn), lambda l, j: (l, 0, j))],
        out_specs=pl.BlockSpec((None, None, rows, tn), lambda l, j: (l, j, 0, 0)),
        out_shape=jax.ShapeDtypeStruct((depth, n_sub9, rows, d), F32),
        compiler_params=_cparams(2),
        name="mod",
    )(c_all, w_mod, b_mod.reshape(depth, 1, n9))


class _Groups:
    def __init__(self, bp, lp, bs, ls, tm, mod_row0):
        assert lp % tm == 0
        self.tm, self.ts, self.bs = tm, bs * ls, bs
        self.n_prompt = bp * lp // tm
        self.seq_rows = (mod_row0, lp // tm)

    def _ptile(self, i):
        return jnp.minimum(i, self.n_prompt - 1)

    def prompt(self, width):
        return pl.BlockSpec((self.tm, width), lambda i: (self._ptile(i), 0))

    def sample(self, width, **kw):
        return pl.BlockSpec((self.ts, width), lambda i: (0, 0), **kw)

    def prompt_mod(self, d, layer, idx):
        row0, per_seq = self.seq_rows
        return pl.BlockSpec((None, None, SUBLANES, d),
                            lambda i: (layer, idx, (row0 + self._ptile(i) // per_seq) // SUBLANES, 0))

    def sample_mod(self, d, layer, idx):
        return pl.BlockSpec((None, None, self.bs, d), lambda i: (layer, idx, 0, 0), **_RESIDENT)

    def mods(self, d, layer, idxs):
        return ([self.prompt_mod(d, layer, k) for k in idxs] + [self.sample_mod(d, layer, k) for k in idxs])

    def run(self, body, prompt_refs, sample_refs):
        i = pl.program_id(0)
        row0, per_seq = self.seq_rows
        seq_row = (row0 + i // per_seq) % SUBLANES
        pl.when(i < self.n_prompt)(lambda: body(prompt_refs, seq_row))
        pl.when(i == self.n_prompt)(lambda: body(sample_refs, None))


FFN_CHUNK = 2 * MXU_TILE


def _ffn_kernel(xp_ref, xs_ref, psh, psc, pgt, ssh, ssc, sgt, g_ref, win_ref, wo_ref, *rest, final, grp):
    if final:
        fg_ref, yp_ref, ys_ref, acc_sc = rest
    else:
        yp_ref, ys_ref, acc_sc = rest
    d_ff = wo_ref.shape[0]

    def body(refs, seq_row):
        x_ref, o_ref, sh_ref, sc_ref, gt_ref = refs
        tm = x_ref.shape[0]
        x = x_ref[...]
        u = _modnorm(x, g_ref[...], _rows(sh_ref, tm, seq_row), _rows(sc_ref, tm, seq_row)).astype(BF16)
        for c0 in range(0, d_ff, FFN_CHUNK):
            c1 = min(c0 + FFN_CHUNK, d_ff)
            hg = jnp.dot(u, win_ref[:, c0:c1], preferred_element_type=F32)
            hu = jnp.dot(u, win_ref[:, d_ff + c0:d_ff + c1], preferred_element_type=F32)
            act = (hg * _sigmoid(hg) * hu).astype(BF16)
            part = jnp.dot(act, wo_ref[c0:c1, :], preferred_element_type=F32)
            if c0 == 0:
                acc_sc[0:tm, :] = part
            else:
                acc_sc[0:tm, :] += part
        y = x + 0.5 * _rows(gt_ref, tm, seq_row) * acc_sc[0:tm, :]
        if final:
            y = _rms(y) * fg_ref[...]
        o_ref[...] = y

    grp.run(body, (xp_ref, yp_ref, psh, psc, pgt), (xs_ref, ys_ref, ssh, ssc, sgt))


def _ffn_call(xp, xs, grp, mod, layer, sub, norm_g, w_in, w_out, final_g=None):
    d = xp.shape[1]
    d_ff = w_out.shape[2]
    assert d_ff % LANES == 0 and FFN_CHUNK % LANES == 0 and grp.ts <= grp.tm
    final = final_g is not None
    const = lambda i: (0, 0)
    half = lambda i: (layer, sub // 2, 0, 0)
    in_specs = ([grp.prompt(d), grp.sample(d, **_RESIDENT)] + grp.mods(d, layer, (3 * sub, 3 * sub + 1, 3 * sub + 2))
                + [pl.BlockSpec((1, d), const),
                   pl.BlockSpec((None, None, d, 2 * d_ff), half, **_RESIDENT),
                   pl.BlockSpec((None, None, d_ff, d), half, **_RESIDENT)])
    args = [xp, xs] + [mod] * 6 + [norm_g.reshape(1, d), w_in, w_out]
    if final:
        in_specs.append(pl.BlockSpec((1, d), const))
        args.append(final_g.reshape(1, d))
    return pl.pallas_call(
        functools.partial(_ffn_kernel, final=final, grp=grp),
        grid=(grp.n_prompt + 1,),
        in_specs=in_specs,
        out_specs=[grp.prompt(d), grp.sample(d)],
        out_shape=[jax.ShapeDtypeStruct(xp.shape, F32), jax.ShapeDtypeStruct(xs.shape, F32)],
        scratch_shapes=[pltpu.VMEM((grp.tm, d), F32)],
        compiler_params=_cparams(1),
        name="ffn",
    )(*args)


def _mixin_kernel(xp_ref, xs_ref, psh, psc, ssh, ssc, g_ref, wa_ref, wb_ref, pap_ref, pbp_ref, pas_ref, pbs_ref, *, grp):
    def body(refs, seq_row):
        x_ref, pa_ref, pb_ref, sh_ref, sc_ref = refs
        tm = x_ref.shape[0]
        u = _modnorm(x_ref[...], g_ref[...], _rows(sh_ref, tm, seq_row), _rows(sc_ref, tm, seq_row)).astype(BF16)
        pa_ref[...] = jnp.dot(u, wa_ref[...], preferred_element_type=F32)
        pb_ref[...] = jnp.dot(u, wb_ref[...], preferred_element_type=F32)

    grp.run(body, (xp_ref, pap_ref, pbp_ref, psh, psc), (xs_ref, pas_ref, pbs_ref, ssh, ssc))


def _mixin_call(xp, xs, grp, mod, layer, norm_g, w_a, w_b):
    d = xp.shape[1]
    na, nb = w_a.shape[1], w_b.shape[1]
    const = lambda i: (0, 0)
    shape = lambda x, n: jax.ShapeDtypeStruct((x.shape[0], n), F32)
    return pl.pallas_call(
        functools.partial(_mixin_kernel, grp=grp),
        grid=(grp.n_prompt + 1,),
        in_specs=([grp.prompt(d), grp.sample(d, **_RESIDENT)] + grp.mods(d, layer, (3, 4))
                  + [pl.BlockSpec((1, d), const), pl.BlockSpec((d, na), const, **_RESIDENT),
                     pl.BlockSpec((d, nb), const, **_RESIDENT)]),
        out_specs=[grp.prompt(na), grp.prompt(nb), grp.sample(na), grp.sample(nb)],
        out_shape=[shape(xp, na), shape(xp, nb), shape(xs, na), shape(xs, nb)],
        compiler_params=_cparams(1),
        name="mixin",
    )(xp, xs, mod, mod, mod, mod, norm_g.reshape(1, d), w_a, w_b)


def _mixout_kernel(xp_ref, xs_ref, yap_ref, ybp_ref, yas_ref, ybs_ref, psh, psc, pgt, ssh, ssc, sgt, g_ref,
                   wgate_ref, bgate_ref, wbra_ref, wbrb_ref, wout_ref, op_ref, os_ref, *, grp):
    def body(refs, seq_row):
        x_ref, ya_ref, yb_ref, o_ref, sh_ref, sc_ref, gt_ref = refs
        tm, d = x_ref.shape
        x = x_ref[...]
        u = _modnorm(x, g_ref[...], _rows(sh_ref, tm, seq_row), _rows(sc_ref, tm, seq_row)).astype(BF16)
        gates = _sigmoid(jnp.dot(u, wgate_ref[...], preferred_element_type=F32) + bgate_ref[...])
        ma = _dot(ya_ref[...], wbra_ref[...])
        mb = _dot(yb_ref[...], wbrb_ref[...])
        merged = gates[:, :d] * ma + gates[:, d:] * mb
        o_ref[...] = x + _rows(gt_ref, tm, seq_row) * _dot(merged, wout_ref[...])

    grp.run(body, (xp_ref, yap_ref, ybp_ref, op_ref, psh, psc, pgt), (xs_ref, yas_ref, ybs_ref, os_ref, ssh, ssc, sgt))


def _mixout_call(xp, xs, grp, mod, layer, norm_g, ya_p, yb_p, ya_s, yb_s, w_gate, b_gate, w_br_a, w_br_b, w_out):
    d = xp.shape[1]
    wa, wb = ya_p.shape[1], yb_p.shape[1]
    const = lambda i: (0, 0)
    lyr = lambda i: (layer, 0, 0)
    return pl.pallas_call(
        functools.partial(_mixout_kernel, grp=grp),
        grid=(grp.n_prompt + 1,),
        in_specs=([grp.prompt(d), grp.sample(d, **_RESIDENT), grp.prompt(wa), grp.prompt(wb),
                   grp.sample(wa, **_RESIDENT), grp.sample(wb, **_RESIDENT)] + grp.mods(d, layer, (3, 4, 5))
                  + [pl.BlockSpec((1, d), const),
                     pl.BlockSpec((None, d, 2 * d), lyr, **_RESIDENT), pl.BlockSpec((None, 1, 2 * d), lyr),
                     pl.BlockSpec((None, wa, d), lyr, **_RESIDENT), pl.BlockSpec((None, wb, d), lyr, **_RESIDENT),
                     pl.BlockSpec((None, d, d), lyr, **_RESIDENT)]),
        out_specs=[grp.prompt(d), grp.sample(d)],
        out_shape=[jax.ShapeDtypeStruct(xp.shape, F32), jax.ShapeDtypeStruct(xs.shape, F32)],
        compiler_params=_cparams(1),
        name="mixout",
    )(xp, xs, ya_p, yb_p, ya_s, yb_s, *([mod] * 6), norm_g.reshape(1, d), w_gate, b_gate.reshape(-1, 1, 2 * d),
      w_br_a, w_br_b, w_out)


def _rwkv_prep(xs, aw, seg_ones, w0, w2, a0, a2, g2, k_k, k_a):
    n_dec, n_aaa = w2.shape[0], a2.shape[0]
    r = xs[:, 0:aw]
    k = xs[:, aw:2 * aw]
    v = xs[:, 2 * aw:3 * aw]
    o = 3 * aw
    wl = xs[:, o:o + n_dec]
    al = xs[:, o + n_dec:o + n_dec + n_aaa]
    gl = xs[:, o + n_dec + n_aaa:]
    logw = -DECAY_SCALE * _sigmoid(w0 + _dot(jnp.tanh(wl), w2))
    asig = _sigmoid(a0 + _dot(al, a2))
    g = _dot(_sigmoid(gl), g2)
    kk = k * k_k
    kk = kk * lax.rsqrt(jnp.maximum(_seg_sum(kk * kk, seg_ones), 1e-24))
    k = k * (1.0 + (asig - 1.0) * k_a)
    return r, k, v, logw, -kk, kk * asig, g


def _pool_group_sum(read_rows, w):
    s = read_rows(0)
    for j in range(1, w):
        s = s + read_rows(j)
    return s


PROMPT_BLOCK = 128
PROMPT_SEQS = 4
SHIFT_PAD = 8
POOL_PAD = 24


def _pool_window_sums(xbuf, e2, e4, e8, s, tb):
    gw = xbuf.shape[2] // len(POOL_WINDOWS)
    lo, hi = POOL_PAD - 16, POOL_PAD + tb
    e2[s, lo:hi, :] = xbuf[s, lo:hi, :] + xbuf[s, lo - 1:hi - 1, :]
    e4[s, lo:hi, :] = e2[s, lo:hi, gw:] + e2[s, lo - 2:hi - 2, gw:]
    e8[s, lo:hi, :] = e4[s, lo:hi, gw:] + e4[s, lo - 4:hi - 4, gw:]
    blk = slice(POOL_PAD, POOL_PAD + tb)
    back8 = slice(POOL_PAD - 8, POOL_PAD - 8 + tb)
    return (e2[s, blk, 0:gw], e4[s, blk, 0:gw], e8[s, blk, 0:gw], e8[s, blk, gw:] + e8[s, back8, gw:])


def _prompt_mix_kernel(pa_ref, pb_ref, mu_ref, w0_ref, w2_ref, a0_ref, a2_ref, g2_ref, kk_ref, ka_ref, rk_ref,
                       lnw_ref, lnb_ref, wpool_ref, pscale_ref, seg_ref,
                       ya_ref, yb_ref, wkv_ref,
                       pbuf, xbuf, e2, e4, e8, s_sc, r_sc, k_sc, v_sc, a_sc, b_sc, lw_sc, g_sc, y_sc, *, n_blocks):
    ns, tb, aw = ya_ref.shape
    bw = yb_ref.shape[2]
    t = WKV_CHUNK
    n_pairs = aw // LANES
    blk = pl.program_id(1)
    seqs = range(ns)

    @pl.when(blk == 0)
    def _():
        pbuf[:, 0:SHIFT_PAD, :] = jnp.zeros((ns, SHIFT_PAD, pbuf.shape[2]), F32)
        xbuf[:, 0:POOL_PAD, :] = jnp.zeros((ns, POOL_PAD, bw), F32)
        e2[:, 0:POOL_PAD, :] = jnp.zeros((ns, POOL_PAD, e2.shape[2]), F32)
        e4[:, 0:POOL_PAD, :] = jnp.zeros((ns, POOL_PAD, e4.shape[2]), F32)
        s_sc[...] = jnp.zeros_like(s_sc)

    xs = []
    for s in seqs:
        pa = pa_ref[s]
        pbuf[s, SHIFT_PAD:SHIFT_PAD + tb, :] = pa
        prev = pbuf[s, SHIFT_PAD - 1:SHIFT_PAD - 1 + tb, :]
        xs.append(pa + (prev - pa) * mu_ref[...])
        pbuf[s, 0:SHIFT_PAD, :] = pbuf[s, tb:tb + SHIFT_PAD, :]
    xs = jnp.concatenate(xs, axis=0)

    seg_ones = seg_ref[...]
    r, k, v, logw, a, b, g = _rwkv_prep(xs, aw, seg_ones, w0_ref[...], w2_ref[...], a0_ref[...], a2_ref[...],
                                        g2_ref[...], kk_ref[...], ka_ref[...])
    r_sc[...] = r
    k_sc[...] = k
    v_sc[...] = v
    a_sc[...] = a
    b_sc[...] = b
    lw_sc[...] = logw
    g_sc[...] = g

    t2 = 2 * t
    ri = lax.broadcasted_iota(jnp.int32, (t, t), 0)
    ci = lax.broadcasted_iota(jnp.int32, (t, t), 1)
    tri_incl = (ri >= ci).astype(BF16)
    rs = lax.broadcasted_iota(jnp.int32, (t2, t2), 0)
    cs = lax.broadcasted_iota(jnp.int32, (t2, t2), 1)
    strict = cs < rs
    rs2 = lax.broadcasted_iota(jnp.int32, (t2, 2 * t2), 0)
    cs2 = lax.broadcasted_iota(jnp.int32, (t2, 2 * t2), 1)
    incl2 = jnp.where(cs2 >= t2, cs2 - t2, cs2) <= rs2
    lane = lax.broadcasted_iota(jnp.int32, (1, LANES), 1)
    head_mask = (lane < HEAD_DIM, lane >= HEAD_DIM)
    units = [(s, p) for s in seqs for p in range(n_pairs)]
    n_levels = int(math.log2(t))

    def stack(x):
        return jnp.concatenate([jnp.where(head_mask[0], x, 0.0), jnp.where(head_mask[1], x, 0.0)], axis=0)

    def chunk(ci_, carry):
        at, rt, bt, kt, vv, p_end, sls = [], [], [], [], [], [], []
        for s in seqs:
            sl = pl.ds(pl.multiple_of(s * tb + ci_ * t, t), t)
            lw = lw_sc[sl, :]
            hi = lw.astype(BF16)
            lo = (lw - hi.astype(F32)).astype(BF16)
            c = (jnp.dot(tri_incl, hi, preferred_element_type=F32)
                 + jnp.dot(tri_incl, lo, preferred_element_type=F32))
            e_in = jnp.exp(c)
            e_neg = jnp.exp(-c)
            p_end.append(e_in[t - 1:t, :])
            at.append(a_sc[sl, :] * jnp.exp(c - lw))
            rt.append(r_sc[sl, :] * e_in)
            bt.append(b_sc[sl, :] * e_neg)
            kt.append(k_sc[sl, :] * e_neg)
            vv.append(v_sc[sl, :])
            sls.append(sl)
        cols = [slice(p * LANES, (p + 1) * LANES) for p in range(n_pairs)]
        un = range(len(units))
        pe = [p_end[s][:, cols[p]] for s, p in units]
        a2 = [stack(at[s][:, cols[p]]).astype(BF16) for s, p in units]
        r2 = [stack(rt[s][:, cols[p]]).astype(BF16) for s, p in units]
        v2b = [stack(vv[s][:, cols[p]]).astype(BF16) for s, p in units]
        bk = [jnp.concatenate([stack(bt[s][:, cols[p]]), stack(kt[s][:, cols[p]])], axis=0) for s, p in units]
        g_ = [_dot_nt(jnp.concatenate([a2[i], r2[i]], axis=0), bk[i]) for i in un]
        n = [jnp.where(strict, g_[i][:t2, :t2], 0.0) for i in un]
        ak = [jnp.where(strict, g_[i][:t2, t2:], 0.0).astype(BF16) for i in un]
        rbk = [jnp.where(incl2, g_[i][t2:, :], 0.0).astype(BF16) for i in un]
        xm = n
        pw = [_dot(n[i], n[i]) for i in un]
        for _ in range(n_levels - 2):
            res = [_dot(jnp.concatenate([pw[i], xm[i]], axis=0), pw[i]) for i in un]
            xm = [xm[i] + pw[i] + res[i][t2:] for i in un]
            pw = [res[i][:t2] for i in un]
        res = [_dot(xm[i], pw[i]) for i in un]
        xm = [(xm[i] + pw[i] + res[i]).astype(BF16) for i in un]
        s_old = [s_sc[s * n_pairs + p] for s, p in units]
        s_bf = [s_old[i].astype(BF16) for i in un]
        rhs = [_dot_nt(a2[i], s_bf[i]) for i in un]
        akv = [jnp.dot(ak[i], v2b[i], preferred_element_type=F32) for i in un]
        rhs = [rhs[i] + akv[i] for i in un]
        u = [rhs[i] + _dot(xm[i], rhs[i]) for i in un]
        uv = [jnp.concatenate([u[i].astype(BF16), v2b[i]], axis=0) for i in un]
        ys = [_dot_nt(r2[i], s_bf[i]) for i in un]
        yi = [jnp.dot(rbk[i], uv[i], preferred_element_type=F32) for i in un]
        sn = [_dot_tn(uv[i], bk[i] * pe[i]) for i in un]
        for i, (s, p) in enumerate(units):
            y2 = ys[i] + yi[i]
            y_sc[sls[s], cols[p]] = y2[:t] + y2[t:]
            s_sc[s * n_pairs + p] = s_old[i] * pe[i] + sn[i]
        return carry

    lax.fori_loop(0, tb // t, chunk, 0)

    y = y_sc[...]
    inv_n = 1.0 / HEAD_DIM
    mean = _seg_sum(y, seg_ones) * inv_n
    yc = y - mean
    var = _seg_sum(yc * yc, seg_ones) * inv_n
    yn = yc * lax.rsqrt(var + GN_EPS) * lnw_ref[...] + lnb_ref[...]
    bonus = _seg_sum(r_sc[...] * k_sc[...] * rk_ref[...], seg_ones) * v_sc[...]
    ya = (yn + bonus) * g_sc[...]
    for s in seqs:
        ya_ref[s] = ya[s * tb:(s + 1) * tb]

    pos = blk * tb + lax.broadcasted_iota(jnp.int32, (tb, 1), 0)
    gw = bw // len(POOL_WINDOWS)
    for s in seqs:
        pb = pb_ref[s]
        xbuf[s, POOL_PAD:POOL_PAD + tb, :] = pb
        sums = _pool_window_sums(xbuf, e2, e4, e8, s, tb)
        for gi, w in enumerate(POOL_WINDOWS):
            cols = slice(gi * gw, (gi + 1) * gw)
            cnt = jnp.minimum(w, pos + 1).astype(F32)
            dlt = sums[gi] / cnt - pb[:, cols]
            yb_ref[s, :, cols] = _dot(dlt, wpool_ref[gi]) * pscale_ref[:, cols]
        xbuf[s, POOL_PAD - 16:POOL_PAD, :] = xbuf[s, tb + POOL_PAD - 16:tb + POOL_PAD, :]

    @pl.when(blk == n_blocks - 1)
    def _():
        for s in seqs:
            for p in range(n_pairs):
                s_p = s_sc[s * n_pairs + p]
                for h in range(2):
                    o = h * HEAD_DIM
                    wkv_ref[s, 2 * p + h] = s_p[o:o + HEAD_DIM, o:o + HEAD_DIM]


def _prompt_mix_call(pa, pb, batch, seq, prm):
    na = pa.shape[1]
    bw = pb.shape[1]
    aw = prm["w0"].shape[1]
    n_heads = aw // HEAD_DIM
    tb, ns = min(PROMPT_BLOCK, seq), PROMPT_SEQS
    assert seq % tb == 0 and tb % WKV_CHUNK == 0 and aw % LANES == 0 and batch % ns == 0
    assert len(POOL_WINDOWS) == 4 and POOL_HIST < 16 <= POOL_PAD - 8
    n_blocks = seq // tb
    gw = bw // len(POOL_WINDOWS)
    row = lambda b, c: (b, c, 0)
    const2 = lambda b, c: (0, 0)
    const3 = lambda b, c: (0, 0, 0)
    full = lambda a: pl.BlockSpec(a.shape, const2 if a.ndim == 2 else const3)
    names = ("mu", "w0", "w2", "a0", "a2", "g2", "k_k", "k_a", "r_k", "ln_w", "ln_b", "w_pool", "pool_scale", "seg")
    params = [prm[n] for n in names]
    blk = lambda: pltpu.VMEM((ns * tb, aw), F32)
    ya, yb, wkv = pl.pallas_call(
        functools.partial(_prompt_mix_kernel, n_blocks=n_blocks),
        grid=(batch // ns, n_blocks),
        in_specs=[pl.BlockSpec((ns, tb, na), row), pl.BlockSpec((ns, tb, bw), row)] + [full(a) for a in params],
        out_specs=[pl.BlockSpec((ns, tb, aw), row), pl.BlockSpec((ns, tb, bw), row),
                   pl.BlockSpec((ns, n_heads, HEAD_DIM, HEAD_DIM), lambda b, c: (b, 0, 0, 0))],
        out_shape=[jax.ShapeDtypeStruct((batch, seq, aw), F32), jax.ShapeDtypeStruct((batch, seq, bw), F32),
                   jax.ShapeDtypeStruct((batch, n_heads, HEAD_DIM, HEAD_DIM), F32)],
        scratch_shapes=[pltpu.VMEM((ns, tb + SHIFT_PAD, na), F32), pltpu.VMEM((ns, tb + POOL_PAD, bw), F32),
                        pltpu.VMEM((ns, tb + POOL_PAD, bw), F32), pltpu.VMEM((ns, tb + POOL_PAD, bw - gw), F32),
                        pltpu.VMEM((ns, tb + POOL_PAD, bw - 2 * gw), F32),
                        pltpu.VMEM((ns * (aw // LANES), LANES, LANES), F32),
                        blk(), blk(), blk(), blk(), blk(), blk(), blk(), blk()],
        compiler_params=_cparams(2),
        name="mix_prompt",
    )(pa.reshape(batch, seq, na), pb.reshape(batch, seq, bw), *params)
    return ya.reshape(batch * seq, aw), yb.reshape(batch * seq, bw), wkv


def _sample_prep_kernel(pa_ref, sh0_ref, pb_ref, hist_ref, mu_ref, w0_ref, w2_ref, a0_ref, a2_ref, g2_ref,
                        kk_ref, ka_ref, wpool_ref, pscale_ref, seg_ref,
                        r_ref, w_ref, k_ref, v_ref, a_ref, b_ref, g_ref, yb_ref, *, batch, seq):
    aw = r_ref.shape[1]
    bw = yb_ref.shape[1]
    pa = pa_ref[...]
    prev = jnp.concatenate([sh0_ref[...], pa[:(seq - 1) * batch]], axis=0)
    xs = pa + (prev - pa) * mu_ref[...]
    r, k, v, logw, a, b, g = _rwkv_prep(xs, aw, seg_ref[...], w0_ref[...], w2_ref[...], a0_ref[...], a2_ref[...],
                                        g2_ref[...], kk_ref[...], ka_ref[...])
    outs = ((r_ref, r), (w_ref, jnp.exp(logw)), (k_ref, k), (v_ref, v), (a_ref, a), (b_ref, b), (g_ref, g))
    for ref, val in outs:
        for ti in range(seq):
            ref[ti] = val[ti * batch:(ti + 1) * batch, :].T

    gw = bw // len(POOL_WINDOWS)

    def slab(i, cols):
        if i < POOL_HIST:
            return hist_ref[i * batch:(i + 1) * batch, cols]
        return pb_ref[(i - POOL_HIST) * batch:(i - POOL_HIST + 1) * batch, cols]

    for ti in range(seq):
        rows = slice(ti * batch, (ti + 1) * batch)
        for gi, w in enumerate(POOL_WINDOWS):
            cols = slice(gi * gw, (gi + 1) * gw)
            s = _pool_group_sum(lambda j: slab(POOL_HIST + ti - j, cols), w)
            cnt = float(min(w, PAST_LEN + ti + 1))
            dlt = s / cnt - pb_ref[rows, cols]
            yb_ref[rows, cols] = _dot(dlt, wpool_ref[gi]) * pscale_ref[:, cols]


def _sample_prep_call(pa, sh0, pb, hist, batch, seq, prm):
    rows = pa.shape[0]
    aw = prm["w0"].shape[1]
    bw = pb.shape[1]
    names = ("mu", "w0", "w2", "a0", "a2", "g2", "k_k", "k_a", "w_pool", "pool_scale", "seg")
    args = [pa, sh0, pb, hist] + [prm[n] for n in names]
    full = lambda a: pl.BlockSpec(a.shape, (lambda i: (0, 0)) if a.ndim == 2 else (lambda i: (0, 0, 0)))
    o = jax.ShapeDtypeStruct((seq, aw, batch), F32)
    ob = jax.ShapeDtypeStruct((rows, bw), F32)
    return pl.pallas_call(
        functools.partial(_sample_prep_kernel, batch=batch, seq=seq),
        grid=(1,),
        in_specs=[full(a) for a in args],
        out_specs=[full(o)] * 7 + [full(ob)],
        out_shape=[o] * 7 + [ob],
        compiler_params=_cparams(1),
        name="prep_sample",
    )(*args)


SAMPLE_VROWS = 16


def _sample_wkv_kernel(r_ref, w_ref, k_ref, v_ref, a_ref, b_ref, g_ref, rk_ref, lnw_ref, lnb_ref, s_ref, all_ref,
                       y_ref, so_ref, y_sc):
    del all_ref
    seq, hd, batch = r_ref.shape
    vb = s_ref.shape[0]
    h, vs = pl.program_id(0), pl.program_id(1)
    hrow = pl.multiple_of((h % 2) * hd, hd)
    for q in range(vb):
        tile = s_ref[q]
        for ti in range(seq):
            sa = jnp.sum(tile * a_ref[ti], axis=0, keepdims=True)
            tile = tile * w_ref[ti] + sa * b_ref[ti] + v_ref[ti, pl.ds(vs * vb + q, 1), :] * k_ref[ti]
            y_sc[ti, pl.ds(hrow + vs * vb + q, 1), :] = jnp.sum(tile * r_ref[ti], axis=0, keepdims=True)
        so_ref[q] = tile

    @pl.when(vs == pl.num_programs(1) - 1)
    def _():
        for ti in range(seq):
            rows = pl.ds(hrow, hd)
            y = y_sc[ti, rows, :]
            yc = y - jnp.mean(y, axis=0, keepdims=True)
            var = jnp.mean(yc * yc, axis=0, keepdims=True)
            yn = yc * lax.rsqrt(var + GN_EPS) * lnw_ref[...] + lnb_ref[...]
            bonus = jnp.sum(r_ref[ti] * k_ref[ti] * rk_ref[...], axis=0, keepdims=True) * v_ref[ti]
            y_sc[ti, rows, :] = (yn + bonus) * g_ref[ti]

    @pl.when((vs == pl.num_programs(1) - 1) & (h % 2 == 1))
    def _():
        for ti in range(seq):
            y_ref[ti * batch:(ti + 1) * batch, :] = y_sc[ti].T


def _sample_wkv_call(vecs, state_t, new_states, layer, batch, seq, prm):
    n_heads, hd = state_t.shape[1:3]
    aw = n_heads * hd
    vb = SAMPLE_VROWS
    assert 2 * hd == LANES and hd % vb == 0
    vspec = pl.BlockSpec((seq, hd, batch), lambda h, s: (0, h, 0))
    cspec = pl.BlockSpec((hd, 1), lambda h, s: (h, 0))
    sspec = pl.BlockSpec((None, None, vb, hd, batch), lambda h, s: (layer, h, s, 0, 0))
    cols = [prm[n].reshape(aw, 1) for n in ("r_k", "ln_w", "ln_b")]
    args = [*vecs, *cols, state_t, new_states]
    return pl.pallas_call(
        _sample_wkv_kernel,
        grid=(n_heads, hd // vb),
        in_specs=[vspec] * 7 + [cspec] * 3 + [sspec, pl.BlockSpec(memory_space=pl.ANY)],
        out_specs=[pl.BlockSpec((seq * batch, LANES), lambda h, s: (0, h // 2)), sspec],
        out_shape=[jax.ShapeDtypeStruct((seq * batch, aw), F32), jax.ShapeDtypeStruct(state_t.shape, F32)],
        scratch_shapes=[pltpu.VMEM((seq, LANES, batch), F32)],
        input_output_aliases={len(args) - 1: 1},
        compiler_params=_cparams(2),
        name="wkv_sample",
    )(*args)


def kernel(x_prompt, x_sample, state_wkv, state_shift, state_pool, c_prompt, c_sample, norm_g, w_mod, b_mod, w_ffn_in, w_ffn_out, w_in, mu_shift, w0, w2, a0, a2, g2, k_k, k_a, r_k, ln_x_w, ln_x_b, w_pool, pool_scale, w_br_a, w_br_b, w_gate, b_gate, w_out, final_g):
    bp, lp, d = x_prompt.shape
    bs, ls, _ = x_sample.shape
    depth = w_mod.shape[0]
    aw = w0.shape[1]
    a_proj = mu_shift.shape[1]
    bw = pool_scale.shape[1]

    bf = lambda w: w.astype(BF16)
    w_ffn_in_b, w_ffn_out_b = bf(w_ffn_in), bf(w_ffn_out)
    w_in_a, w_in_b = bf(w_in[:, :, :a_proj]), bf(w_in[:, :, a_proj:])
    w_gate_b, w_br_a_b, w_br_b_b, w_out_b = bf(w_gate), bf(w_br_a), bf(w_br_b), bf(w_out)
    w2_b, a2_b, g2_b, w_pool_b = bf(w2), bf(a2), bf(g2), bf(w_pool)
    head_of = jnp.arange(MXU_TILE, dtype=jnp.int32) // HEAD_DIM
    seg_ones = (head_of[:, None] == head_of[None, :]).astype(BF16)

    mod = _mod_call(jnp.concatenate([c_sample, c_prompt], axis=0), w_mod, b_mod)
    grp = _Groups(bp, lp, bs, ls, min(ROW_TILE, lp), mod_row0=bs)

    xp = x_prompt.reshape(bp * lp, d)
    xs = jnp.transpose(x_sample, (1, 0, 2)).reshape(ls * bs, d)
    hist_s = jnp.transpose(state_pool, (0, 2, 1, 3)).reshape(depth, POOL_HIST * bs, bw)
    state_t = jnp.transpose(state_wkv, (0, 2, 3, 4, 1))
    wkv_s = jnp.zeros_like(state_t)

    wkv_p, shift_p, pool_p, shift_s, pool_s = [], [], [], [], []
    for l in range(depth):
        prm = dict(mu=mu_shift[l][None], w0=w0[l][None], w2=w2_b[l], a0=a0[l][None], a2=a2_b[l], g2=g2_b[l],
                   k_k=k_k[l][None], k_a=k_a[l][None], r_k=r_k[l].reshape(1, aw), ln_w=ln_x_w[l][None],
                   ln_b=ln_x_b[l][None], w_pool=w_pool_b[l], pool_scale=pool_scale[l][None], seg=seg_ones)
        xp, xs = _ffn_call(xp, xs, grp, mod, l, 0, norm_g[l, 0], w_ffn_in_b, w_ffn_out_b)
        pa_p, pb_p, pa_s, pb_s = _mixin_call(xp, xs, grp, mod, l, norm_g[l, 1], w_in_a[l], w_in_b[l])
        ya_p, yb_p, s_new = _prompt_mix_call(pa_p, pb_p, bp, lp, prm)
        *vecs, yb_s = _sample_prep_call(pa_s, state_shift[l, :, 0, :], pb_s, hist_s[l], bs, ls, prm)
        ya_s, wkv_s = _sample_wkv_call(vecs, state_t, wkv_s, l, bs, ls, prm)
        xp, xs = _mixout_call(xp, xs, grp, mod, l, norm_g[l, 1], ya_p, yb_p, ya_s, yb_s, w_gate_b, b_gate, w_br_a_b,
                              w_br_b_b, w_out_b)
        xp, xs = _ffn_call(xp, xs, grp, mod, l, 2, norm_g[l, 2], w_ffn_in_b, w_ffn_out_b,
                           final_g if l == depth - 1 else None)
        wkv_p.append(s_new)
        shift_p.append(pa_p.reshape(bp, lp, a_proj)[:, -1:])
        pool_p.append(pb_p.reshape(bp, lp, bw)[:, -POOL_HIST:])
        shift_s.append(pa_s.reshape(ls, bs, a_proj)[-1][:, None, :])
        pb_bm = jnp.transpose(pb_s.reshape(ls, bs, bw), (1, 0, 2))
        pool_s.append(jnp.concatenate([state_pool[l], pb_bm], axis=1)[:, -POOL_HIST:])

    y_prompt = xp.reshape(bp, lp, d)
    y_sample = jnp.transpose(xs.reshape(ls, bs, d), (1, 0, 2))
    return (y_prompt, y_sample, jnp.stack(wkv_p), jnp.stack(shift_p), jnp.stack(pool_p),
            jnp.transpose(wkv_s, (0, 4, 1, 2, 3)), jnp.stack(shift_s), jnp.stack(pool_s))
```

```python
import functools
import math

import jax
import jax.numpy as jnp
from jax import lax
from jax.experimental import pallas as pl
from jax.experimental.pallas import tpu as pltpu

F32 = jnp.float32
BF16 = jnp.bfloat16

HEAD_DIM = 64
POOL_WINDOWS = (2, 4, 8, 16)
POOL_HIST = max(POOL_WINDOWS) - 1
N_SUB = 3
NORM_EPS = 1e-6
GN_EPS = 64e-5
PAST_LEN = 16384
DECAY_SCALE = math.exp(-0.5)

LANES = 128
MXU_TILE = 256
ROW_TILE = 1024
WKV_CHUNK = 64
VMEM_LIMIT_BYTES = 56 * 1024 * 1024


_RESIDENT = dict(pipeline_mode=pl.Buffered(1))


def _cparams(n_axes, **kw):
    return pltpu.CompilerParams(dimension_semantics=("arbitrary",) * n_axes,
                                vmem_limit_bytes=VMEM_LIMIT_BYTES, **kw)


def _sigmoid(x):
    return 1.0 / (1.0 + jnp.exp(-x))


def _dot(a, b):
    return jnp.dot(a.astype(BF16), b.astype(BF16), preferred_element_type=F32)


def _dot_nt(a, b):
    return lax.dot_general(a.astype(BF16), b.astype(BF16), (((1,), (1,)), ((), ())),
                           preferred_element_type=F32)


def _dot_tn(a, b):
    return lax.dot_general(a.astype(BF16), b.astype(BF16), (((0,), (0,)), ((), ())),
                           preferred_element_type=F32)


def _seg_sum(x, seg_ones):
    n, width = x.shape
    tw = seg_ones.shape[0]
    outs = []
    for c0 in range(0, width, tw):
        xc = x[:, c0:c0 + tw]
        hi = xc.astype(BF16)
        lo = (xc - hi.astype(F32)).astype(BF16)
        both = jnp.dot(jnp.concatenate([hi, lo], axis=0), seg_ones, preferred_element_type=F32)
        outs.append(both[:n] + both[n:])
    return jnp.concatenate(outs, axis=1)


def _rms(x):
    return x * lax.rsqrt(jnp.mean(x * x, axis=-1, keepdims=True) + NORM_EPS)


SUBLANES = 8


def _rows(ref, tm, seq_row):
    if seq_row is not None:
        return ref[pl.ds(seq_row, 1), :]
    m = ref[...]
    return m if m.shape[0] == tm else jnp.concatenate([m] * (tm // m.shape[0]), axis=0)


def _modnorm(x, g, shift, scale):
    return _rms(x) * g * (1.0 + scale) + shift


def _mod_kernel(c_ref, w_ref, b_ref, o_ref):
    c = c_ref[...]
    o_ref[...] = _dot(c * _sigmoid(c), w_ref[...]) + b_ref[...]


def _mod_call(c_all, w_mod, b_mod):
    depth, d, n9 = w_mod.shape
    rows = c_all.shape[0]
    n_sub9 = n9 // d
    tn = d
    return pl.pallas_call(
        _mod_kernel,
        grid=(depth, n_sub9),
        in_specs=[pl.BlockSpec((rows, d), lambda l, j: (0, 0)),
                  pl.BlockSpec((None, d, tn), lambda l, j: (l, 0, j)),
                  pl.BlockSpec((None, 1, tn), lambda l, j: (l, 0, j))],
        out_specs=pl.BlockSpec((None, None, rows, tn), lambda l, j: (l, j, 0, 0)),
        out_shape=jax.ShapeDtypeStruct((depth, n_sub9, rows, d), F32),
        compiler_params=_cparams(2),
        name="mod",
    )(c_all, w_mod, b_mod.reshape(depth, 1, n9))


class _Groups:
    def __init__(self, bp, lp, bs, ls, tm, mod_row0):
        assert lp % tm == 0
        self.tm, self.ts, self.bs = tm, bs * ls, bs
        self.n_prompt = bp * lp // tm
        self.seq_rows = (mod_row0, lp // tm)

    def _ptile(self, i):
        return jnp.minimum(i, self.n_prompt - 1)

    def prompt(self, width):
        return pl.BlockSpec((self.tm, width), lambda i: (self._ptile(i), 0))

    def sample(self, width, **kw):
        return pl.BlockSpec((self.ts, width), lambda i: (0, 0), **kw)

    def prompt_mod(self, d, layer, idx):
        row0, per_seq = self.seq_rows
        return pl.BlockSpec((None, None, SUBLANES, d),
                            lambda i: (layer, idx, (row0 + self._ptile(i) // per_seq) // SUBLANES, 0))

    def sample_mod(self, d, layer, idx):
        return pl.BlockSpec((None, None, self.bs, d), lambda i: (layer, idx, 0, 0), **_RESIDENT)

    def mods(self, d, layer, idxs):
        return ([self.prompt_mod(d, layer, k) for k in idxs] + [self.sample_mod(d, layer, k) for k in idxs])

    def run(self, body, prompt_refs, sample_refs):
        i = pl.program_id(0)
        row0, per_seq = self.seq_rows
        seq_row = (row0 + i // per_seq) % SUBLANES
        pl.when(i < self.n_prompt)(lambda: body(prompt_refs, seq_row))
        pl.when(i == self.n_prompt)(lambda: body(sample_refs, None))


FFN_CHUNK = 2 * MXU_TILE
FFN_STAGE = MXU_TILE


def _ffn_kernel(xp_ref, xs_ref, psh, psc, pgt, ssh, ssc, sgt, g_ref, win_hbm, wo_hbm, *rest, final, grp, layer, half):
    if final:
        fg_ref, yp_ref, ys_ref, win_bf, wo_bf, stg_in, stg_out, sem = rest
    else:
        yp_ref, ys_ref, win_bf, wo_bf, stg_in, stg_out, sem = rest
    d_ff = wo_bf.shape[0]
    n_pieces = d_ff // FFN_STAGE

    def pieces(k, slot):
        cols = pl.ds(k * FFN_STAGE, FFN_STAGE)
        ucols = pl.ds(d_ff + k * FFN_STAGE, FFN_STAGE)
        return (pltpu.make_async_copy(win_hbm.at[layer, half, :, cols], stg_in.at[slot, 0], sem.at[slot, 0]),
                pltpu.make_async_copy(win_hbm.at[layer, half, :, ucols], stg_in.at[slot, 1], sem.at[slot, 1]),
                pltpu.make_async_copy(wo_hbm.at[layer, half, cols, :], stg_out.at[slot], sem.at[slot, 2]))

    def fetch(k):
        for cp in pieces(k, k % 2):
            cp.start()

    def land(k):
        slot = k % 2
        for cp in pieces(k, slot):
            cp.wait()
        c0, c1 = k * FFN_STAGE, (k + 1) * FFN_STAGE
        win_bf[:, c0:c1] = stg_in[slot, 0].astype(BF16)
        win_bf[:, d_ff + c0:d_ff + c1] = stg_in[slot, 1].astype(BF16)
        wo_bf[c0:c1, :] = stg_out[slot].astype(BF16)

    def body(refs, seq_row, load_weights=False):
        x_ref, o_ref, sh_ref, sc_ref, gt_ref = refs
        tm = x_ref.shape[0]
        if load_weights:
            fetch(0)
        x = x_ref[...]
        u = _modnorm(x, g_ref[...], _rows(sh_ref, tm, seq_row), _rows(sc_ref, tm, seq_row)).astype(BF16)
        landed = 0
        for c0 in range(0, d_ff, FFN_CHUNK):
            c1 = min(c0 + FFN_CHUNK, d_ff)
            if load_weights:
                while landed * FFN_STAGE < c1:
                    if landed + 1 < n_pieces:
                        fetch(landed + 1)
                    land(landed)
                    landed += 1
            hg = jnp.dot(u, win_bf[:, c0:c1], preferred_element_type=F32)
            hu = jnp.dot(u, win_bf[:, d_ff + c0:d_ff + c1], preferred_element_type=F32)
            act = (hg * _sigmoid(hg) * hu).astype(BF16)
            part = jnp.dot(act, wo_bf[c0:c1, :], preferred_element_type=F32)
            if c0 == 0:
                o_ref[...] = part
            else:
                o_ref[...] += part
        y = x + 0.5 * _rows(gt_ref, tm, seq_row) * o_ref[...]
        if final:
            y = _rms(y) * fg_ref[...]
        o_ref[...] = y

    i = pl.program_id(0)
    row0, per_seq = grp.seq_rows
    seq_row = (row0 + i // per_seq) % SUBLANES
    prompt_refs = (xp_ref, yp_ref, psh, psc, pgt)
    pl.when(i == 0)(lambda: body(prompt_refs, seq_row, load_weights=True))
    pl.when((i > 0) & (i < grp.n_prompt))(lambda: body(prompt_refs, seq_row))
    pl.when(i == grp.n_prompt)(lambda: body((xs_ref, ys_ref, ssh, ssc, sgt), None))


def _ffn_call(xp, xs, grp, mod, layer, sub, norm_g, w_in, w_out, final_g=None):
    d = xp.shape[1]
    d_ff = w_out.shape[2]
    assert d_ff % FFN_STAGE == 0 and FFN_CHUNK % FFN_STAGE == 0 and grp.ts <= grp.tm and grp.n_prompt >= 1
    final = final_g is not None
    const = lambda i: (0, 0)
    hbm = pl.BlockSpec(memory_space=pl.ANY)
    in_specs = ([grp.prompt(d), grp.sample(d, **_RESIDENT)] + grp.mods(d, layer, (3 * sub, 3 * sub + 1, 3 * sub + 2))
                + [pl.BlockSpec((1, d), const), hbm, hbm])
    args = [xp, xs] + [mod] * 6 + [norm_g.reshape(1, d), w_in, w_out]
    if final:
        in_specs.append(pl.BlockSpec((1, d), const))
        args.append(final_g.reshape(1, d))
    return pl.pallas_call(
        functools.partial(_ffn_kernel, final=final, grp=grp, layer=layer, half=sub // 2),
        grid=(grp.n_prompt + 1,),
        in_specs=in_specs,
        out_specs=[grp.prompt(d), grp.sample(d)],
        out_shape=[jax.ShapeDtypeStruct(xp.shape, F32), jax.ShapeDtypeStruct(xs.shape, F32)],
        scratch_shapes=[pltpu.VMEM((d, 2 * d_ff), BF16), pltpu.VMEM((d_ff, d), BF16),
                        pltpu.VMEM((2, 2, d, FFN_STAGE), F32), pltpu.VMEM((2, FFN_STAGE, d), F32),
                        pltpu.SemaphoreType.DMA((2, 3))],
        compiler_params=_cparams(1),
        name="ffn",
    )(*args)


def _mixin_kernel(xp_ref, xs_ref, psh, psc, ssh, ssc, g_ref, wa_ref, wb_ref, pap_ref, pbp_ref, pas_ref, pbs_ref, *, grp):
    def body(refs, seq_row):
        x_ref, pa_ref, pb_ref, sh_ref, sc_ref = refs
        tm = x_ref.shape[0]
        u = _modnorm(x_ref[...], g_ref[...], _rows(sh_ref, tm, seq_row), _rows(sc_ref, tm, seq_row)).astype(BF16)
        pa_ref[...] = jnp.dot(u, wa_ref[...], preferred_element_type=F32)
        pb_ref[...] = jnp.dot(u, wb_ref[...], preferred_element_type=F32)

    grp.run(body, (xp_ref, pap_ref, pbp_ref, psh, psc), (xs_ref, pas_ref, pbs_ref, ssh, ssc))


def _mixin_call(xp, xs, grp, mod, layer, norm_g, w_a, w_b):
    d = xp.shape[1]
    na, nb = w_a.shape[1], w_b.shape[1]
    const = lambda i: (0, 0)
    shape = lambda x, n: jax.ShapeDtypeStruct((x.shape[0], n), F32)
    return pl.pallas_call(
        functools.partial(_mixin_kernel, grp=grp),
        grid=(grp.n_prompt + 1,),
        in_specs=([grp.prompt(d), grp.sample(d, **_RESIDENT)] + grp.mods(d, layer, (3, 4))
                  + [pl.BlockSpec((1, d), const), pl.BlockSpec((d, na), const, **_RESIDENT),
                     pl.BlockSpec((d, nb), const, **_RESIDENT)]),
        out_specs=[grp.prompt(na), grp.prompt(nb), grp.sample(na), grp.sample(nb)],
        out_shape=[shape(xp, na), shape(xp, nb), shape(xs, na), shape(xs, nb)],
        compiler_params=_cparams(1),
        name="mixin",
    )(xp, xs, mod, mod, mod, mod, norm_g.reshape(1, d), w_a, w_b)


def _mixout_kernel(xp_ref, xs_ref, yap_ref, ybp_ref, yas_ref, ybs_ref, psh, psc, pgt, ssh, ssc, sgt, g_ref,
                   wgate_ref, bgate_ref, wbra_ref, wbrb_ref, wout_ref, op_ref, os_ref, *, grp):
    def body(refs, seq_row):
        x_ref, ya_ref, yb_ref, o_ref, sh_ref, sc_ref, gt_ref = refs
        tm, d = x_ref.shape
        x = x_ref[...]
        u = _modnorm(x, g_ref[...], _rows(sh_ref, tm, seq_row), _rows(sc_ref, tm, seq_row)).astype(BF16)
        gates = _sigmoid(jnp.dot(u, wgate_ref[...], preferred_element_type=F32) + bgate_ref[...])
        ma = _dot(ya_ref[...], wbra_ref[...])
        mb = _dot(yb_ref[...], wbrb_ref[...])
        merged = gates[:, :d] * ma + gates[:, d:] * mb
        o_ref[...] = x + _rows(gt_ref, tm, seq_row) * _dot(merged, wout_ref[...])

    grp.run(body, (xp_ref, yap_ref, ybp_ref, op_ref, psh, psc, pgt), (xs_ref, yas_ref, ybs_ref, os_ref, ssh, ssc, sgt))


def _mixout_call(xp, xs, grp, mod, layer, norm_g, ya_p, yb_p, ya_s, yb_s, w_gate, b_gate, w_br_a, w_br_b, w_out):
    d = xp.shape[1]
    wa, wb = ya_p.shape[1], yb_p.shape[1]
    const = lambda i: (0, 0)
    lyr = lambda i: (layer, 0, 0)
    return pl.pallas_call(
        functools.partial(_mixout_kernel, grp=grp),
        grid=(grp.n_prompt + 1,),
        in_specs=([grp.prompt(d), grp.sample(d, **_RESIDENT), grp.prompt(wa), grp.prompt(wb),
                   grp.sample(wa, **_RESIDENT), grp.sample(wb, **_RESIDENT)] + grp.mods(d, layer, (3, 4, 5))
                  + [pl.BlockSpec((1, d), const),
                     pl.BlockSpec((None, d, 2 * d), lyr, **_RESIDENT), pl.BlockSpec((None, 1, 2 * d), lyr),
                     pl.BlockSpec((None, wa, d), lyr, **_RESIDENT), pl.BlockSpec((None, wb, d), lyr, **_RESIDENT),
                     pl.BlockSpec((None, d, d), lyr, **_RESIDENT)]),
        out_specs=[grp.prompt(d), grp.sample(d)],
        out_shape=[jax.ShapeDtypeStruct(xp.shape, F32), jax.ShapeDtypeStruct(xs.shape, F32)],
        compiler_params=_cparams(1),
        name="mixout",
    )(xp, xs, ya_p, yb_p, ya_s, yb_s, *([mod] * 6), norm_g.reshape(1, d), w_gate, b_gate.reshape(-1, 1, 2 * d),
      w_br_a, w_br_b, w_out)


def _rwkv_prep(xs, aw, seg_ones, w0, w2, a0, a2, g2, k_k, k_a):
    n_dec, n_aaa = w2.shape[0], a2.shape[0]
    r = xs[:, 0:aw]
    k = xs[:, aw:2 * aw]
    v = xs[:, 2 * aw:3 * aw]
    o = 3 * aw
    wl = xs[:, o:o + n_dec]
    al = xs[:, o + n_dec:o + n_dec + n_aaa]
    gl = xs[:, o + n_dec + n_aaa:]
    logw = -DECAY_SCALE * _sigmoid(w0 + _dot(jnp.tanh(wl), w2))
    asig = _sigmoid(a0 + _dot(al, a2))
    g = _dot(_sigmoid(gl), g2)
    kk = k * k_k
    kk = kk * lax.rsqrt(jnp.maximum(_seg_sum(kk * kk, seg_ones), 1e-24))
    k = k * (1.0 + (asig - 1.0) * k_a)
    return r, k, v, logw, -kk, kk * asig, g


def _pool_group_sum(read_rows, w):
    s = read_rows(0)
    for j in range(1, w):
        s = s + read_rows(j)
    return s


PROMPT_BLOCK = 128
PROMPT_SEQS = 4
SHIFT_PAD = 8
POOL_PAD = 24


def _pool_window_sums(xbuf, e2, e4, e8, s, tb):
    gw = xbuf.shape[2] // len(POOL_WINDOWS)
    lo, hi = POOL_PAD - 16, POOL_PAD + tb
    e2[s, lo:hi, :] = xbuf[s, lo:hi, :] + xbuf[s, lo - 1:hi - 1, :]
    e4[s, lo:hi, :] = e2[s, lo:hi, gw:] + e2[s, lo - 2:hi - 2, gw:]
    e8[s, lo:hi, :] = e4[s, lo:hi, gw:] + e4[s, lo - 4:hi - 4, gw:]
    blk = slice(POOL_PAD, POOL_PAD + tb)
    back8 = slice(POOL_PAD - 8, POOL_PAD - 8 + tb)
    return (e2[s, blk, 0:gw], e4[s, blk, 0:gw], e8[s, blk, 0:gw], e8[s, blk, gw:] + e8[s, back8, gw:])


def _prompt_mix_kernel(pa_ref, pb_ref, mu_ref, w0_ref, w2_ref, a0_ref, a2_ref, g2_ref, kk_ref, ka_ref, rk_ref,
                       lnw_ref, lnb_ref, wpool_ref, pscale_ref, seg_ref,
                       ya_ref, yb_ref, wkv_ref,
                       pbuf, xbuf, e2, e4, e8, s_sc, r_sc, k_sc, v_sc, a_sc, b_sc, lw_sc, g_sc, y_sc, *, n_blocks):
    ns, tb, aw = ya_ref.shape
    bw = yb_ref.shape[2]
    t = WKV_CHUNK
    n_pairs = aw // LANES
    blk = pl.program_id(1)
    seqs = range(ns)

    @pl.when(blk == 0)
    def _():
        pbuf[:, 0:SHIFT_PAD, :] = jnp.zeros((ns, SHIFT_PAD, pbuf.shape[2]), F32)
        xbuf[:, 0:POOL_PAD, :] = jnp.zeros((ns, POOL_PAD, bw), F32)
        e2[:, 0:POOL_PAD, :] = jnp.zeros((ns, POOL_PAD, e2.shape[2]), F32)
        e4[:, 0:POOL_PAD, :] = jnp.zeros((ns, POOL_PAD, e4.shape[2]), F32)
        s_sc[...] = jnp.zeros_like(s_sc)

    xs = []
    for s in seqs:
        pa = pa_ref[s]
        pbuf[s, SHIFT_PAD:SHIFT_PAD + tb, :] = pa
        prev = pbuf[s, SHIFT_PAD - 1:SHIFT_PAD - 1 + tb, :]
        xs.append(pa + (prev - pa) * mu_ref[...])
        pbuf[s, 0:SHIFT_PAD, :] = pbuf[s, tb:tb + SHIFT_PAD, :]
    xs = jnp.concatenate(xs, axis=0)

    seg_ones = seg_ref[...]
    r, k, v, logw, a, b, g = _rwkv_prep(xs, aw, seg_ones, w0_ref[...], w2_ref[...], a0_ref[...], a2_ref[...],
                                        g2_ref[...], kk_ref[...], ka_ref[...])
    r_sc[...] = r
    k_sc[...] = k
    v_sc[...] = v
    a_sc[...] = a
    b_sc[...] = b
    lw_sc[...] = logw
    g_sc[...] = g

    t2 = 2 * t
    ri = lax.broadcasted_iota(jnp.int32, (t, t), 0)
    ci = lax.broadcasted_iota(jnp.int32, (t, t), 1)
    tri_incl = (ri >= ci).astype(BF16)
    rs = lax.broadcasted_iota(jnp.int32, (t2, t2), 0)
    cs = lax.broadcasted_iota(jnp.int32, (t2, t2), 1)
    strict = cs < rs
    rs2 = lax.broadcasted_iota(jnp.int32, (t2, 2 * t2), 0)
    cs2 = lax.broadcasted_iota(jnp.int32, (t2, 2 * t2), 1)
    incl2 = jnp.where(cs2 >= t2, cs2 - t2, cs2) <= rs2
    lane = lax.broadcasted_iota(jnp.int32, (1, LANES), 1)
    head_mask = (lane < HEAD_DIM, lane >= HEAD_DIM)
    units = [(s, p) for s in seqs for p in range(n_pairs)]
    n_levels = int(math.log2(t))

    def stack(x):
        return jnp.concatenate([jnp.where(head_mask[0], x, 0.0), jnp.where(head_mask[1], x, 0.0)], axis=0)

    def chunk(ci_, carry):
        at, rt, bt, kt, vv, p_end, sls = [], [], [], [], [], [], []
        for s in seqs:
            sl = pl.ds(pl.multiple_of(s * tb + ci_ * t, t), t)
            lw = lw_sc[sl, :]
            hi = lw.astype(BF16)
            lo = (lw - hi.astype(F32)).astype(BF16)
            c = (jnp.dot(tri_incl, hi, preferred_element_type=F32)
                 + jnp.dot(tri_incl, lo, preferred_element_type=F32))
            e_in = jnp.exp(c)
            e_neg = jnp.exp(-c)
            p_end.append(e_in[t - 1:t, :])
            at.append(a_sc[sl, :] * jnp.exp(c - lw))
            rt.append(r_sc[sl, :] * e_in)
            bt.append(b_sc[sl, :] * e_neg)
            kt.append(k_sc[sl, :] * e_neg)
            vv.append(v_sc[sl, :])
            sls.append(sl)
        cols = [slice(p * LANES, (p + 1) * LANES) for p in range(n_pairs)]
        un = range(len(units))
        pe = [p_end[s][:, cols[p]] for s, p in units]
        a2 = [stack(at[s][:, cols[p]]).astype(BF16) for s, p in units]
        r2 = [stack(rt[s][:, cols[p]]).astype(BF16) for s, p in units]
        v2b = [stack(vv[s][:, cols[p]]).astype(BF16) for s, p in units]
        bk = [jnp.concatenate([stack(bt[s][:, cols[p]]), stack(kt[s][:, cols[p]])], axis=0) for s, p in units]
        g_ = [_dot_nt(jnp.concatenate([a2[i], r2[i]], axis=0), bk[i]) for i in un]
        n = [jnp.where(strict, g_[i][:t2, :t2], 0.0) for i in un]
        ak = [jnp.where(strict, g_[i][:t2, t2:], 0.0).astype(BF16) for i in un]
        rbk = [jnp.where(incl2, g_[i][t2:, :], 0.0).astype(BF16) for i in un]
        xm = n
        pw = [_dot(n[i], n[i]) for i in un]
        for _ in range(n_levels - 2):
            res = [_dot(jnp.concatenate([pw[i], xm[i]], axis=0), pw[i]) for i in un]
            xm = [xm[i] + pw[i] + res[i][t2:] for i in un]
            pw = [res[i][:t2] for i in un]
        res = [_dot(xm[i], pw[i]) for i in un]
        xm = [(xm[i] + pw[i] + res[i]).astype(BF16) for i in un]
        s_old = [s_sc[s * n_pairs + p] for s, p in units]
        s_bf = [s_old[i].astype(BF16) for i in un]
        rhs = [_dot_nt(a2[i], s_bf[i]) for i in un]
        akv = [jnp.dot(ak[i], v2b[i], preferred_element_type=F32) for i in un]
        rhs = [rhs[i] + akv[i] for i in un]
        u = [rhs[i] + _dot(xm[i], rhs[i]) for i in un]
        uv = [jnp.concatenate([u[i].astype(BF16), v2b[i]], axis=0) for i in un]
        ys = [_dot_nt(r2[i], s_bf[i]) for i in un]
        yi = [jnp.dot(rbk[i], uv[i], preferred_element_type=F32) for i in un]
        sn = [_dot_tn(uv[i], bk[i] * pe[i]) for i in un]
        for i, (s, p) in enumerate(units):
            y2 = ys[i] + yi[i]
            y_sc[sls[s], cols[p]] = y2[:t] + y2[t:]
            s_sc[s * n_pairs + p] = s_old[i] * pe[i] + sn[i]
        return carry

    lax.fori_loop(0, tb // t, chunk, 0)

    y = y_sc[...]
    inv_n = 1.0 / HEAD_DIM
    mean = _seg_sum(y, seg_ones) * inv_n
    yc = y - mean
    var = _seg_sum(yc * yc, seg_ones) * inv_n
    yn = yc * lax.rsqrt(var + GN_EPS) * lnw_ref[...] + lnb_ref[...]
    bonus = _seg_sum(r_sc[...] * k_sc[...] * rk_ref[...], seg_ones) * v_sc[...]
    ya = (yn + bonus) * g_sc[...]
    for s in seqs:
        ya_ref[s] = ya[s * tb:(s + 1) * tb]

    pos = blk * tb + lax.broadcasted_iota(jnp.int32, (tb, 1), 0)
    gw = bw // len(POOL_WINDOWS)
    for s in seqs:
        pb = pb_ref[s]
        xbuf[s, POOL_PAD:POOL_PAD + tb, :] = pb
        sums = _pool_window_sums(xbuf, e2, e4, e8, s, tb)
        for gi, w in enumerate(POOL_WINDOWS):
            cols = slice(gi * gw, (gi + 1) * gw)
            cnt = jnp.minimum(w, pos + 1).astype(F32)
            dlt = sums[gi] / cnt - pb[:, cols]
            yb_ref[s, :, cols] = _dot(dlt, wpool_ref[gi]) * pscale_ref[:, cols]
        xbuf[s, POOL_PAD - 16:POOL_PAD, :] = xbuf[s, tb + POOL_PAD - 16:tb + POOL_PAD, :]

    @pl.when(blk == n_blocks - 1)
    def _():
        for s in seqs:
            for p in range(n_pairs):
                s_p = s_sc[s * n_pairs + p]
                for h in range(2):
                    o = h * HEAD_DIM
                    wkv_ref[s, 2 * p + h] = s_p[o:o + HEAD_DIM, o:o + HEAD_DIM]


def _prompt_mix_call(pa, pb, batch, seq, prm):
    na = pa.shape[1]
    bw = pb.shape[1]
    aw = prm["w0"].shape[1]
    n_heads = aw // HEAD_DIM
    tb, ns = min(PROMPT_BLOCK, seq), PROMPT_SEQS
    assert seq % tb == 0 and tb % WKV_CHUNK == 0 and aw % LANES == 0 and batch % ns == 0
    assert len(POOL_WINDOWS) == 4 and POOL_HIST < 16 <= POOL_PAD - 8
    n_blocks = seq // tb
    gw = bw // len(POOL_WINDOWS)
    row = lambda b, c: (b, c, 0)
    const2 = lambda b, c: (0, 0)
    const3 = lambda b, c: (0, 0, 0)
    full = lambda a: pl.BlockSpec(a.shape, const2 if a.ndim == 2 else const3)
    names = ("mu", "w0", "w2", "a0", "a2", "g2", "k_k", "k_a", "r_k", "ln_w", "ln_b", "w_pool", "pool_scale", "seg")
    params = [prm[n] for n in names]
    blk = lambda: pltpu.VMEM((ns * tb, aw), F32)
    ya, yb, wkv = pl.pallas_call(
        functools.partial(_prompt_mix_kernel, n_blocks=n_blocks),
        grid=(batch // ns, n_blocks),
        in_specs=[pl.BlockSpec((ns, tb, na), row), pl.BlockSpec((ns, tb, bw), row)] + [full(a) for a in params],
        out_specs=[pl.BlockSpec((ns, tb, aw), row), pl.BlockSpec((ns, tb, bw), row),
                   pl.BlockSpec((ns, n_heads, HEAD_DIM, HEAD_DIM), lambda b, c: (b, 0, 0, 0))],
        out_shape=[jax.ShapeDtypeStruct((batch, seq, aw), F32), jax.ShapeDtypeStruct((batch, seq, bw), F32),
                   jax.ShapeDtypeStruct((batch, n_heads, HEAD_DIM, HEAD_DIM), F32)],
        scratch_shapes=[pltpu.VMEM((ns, tb + SHIFT_PAD, na), F32), pltpu.VMEM((ns, tb + POOL_PAD, bw), F32),
                        pltpu.VMEM((ns, tb + POOL_PAD, bw), F32), pltpu.VMEM((ns, tb + POOL_PAD, bw - gw), F32),
                        pltpu.VMEM((ns, tb + POOL_PAD, bw - 2 * gw), F32),
                        pltpu.VMEM((ns * (aw // LANES), LANES, LANES), F32),
                        blk(), blk(), blk(), blk(), blk(), blk(), blk(), blk()],
        compiler_params=_cparams(2),
        name="mix_prompt",
    )(pa.reshape(batch, seq, na), pb.reshape(batch, seq, bw), *params)
    return ya.reshape(batch * seq, aw), yb.reshape(batch * seq, bw), wkv


def _sample_prep_kernel(pa_ref, sh0_ref, pb_ref, hist_ref, mu_ref, w0_ref, w2_ref, a0_ref, a2_ref, g2_ref,
                        kk_ref, ka_ref, wpool_ref, pscale_ref, seg_ref,
                        r_ref, w_ref, k_ref, v_ref, a_ref, b_ref, g_ref, yb_ref, *, batch, seq):
    aw = r_ref.shape[1]
    bw = yb_ref.shape[1]
    pa = pa_ref[...]
    prev = jnp.concatenate([sh0_ref[...], pa[:(seq - 1) * batch]], axis=0)
    xs = pa + (prev - pa) * mu_ref[...]
    r, k, v, logw, a, b, g = _rwkv_prep(xs, aw, seg_ref[...], w0_ref[...], w2_ref[...], a0_ref[...], a2_ref[...],
                                        g2_ref[...], kk_ref[...], ka_ref[...])
    outs = ((r_ref, r), (w_ref, jnp.exp(logw)), (k_ref, k), (v_ref, v), (a_ref, a), (b_ref, b), (g_ref, g))
    for ref, val in outs:
        for ti in range(seq):
            ref[ti] = val[ti * batch:(ti + 1) * batch, :].T

    gw = bw // len(POOL_WINDOWS)

    def slab(i, cols):
        if i < POOL_HIST:
            return hist_ref[i * batch:(i + 1) * batch, cols]
        return pb_ref[(i - POOL_HIST) * batch:(i - POOL_HIST + 1) * batch, cols]

    for ti in range(seq):
        rows = slice(ti * batch, (ti + 1) * batch)
        for gi, w in enumerate(POOL_WINDOWS):
            cols = slice(gi * gw, (gi + 1) * gw)
            s = _pool_group_sum(lambda j: slab(POOL_HIST + ti - j, cols), w)
            cnt = float(min(w, PAST_LEN + ti + 1))
            dlt = s / cnt - pb_ref[rows, cols]
            yb_ref[rows, cols] = _dot(dlt, wpool_ref[gi]) * pscale_ref[:, cols]


def _sample_prep_call(pa, sh0, pb, hist, batch, seq, prm):
    rows = pa.shape[0]
    aw = prm["w0"].shape[1]
    bw = pb.shape[1]
    names = ("mu", "w0", "w2", "a0", "a2", "g2", "k_k", "k_a", "w_pool", "pool_scale", "seg")
    args = [pa, sh0, pb, hist] + [prm[n] for n in names]
    full = lambda a: pl.BlockSpec(a.shape, (lambda i: (0, 0)) if a.ndim == 2 else (lambda i: (0, 0, 0)))
    o = jax.ShapeDtypeStruct((seq, aw, batch), F32)
    ob = jax.ShapeDtypeStruct((rows, bw), F32)
    return pl.pallas_call(
        functools.partial(_sample_prep_kernel, batch=batch, seq=seq),
        grid=(1,),
        in_specs=[full(a) for a in args],
        out_specs=[full(o)] * 7 + [full(ob)],
        out_shape=[o] * 7 + [ob],
        compiler_params=_cparams(1),
        name="prep_sample",
    )(*args)


SAMPLE_VROWS = 16


def _sample_wkv_kernel(r_ref, w_ref, k_ref, v_ref, a_ref, b_ref, g_ref, rk_ref, lnw_ref, lnb_ref, s_ref, all_ref,
                       y_ref, so_ref, y_sc):
    del all_ref
    seq, hd, batch = r_ref.shape
    vb = s_ref.shape[0]
    h, vs = pl.program_id(0), pl.program_id(1)
    hrow = pl.multiple_of((h % 2) * hd, hd)
    for q in range(vb):
        tile = s_ref[q]
        for ti in range(seq):
            sa = jnp.sum(tile * a_ref[ti], axis=0, keepdims=True)
            tile = tile * w_ref[ti] + sa * b_ref[ti] + v_ref[ti, pl.ds(vs * vb + q, 1), :] * k_ref[ti]
            y_sc[ti, pl.ds(hrow + vs * vb + q, 1), :] = jnp.sum(tile * r_ref[ti], axis=0, keepdims=True)
        so_ref[q] = tile

    @pl.when(vs == pl.num_programs(1) - 1)
    def _():
        for ti in range(seq):
            rows = pl.ds(hrow, hd)
            y = y_sc[ti, rows, :]
            yc = y - jnp.mean(y, axis=0, keepdims=True)
            var = jnp.mean(yc * yc, axis=0, keepdims=True)
            yn = yc * lax.rsqrt(var + GN_EPS) * lnw_ref[...] + lnb_ref[...]
            bonus = jnp.sum(r_ref[ti] * k_ref[ti] * rk_ref[...], axis=0, keepdims=True) * v_ref[ti]
            y_sc[ti, rows, :] = (yn + bonus) * g_ref[ti]

    @pl.when((vs == pl.num_programs(1) - 1) & (h % 2 == 1))
    def _():
        for ti in range(seq):
            y_ref[ti * batch:(ti + 1) * batch, :] = y_sc[ti].T


def _sample_wkv_call(vecs, state_t, new_states, layer, batch, seq, prm):
    n_heads, hd = state_t.shape[1:3]
    aw = n_heads * hd
    vb = SAMPLE_VROWS
    assert 2 * hd == LANES and hd % vb == 0
    vspec = pl.BlockSpec((seq, hd, batch), lambda h, s: (0, h, 0))
    cspec = pl.BlockSpec((hd, 1), lambda h, s: (h, 0))
    sspec = pl.BlockSpec((None, None, vb, hd, batch), lambda h, s: (layer, h, s, 0, 0))
    cols = [prm[n].reshape(aw, 1) for n in ("r_k", "ln_w", "ln_b")]
    args = [*vecs, *cols, state_t, new_states]
    return pl.pallas_call(
        _sample_wkv_kernel,
        grid=(n_heads, hd // vb),
        in_specs=[vspec] * 7 + [cspec] * 3 + [sspec, pl.BlockSpec(memory_space=pl.ANY)],
        out_specs=[pl.BlockSpec((seq * batch, LANES), lambda h, s: (0, h // 2)), sspec],
        out_shape=[jax.ShapeDtypeStruct((seq * batch, aw), F32), jax.ShapeDtypeStruct(state_t.shape, F32)],
        scratch_shapes=[pltpu.VMEM((seq, LANES, batch), F32)],
        input_output_aliases={len(args) - 1: 1},
        compiler_params=_cparams(2),
        name="wkv_sample",
    )(*args)


def kernel(x_prompt, x_sample, state_wkv, state_shift, state_pool, c_prompt, c_sample, norm_g, w_mod, b_mod, w_ffn_in, w_ffn_out, w_in, mu_shift, w0, w2, a0, a2, g2, k_k, k_a, r_k, ln_x_w, ln_x_b, w_pool, pool_scale, w_br_a, w_br_b, w_gate, b_gate, w_out, final_g):
    bp, lp, d = x_prompt.shape
    bs, ls, _ = x_sample.shape
    depth = w_mod.shape[0]
    aw = w0.shape[1]
    a_proj = mu_shift.shape[1]
    bw = pool_scale.shape[1]

    bf = lambda w: w.astype(BF16)
    w_in_a, w_in_b = bf(w_in[:, :, :a_proj]), bf(w_in[:, :, a_proj:])
    w_gate_b, w_br_a_b, w_br_b_b, w_out_b = bf(w_gate), bf(w_br_a), bf(w_br_b), bf(w_out)
    w2_b, a2_b, g2_b, w_pool_b = bf(w2), bf(a2), bf(g2), bf(w_pool)
    head_of = jnp.arange(MXU_TILE, dtype=jnp.int32) // HEAD_DIM
    seg_ones = (head_of[:, None] == head_of[None, :]).astype(BF16)

    mod = _mod_call(jnp.concatenate([c_sample, c_prompt], axis=0), w_mod, b_mod)
    grp = _Groups(bp, lp, bs, ls, min(ROW_TILE, lp), mod_row0=bs)

    xp = x_prompt.reshape(bp * lp, d)
    xs = jnp.transpose(x_sample, (1, 0, 2)).reshape(ls * bs, d)
    hist_s = jnp.transpose(state_pool, (0, 2, 1, 3)).reshape(depth, POOL_HIST * bs, bw)
    state_t = jnp.transpose(state_wkv, (0, 2, 3, 4, 1))
    wkv_s = jnp.zeros_like(state_t)

    wkv_p, shift_p, pool_p, shift_s, pool_s = [], [], [], [], []
    for l in range(depth):
        prm = dict(mu=mu_shift[l][None], w0=w0[l][None], w2=w2_b[l], a0=a0[l][None], a2=a2_b[l], g2=g2_b[l],
                   k_k=k_k[l][None], k_a=k_a[l][None], r_k=r_k[l].reshape(1, aw), ln_w=ln_x_w[l][None],
                   ln_b=ln_x_b[l][None], w_pool=w_pool_b[l], pool_scale=pool_scale[l][None], seg=seg_ones)
        xp, xs = _ffn_call(xp, xs, grp, mod, l, 0, norm_g[l, 0], w_ffn_in, w_ffn_out)
        pa_p, pb_p, pa_s, pb_s = _mixin_call(xp, xs, grp, mod, l, norm_g[l, 1], w_in_a[l], w_in_b[l])
        ya_p, yb_p, s_new = _prompt_mix_call(pa_p, pb_p, bp, lp, prm)
        *vecs, yb_s = _sample_prep_call(pa_s, state_shift[l, :, 0, :], pb_s, hist_s[l], bs, ls, prm)
        ya_s, wkv_s = _sample_wkv_call(vecs, state_t, wkv_s, l, bs, ls, prm)
        xp, xs = _mixout_call(xp, xs, grp, mod, l, norm_g[l, 1], ya_p, yb_p, ya_s, yb_s, w_gate_b, b_gate, w_br_a_b,
                              w_br_b_b, w_out_b)
        xp, xs = _ffn_call(xp, xs, grp, mod, l, 2, norm_g[l, 2], w_ffn_in, w_ffn_out,
                           final_g if l == depth - 1 else None)
        wkv_p.append(s_new)
        shift_p.append(pa_p.reshape(bp, lp, a_proj)[:, -1:])
        pool_p.append(pb_p.reshape(bp, lp, bw)[:, -POOL_HIST:])
        shift_s.append(pa_s.reshape(ls, bs, a_proj)[-1][:, None, :])
        pb_bm = jnp.transpose(pb_s.reshape(ls, bs, bw), (1, 0, 2))
        pool_s.append(jnp.concatenate([state_pool[l], pb_bm], axis=1)[:, -POOL_HIST:])

    y_prompt = xp.reshape(bp, lp, d)
    y_sample = jnp.transpose(xs.reshape(ls, bs, d), (1, 0, 2))
    return (y_prompt, y_sample, jnp.stack(wkv_p), jnp.stack(shift_p), jnp.stack(pool_p),
            jnp.transpose(wkv_s, (0, 4, 1, 2, 3)), jnp.stack(shift_s), jnp.stack(pool_s))
```

```python
import functools
import math

import jax
import jax.numpy as jnp
from jax import lax
from jax.experimental import pallas as pl
from jax.experimental.pallas import tpu as pltpu

F32 = jnp.float32
BF16 = jnp.bfloat16

HEAD_DIM = 64
POOL_WINDOWS = (2, 4, 8, 16)
POOL_HIST = max(POOL_WINDOWS) - 1
N_SUB = 3
NORM_EPS = 1e-6
GN_EPS = 64e-5
PAST_LEN = 16384
DECAY_SCALE = math.exp(-0.5)

LANES = 128
MXU_TILE = 256
ROW_TILE = 1024
WKV_CHUNK = 64
VMEM_LIMIT_BYTES = 56 * 1024 * 1024


_RESIDENT = dict(pipeline_mode=pl.Buffered(1))


def _cparams(n_axes, **kw):
    return pltpu.CompilerParams(dimension_semantics=("arbitrary",) * n_axes,
                                vmem_limit_bytes=VMEM_LIMIT_BYTES, **kw)


def _sigmoid(x):
    return 0.5 * jnp.tanh(0.5 * x) + 0.5


def _dot(a, b):
    return jnp.dot(a.astype(BF16), b.astype(BF16), preferred_element_type=F32)


def _dot_nt(a, b):
    return lax.dot_general(a.astype(BF16), b.astype(BF16), (((1,), (1,)), ((), ())),
                           preferred_element_type=F32)


def _dot_tn(a, b):
    return lax.dot_general(a.astype(BF16), b.astype(BF16), (((0,), (0,)), ((), ())),
                           preferred_element_type=F32)


def _seg_sum(x, seg_ones):
    n, width = x.shape
    tw = seg_ones.shape[0]
    outs = []
    for c0 in range(0, width, tw):
        xc = x[:, c0:c0 + tw]
        hi = xc.astype(BF16)
        lo = (xc - hi.astype(F32)).astype(BF16)
        both = jnp.dot(jnp.concatenate([hi, lo], axis=0), seg_ones, preferred_element_type=F32)
        outs.append(both[:n] + both[n:])
    return jnp.concatenate(outs, axis=1)


def _rms(x):
    return x * lax.rsqrt(jnp.mean(x * x, axis=-1, keepdims=True) + NORM_EPS)


SUBLANES = 8


def _rows(ref, tm, seq_row):
    if seq_row is not None:
        return ref[pl.ds(seq_row, 1), :]
    m = ref[...]
    return m if m.shape[0] == tm else jnp.concatenate([m] * (tm // m.shape[0]), axis=0)


def _modnorm(x, g, shift, scale):
    return _rms(x) * g * (1.0 + scale) + shift


def _mod_kernel(c_ref, w_ref, b_ref, o_ref):
    c = c_ref[...]
    o_ref[...] = _dot(c * _sigmoid(c), w_ref[...]) + b_ref[...]


def _mod_call(c_all, w_mod, b_mod):
    depth, d, n9 = w_mod.shape
    rows = c_all.shape[0]
    n_sub9 = n9 // d
    tn = d
    return pl.pallas_call(
        _mod_kernel,
        grid=(depth, n_sub9),
        in_specs=[pl.BlockSpec((rows, d), lambda l, j: (0, 0)),
                  pl.BlockSpec((None, d, tn), lambda l, j: (l, 0, j)),
                  pl.BlockSpec((None, 1, tn), lambda l, j: (l, 0, j))],
        out_specs=pl.BlockSpec((None, None, rows, tn), lambda l, j: (l, j, 0, 0)),
        out_shape=jax.ShapeDtypeStruct((depth, n_sub9, rows, d), F32),
        compiler_params=_cparams(2),
        name="mod",
    )(c_all, w_mod, b_mod.reshape(depth, 1, n9))


class _Groups:
    def __init__(self, bp, lp, bs, ls, tm, mod_row0):
        assert lp % tm == 0
        self.tm, self.ts, self.bs = tm, bs * ls, bs
        self.n_prompt = bp * lp // tm
        self.seq_rows = (mod_row0, lp // tm)

    def _ptile(self, i):
        return jnp.minimum(i, self.n_prompt - 1)

    def prompt(self, width):
        return pl.BlockSpec((self.tm, width), lambda i: (self._ptile(i), 0))

    def sample(self, width, **kw):
        return pl.BlockSpec((self.ts, width), lambda i: (0, 0), **kw)

    def prompt_mod(self, d, layer, idx):
        row0, per_seq = self.seq_rows
        return pl.BlockSpec((None, None, SUBLANES, d),
                            lambda i: (layer, idx, (row0 + self._ptile(i) // per_seq) // SUBLANES, 0))

    def sample_mod(self, d, layer, idx):
        return pl.BlockSpec((None, None, self.bs, d), lambda i: (layer, idx, 0, 0), **_RESIDENT)

    def mods(self, d, layer, idxs):
        return ([self.prompt_mod(d, layer, k) for k in idxs] + [self.sample_mod(d, layer, k) for k in idxs])

    def run(self, body, prompt_refs, sample_refs):
        i = pl.program_id(0)
        row0, per_seq = self.seq_rows
        seq_row = (row0 + i // per_seq) % SUBLANES
        pl.when(i < self.n_prompt)(lambda: body(prompt_refs, seq_row))
        pl.when(i == self.n_prompt)(lambda: body(sample_refs, None))


FFN_CHUNK = 2 * MXU_TILE


def _ffn_kernel(xp_ref, xs_ref, psh, psc, pgt, ssh, ssc, sgt, g_ref, win_ref, wo_ref, *rest, final, grp):
    if final:
        fg_ref, yp_ref, ys_ref, acc_sc = rest
    else:
        yp_ref, ys_ref, acc_sc = rest
    d_ff = wo_ref.shape[0]

    def body(refs, seq_row):
        x_ref, o_ref, sh_ref, sc_ref, gt_ref = refs
        tm = x_ref.shape[0]
        x = x_ref[...]
        u = _modnorm(x, g_ref[...], _rows(sh_ref, tm, seq_row), _rows(sc_ref, tm, seq_row)).astype(BF16)
        for c0 in range(0, d_ff, FFN_CHUNK):
            c1 = min(c0 + FFN_CHUNK, d_ff)
            hg = jnp.dot(u, win_ref[:, c0:c1], preferred_element_type=F32)
            hu = jnp.dot(u, win_ref[:, d_ff + c0:d_ff + c1], preferred_element_type=F32)
            act = (hg * _sigmoid(hg) * hu).astype(BF16)
            part = jnp.dot(act, wo_ref[c0:c1, :], preferred_element_type=F32)
            if c0 == 0:
                acc_sc[0:tm, :] = part
            else:
                acc_sc[0:tm, :] += part
        y = x + 0.5 * _rows(gt_ref, tm, seq_row) * acc_sc[0:tm, :]
        if final:
            y = _rms(y) * fg_ref[...]
        o_ref[...] = y

    grp.run(body, (xp_ref, yp_ref, psh, psc, pgt), (xs_ref, ys_ref, ssh, ssc, sgt))


def _ffn_call(xp, xs, grp, mod, layer, sub, norm_g, w_in, w_out, final_g=None):
    d = xp.shape[1]
    d_ff = w_out.shape[2]
    assert d_ff % LANES == 0 and FFN_CHUNK % LANES == 0 and grp.ts <= grp.tm
    final = final_g is not None
    const = lambda i: (0, 0)
    half = lambda i: (layer, sub // 2, 0, 0)
    in_specs = ([grp.prompt(d), grp.sample(d, **_RESIDENT)] + grp.mods(d, layer, (3 * sub, 3 * sub + 1, 3 * sub + 2))
                + [pl.BlockSpec((1, d), const),
                   pl.BlockSpec((None, None, d, 2 * d_ff), half, **_RESIDENT),
                   pl.BlockSpec((None, None, d_ff, d), half, **_RESIDENT)])
    args = [xp, xs] + [mod] * 6 + [norm_g.reshape(1, d), w_in, w_out]
    if final:
        in_specs.append(pl.BlockSpec((1, d), const))
        args.append(final_g.reshape(1, d))
    return pl.pallas_call(
        functools.partial(_ffn_kernel, final=final, grp=grp),
        grid=(grp.n_prompt + 1,),
        in_specs=in_specs,
        out_specs=[grp.prompt(d), grp.sample(d)],
        out_shape=[jax.ShapeDtypeStruct(xp.shape, F32), jax.ShapeDtypeStruct(xs.shape, F32)],
        scratch_shapes=[pltpu.VMEM((grp.tm, d), F32)],
        compiler_params=_cparams(1),
        name="ffn",
    )(*args)


def _mixin_kernel(xp_ref, xs_ref, psh, psc, ssh, ssc, g_ref, wa_ref, wb_ref, pap_ref, pbp_ref, pas_ref, pbs_ref, *, grp):
    def body(refs, seq_row):
        x_ref, pa_ref, pb_ref, sh_ref, sc_ref = refs
        tm = x_ref.shape[0]
        u = _modnorm(x_ref[...], g_ref[...], _rows(sh_ref, tm, seq_row), _rows(sc_ref, tm, seq_row)).astype(BF16)
        pa_ref[...] = jnp.dot(u, wa_ref[...], preferred_element_type=F32)
        pb_ref[...] = jnp.dot(u, wb_ref[...], preferred_element_type=F32)

    grp.run(body, (xp_ref, pap_ref, pbp_ref, psh, psc), (xs_ref, pas_ref, pbs_ref, ssh, ssc))


def _mixin_call(xp, xs, grp, mod, layer, norm_g, w_a, w_b):
    d = xp.shape[1]
    na, nb = w_a.shape[1], w_b.shape[1]
    const = lambda i: (0, 0)
    shape = lambda x, n: jax.ShapeDtypeStruct((x.shape[0], n), F32)
    return pl.pallas_call(
        functools.partial(_mixin_kernel, grp=grp),
        grid=(grp.n_prompt + 1,),
        in_specs=([grp.prompt(d), grp.sample(d, **_RESIDENT)] + grp.mods(d, layer, (3, 4))
                  + [pl.BlockSpec((1, d), const), pl.BlockSpec((d, na), const, **_RESIDENT),
                     pl.BlockSpec((d, nb), const, **_RESIDENT)]),
        out_specs=[grp.prompt(na), grp.prompt(nb), grp.sample(na), grp.sample(nb)],
        out_shape=[shape(xp, na), shape(xp, nb), shape(xs, na), shape(xs, nb)],
        compiler_params=_cparams(1),
        name="mixin",
    )(xp, xs, mod, mod, mod, mod, norm_g.reshape(1, d), w_a, w_b)


def _mixout_kernel(xp_ref, xs_ref, yap_ref, ybp_ref, yas_ref, ybs_ref, psh, psc, pgt, ssh, ssc, sgt, g_ref,
                   wgate_ref, bgate_ref, wbra_ref, wbrb_ref, wout_ref, op_ref, os_ref, *, grp):
    def body(refs, seq_row):
        x_ref, ya_ref, yb_ref, o_ref, sh_ref, sc_ref, gt_ref = refs
        tm, d = x_ref.shape
        x = x_ref[...]
        u = _modnorm(x, g_ref[...], _rows(sh_ref, tm, seq_row), _rows(sc_ref, tm, seq_row)).astype(BF16)
        gates = _sigmoid(jnp.dot(u, wgate_ref[...], preferred_element_type=F32) + bgate_ref[...])
        ma = _dot(ya_ref[...], wbra_ref[...])
        mb = _dot(yb_ref[...], wbrb_ref[...])
        merged = gates[:, :d] * ma + gates[:, d:] * mb
        o_ref[...] = x + _rows(gt_ref, tm, seq_row) * _dot(merged, wout_ref[...])

    grp.run(body, (xp_ref, yap_ref, ybp_ref, op_ref, psh, psc, pgt), (xs_ref, yas_ref, ybs_ref, os_ref, ssh, ssc, sgt))


def _mixout_call(xp, xs, grp, mod, layer, norm_g, ya_p, yb_p, ya_s, yb_s, w_gate, b_gate, w_br_a, w_br_b, w_out):
    d = xp.shape[1]
    wa, wb = ya_p.shape[1], yb_p.shape[1]
    const = lambda i: (0, 0)
    lyr = lambda i: (layer, 0, 0)
    return pl.pallas_call(
        functools.partial(_mixout_kernel, grp=grp),
        grid=(grp.n_prompt + 1,),
        in_specs=([grp.prompt(d), grp.sample(d, **_RESIDENT), grp.prompt(wa), grp.prompt(wb),
                   grp.sample(wa, **_RESIDENT), grp.sample(wb, **_RESIDENT)] + grp.mods(d, layer, (3, 4, 5))
                  + [pl.BlockSpec((1, d), const),
                     pl.BlockSpec((None, d, 2 * d), lyr, **_RESIDENT), pl.BlockSpec((None, 1, 2 * d), lyr),
                     pl.BlockSpec((None, wa, d), lyr, **_RESIDENT), pl.BlockSpec((None, wb, d), lyr, **_RESIDENT),
                     pl.BlockSpec((None, d, d), lyr, **_RESIDENT)]),
        out_specs=[grp.prompt(d), grp.sample(d)],
        out_shape=[jax.ShapeDtypeStruct(xp.shape, F32), jax.ShapeDtypeStruct(xs.shape, F32)],
        compiler_params=_cparams(1),
        name="mixout",
    )(xp, xs, ya_p, yb_p, ya_s, yb_s, *([mod] * 6), norm_g.reshape(1, d), w_gate, b_gate.reshape(-1, 1, 2 * d),
      w_br_a, w_br_b, w_out)


def _rwkv_prep(xs, aw, seg_ones, w0, w2, a0, a2, g2, k_k, k_a):
    n_dec, n_aaa = w2.shape[0], a2.shape[0]
    r = xs[:, 0:aw]
    k = xs[:, aw:2 * aw]
    v = xs[:, 2 * aw:3 * aw]
    o = 3 * aw
    wl = xs[:, o:o + n_dec]
    al = xs[:, o + n_dec:o + n_dec + n_aaa]
    gl = xs[:, o + n_dec + n_aaa:]
    logw = -DECAY_SCALE * _sigmoid(w0 + _dot(jnp.tanh(wl), w2))
    asig = _sigmoid(a0 + _dot(al, a2))
    g = _dot(_sigmoid(gl), g2)
    kk = k * k_k
    kk = kk * lax.rsqrt(jnp.maximum(_seg_sum(kk * kk, seg_ones), 1e-24))
    k = k * (1.0 + (asig - 1.0) * k_a)
    return r, k, v, logw, -kk, kk * asig, g


def _pool_group_sum(read_rows, w):
    s = read_rows(0)
    for j in range(1, w):
        s = s + read_rows(j)
    return s


PROMPT_BLOCK = 128
PROMPT_SEQS = 4
SHIFT_PAD = 8
POOL_PAD = 24


def _pool_window_sums(xbuf, e2, e4, e8, s, tb):
    gw = xbuf.shape[2] // len(POOL_WINDOWS)
    lo, hi = POOL_PAD - 16, POOL_PAD + tb
    e2[s, lo:hi, :] = xbuf[s, lo:hi, :] + xbuf[s, lo - 1:hi - 1, :]
    e4[s, lo:hi, :] = e2[s, lo:hi, gw:] + e2[s, lo - 2:hi - 2, gw:]
    e8[s, lo:hi, :] = e4[s, lo:hi, gw:] + e4[s, lo - 4:hi - 4, gw:]
    blk = slice(POOL_PAD, POOL_PAD + tb)
    back8 = slice(POOL_PAD - 8, POOL_PAD - 8 + tb)
    return (e2[s, blk, 0:gw], e4[s, blk, 0:gw], e8[s, blk, 0:gw], e8[s, blk, gw:] + e8[s, back8, gw:])


def _prompt_mix_kernel(pa_ref, pb_ref, mu_ref, w0_ref, w2_ref, a0_ref, a2_ref, g2_ref, kk_ref, ka_ref, rk_ref,
                       lnw_ref, lnb_ref, wpool_ref, pscale_ref, seg_ref,
                       ya_ref, yb_ref, wkv_ref,
                       pbuf, xbuf, e2, e4, e8, s_sc, r_sc, k_sc, v_sc, a_sc, b_sc, lw_sc, g_sc, y_sc, *, n_blocks):
    ns, tb, aw = ya_ref.shape
    bw = yb_ref.shape[2]
    t = WKV_CHUNK
    n_pairs = aw // LANES
    blk = pl.program_id(1)
    seqs = range(ns)

    @pl.when(blk == 0)
    def _():
        pbuf[:, 0:SHIFT_PAD, :] = jnp.zeros((ns, SHIFT_PAD, pbuf.shape[2]), F32)
        xbuf[:, 0:POOL_PAD, :] = jnp.zeros((ns, POOL_PAD, bw), F32)
        e2[:, 0:POOL_PAD, :] = jnp.zeros((ns, POOL_PAD, e2.shape[2]), F32)
        e4[:, 0:POOL_PAD, :] = jnp.zeros((ns, POOL_PAD, e4.shape[2]), F32)
        s_sc[...] = jnp.zeros_like(s_sc)

    xs = []
    for s in seqs:
        pa = pa_ref[s]
        pbuf[s, SHIFT_PAD:SHIFT_PAD + tb, :] = pa
        prev = pbuf[s, SHIFT_PAD - 1:SHIFT_PAD - 1 + tb, :]
        xs.append(pa + (prev - pa) * mu_ref[...])
        pbuf[s, 0:SHIFT_PAD, :] = pbuf[s, tb:tb + SHIFT_PAD, :]
    xs = jnp.concatenate(xs, axis=0)

    seg_ones = seg_ref[...]
    r, k, v, logw, a, b, g = _rwkv_prep(xs, aw, seg_ones, w0_ref[...], w2_ref[...], a0_ref[...], a2_ref[...],
                                        g2_ref[...], kk_ref[...], ka_ref[...])
    r_sc[...] = r
    k_sc[...] = k
    v_sc[...] = v
    a_sc[...] = a
    b_sc[...] = b
    lw_sc[...] = logw
    g_sc[...] = g

    t2 = 2 * t
    ri = lax.broadcasted_iota(jnp.int32, (t, t), 0)
    ci = lax.broadcasted_iota(jnp.int32, (t, t), 1)
    tri_incl = (ri >= ci).astype(BF16)
    rs = lax.broadcasted_iota(jnp.int32, (t2, t2), 0)
    cs = lax.broadcasted_iota(jnp.int32, (t2, t2), 1)
    strict = cs < rs
    rs2 = lax.broadcasted_iota(jnp.int32, (t2, 2 * t2), 0)
    cs2 = lax.broadcasted_iota(jnp.int32, (t2, 2 * t2), 1)
    incl2 = jnp.where(cs2 >= t2, cs2 - t2, cs2) <= rs2
    lane = lax.broadcasted_iota(jnp.int32, (1, LANES), 1)
    head_mask = (lane < HEAD_DIM, lane >= HEAD_DIM)
    units = [(s, p) for s in seqs for p in range(n_pairs)]
    n_levels = int(math.log2(t))

    def stack(x):
        return jnp.concatenate([jnp.where(head_mask[0], x, 0.0), jnp.where(head_mask[1], x, 0.0)], axis=0)

    def chunk(ci_, carry):
        at, rt, bt, kt, vv, p_end, sls = [], [], [], [], [], [], []
        for s in seqs:
            sl = pl.ds(pl.multiple_of(s * tb + ci_ * t, t), t)
            lw = lw_sc[sl, :]
            hi = lw.astype(BF16)
            lo = (lw - hi.astype(F32)).astype(BF16)
            c = (jnp.dot(tri_incl, hi, preferred_element_type=F32)
                 + jnp.dot(tri_incl, lo, preferred_element_type=F32))
            e_in = jnp.exp(c)
            e_neg = jnp.exp(-c)
            p_end.append(e_in[t - 1:t, :])
            at.append(a_sc[sl, :] * jnp.exp(c - lw))
            rt.append(r_sc[sl, :] * e_in)
            bt.append(b_sc[sl, :] * e_neg)
            kt.append(k_sc[sl, :] * e_neg)
            vv.append(v_sc[sl, :])
            sls.append(sl)
        cols = [slice(p * LANES, (p + 1) * LANES) for p in range(n_pairs)]
        un = range(len(units))
        pe = [p_end[s][:, cols[p]] for s, p in units]
        a2 = [stack(at[s][:, cols[p]]).astype(BF16) for s, p in units]
        r2 = [stack(rt[s][:, cols[p]]).astype(BF16) for s, p in units]
        v2b = [stack(vv[s][:, cols[p]]).astype(BF16) for s, p in units]
        bk = [jnp.concatenate([stack(bt[s][:, cols[p]]), stack(kt[s][:, cols[p]])], axis=0) for s, p in units]
        g_ = [_dot_nt(jnp.concatenate([a2[i], r2[i]], axis=0), bk[i]) for i in un]
        n = [jnp.where(strict, g_[i][:t2, :t2], 0.0) for i in un]
        ak = [jnp.where(strict, g_[i][:t2, t2:], 0.0).astype(BF16) for i in un]
        rbk = [jnp.where(incl2, g_[i][t2:, :], 0.0).astype(BF16) for i in un]
        xm = n
        pw = [_dot(n[i], n[i]) for i in un]
        for _ in range(n_levels - 2):
            res = [_dot(jnp.concatenate([pw[i], xm[i]], axis=0), pw[i]) for i in un]
            xm = [xm[i] + pw[i] + res[i][t2:] for i in un]
            pw = [res[i][:t2] for i in un]
        res = [_dot(xm[i], pw[i]) for i in un]
        xm = [(xm[i] + pw[i] + res[i]).astype(BF16) for i in un]
        s_old = [s_sc[s * n_pairs + p] for s, p in units]
        s_bf = [s_old[i].astype(BF16) for i in un]
        rhs = [_dot_nt(a2[i], s_bf[i]) for i in un]
        akv = [jnp.dot(ak[i], v2b[i], preferred_element_type=F32) for i in un]
        rhs = [rhs[i] + akv[i] for i in un]
        u = [rhs[i] + _dot(xm[i], rhs[i]) for i in un]
        uv = [jnp.concatenate([u[i].astype(BF16), v2b[i]], axis=0) for i in un]
        ys = [_dot_nt(r2[i], s_bf[i]) for i in un]
        yi = [jnp.dot(rbk[i], uv[i], preferred_element_type=F32) for i in un]
        sn = [_dot_tn(uv[i], bk[i] * pe[i]) for i in un]
        for i, (s, p) in enumerate(units):
            y2 = ys[i] + yi[i]
            y_sc[sls[s], cols[p]] = y2[:t] + y2[t:]
            s_sc[s * n_pairs + p] = s_old[i] * pe[i] + sn[i]
        return carry

    lax.fori_loop(0, tb // t, chunk, 0)

    y = y_sc[...]
    inv_n = 1.0 / HEAD_DIM
    mean = _seg_sum(y, seg_ones) * inv_n
    yc = y - mean
    var = _seg_sum(yc * yc, seg_ones) * inv_n
    yn = yc * lax.rsqrt(var + GN_EPS) * lnw_ref[...] + lnb_ref[...]
    bonus = _seg_sum(r_sc[...] * k_sc[...] * rk_ref[...], seg_ones) * v_sc[...]
    ya = (yn + bonus) * g_sc[...]
    for s in seqs:
        ya_ref[s] = ya[s * tb:(s + 1) * tb]

    pos = blk * tb + lax.broadcasted_iota(jnp.int32, (tb, 1), 0)
    gw = bw // len(POOL_WINDOWS)
    for s in seqs:
        pb = pb_ref[s]
        xbuf[s, POOL_PAD:POOL_PAD + tb, :] = pb
        sums = _pool_window_sums(xbuf, e2, e4, e8, s, tb)
        for gi, w in enumerate(POOL_WINDOWS):
            cols = slice(gi * gw, (gi + 1) * gw)
            cnt = jnp.minimum(w, pos + 1).astype(F32)
            dlt = sums[gi] / cnt - pb[:, cols]
            yb_ref[s, :, cols] = _dot(dlt, wpool_ref[gi]) * pscale_ref[:, cols]
        xbuf[s, POOL_PAD - 16:POOL_PAD, :] = xbuf[s, tb + POOL_PAD - 16:tb + POOL_PAD, :]

    @pl.when(blk == n_blocks - 1)
    def _():
        for s in seqs:
            for p in range(n_pairs):
                s_p = s_sc[s * n_pairs + p]
                for h in range(2):
                    o = h * HEAD_DIM
                    wkv_ref[s, 2 * p + h] = s_p[o:o + HEAD_DIM, o:o + HEAD_DIM]


def _prompt_mix_call(pa, pb, batch, seq, prm):
    na = pa.shape[1]
    bw = pb.shape[1]
    aw = prm["w0"].shape[1]
    n_heads = aw // HEAD_DIM
    tb, ns = min(PROMPT_BLOCK, seq), PROMPT_SEQS
    assert seq % tb == 0 and tb % WKV_CHUNK == 0 and aw % LANES == 0 and batch % ns == 0
    assert len(POOL_WINDOWS) == 4 and POOL_HIST < 16 <= POOL_PAD - 8
    n_blocks = seq // tb
    gw = bw // len(POOL_WINDOWS)
    row = lambda b, c: (b, c, 0)
    const2 = lambda b, c: (0, 0)
    const3 = lambda b, c: (0, 0, 0)
    full = lambda a: pl.BlockSpec(a.shape, const2 if a.ndim == 2 else const3)
    names = ("mu", "w0", "w2", "a0", "a2", "g2", "k_k", "k_a", "r_k", "ln_w", "ln_b", "w_pool", "pool_scale", "seg")
    params = [prm[n] for n in names]
    blk = lambda: pltpu.VMEM((ns * tb, aw), F32)
    ya, yb, wkv = pl.pallas_call(
        functools.partial(_prompt_mix_kernel, n_blocks=n_blocks),
        grid=(batch // ns, n_blocks),
        in_specs=[pl.BlockSpec((ns, tb, na), row), pl.BlockSpec((ns, tb, bw), row)] + [full(a) for a in params],
        out_specs=[pl.BlockSpec((ns, tb, aw), row), pl.BlockSpec((ns, tb, bw), row),
                   pl.BlockSpec((ns, n_heads, HEAD_DIM, HEAD_DIM), lambda b, c: (b, 0, 0, 0))],
        out_shape=[jax.ShapeDtypeStruct((batch, seq, aw), F32), jax.ShapeDtypeStruct((batch, seq, bw), F32),
                   jax.ShapeDtypeStruct((batch, n_heads, HEAD_DIM, HEAD_DIM), F32)],
        scratch_shapes=[pltpu.VMEM((ns, tb + SHIFT_PAD, na), F32), pltpu.VMEM((ns, tb + POOL_PAD, bw), F32),
                        pltpu.VMEM((ns, tb + POOL_PAD, bw), F32), pltpu.VMEM((ns, tb + POOL_PAD, bw - gw), F32),
                        pltpu.VMEM((ns, tb + POOL_PAD, bw - 2 * gw), F32),
                        pltpu.VMEM((ns * (aw // LANES), LANES, LANES), F32),
                        blk(), blk(), blk(), blk(), blk(), blk(), blk(), blk()],
        compiler_params=_cparams(2),
        name="mix_prompt",
    )(pa.reshape(batch, seq, na), pb.reshape(batch, seq, bw), *params)
    return ya.reshape(batch * seq, aw), yb.reshape(batch * seq, bw), wkv


def _sample_prep_kernel(pa_ref, sh0_ref, pb_ref, hist_ref, mu_ref, w0_ref, w2_ref, a0_ref, a2_ref, g2_ref,
                        kk_ref, ka_ref, wpool_ref, pscale_ref, seg_ref,
                        r_ref, w_ref, k_ref, v_ref, a_ref, b_ref, g_ref, yb_ref, *, batch, seq):
    aw = r_ref.shape[1]
    bw = yb_ref.shape[1]
    pa = pa_ref[...]
    prev = jnp.concatenate([sh0_ref[...], pa[:(seq - 1) * batch]], axis=0)
    xs = pa + (prev - pa) * mu_ref[...]
    r, k, v, logw, a, b, g = _rwkv_prep(xs, aw, seg_ref[...], w0_ref[...], w2_ref[...], a0_ref[...], a2_ref[...],
                                        g2_ref[...], kk_ref[...], ka_ref[...])
    outs = ((r_ref, r), (w_ref, jnp.exp(logw)), (k_ref, k), (v_ref, v), (a_ref, a), (b_ref, b), (g_ref, g))
    for ref, val in outs:
        for ti in range(seq):
            ref[ti] = val[ti * batch:(ti + 1) * batch, :].T

    gw = bw // len(POOL_WINDOWS)

    def slab(i, cols):
        if i < POOL_HIST:
            return hist_ref[i * batch:(i + 1) * batch, cols]
        return pb_ref[(i - POOL_HIST) * batch:(i - POOL_HIST + 1) * batch, cols]

    for ti in range(seq):
        rows = slice(ti * batch, (ti + 1) * batch)
        for gi, w in enumerate(POOL_WINDOWS):
            cols = slice(gi * gw, (gi + 1) * gw)
            s = _pool_group_sum(lambda j: slab(POOL_HIST + ti - j, cols), w)
            cnt = float(min(w, PAST_LEN + ti + 1))
            dlt = s / cnt - pb_ref[rows, cols]
            yb_ref[rows, cols] = _dot(dlt, wpool_ref[gi]) * pscale_ref[:, cols]


def _sample_prep_call(pa, sh0, pb, hist, batch, seq, prm):
    rows = pa.shape[0]
    aw = prm["w0"].shape[1]
    bw = pb.shape[1]
    names = ("mu", "w0", "w2", "a0", "a2", "g2", "k_k", "k_a", "w_pool", "pool_scale", "seg")
    args = [pa, sh0, pb, hist] + [prm[n] for n in names]
    full = lambda a: pl.BlockSpec(a.shape, (lambda i: (0, 0)) if a.ndim == 2 else (lambda i: (0, 0, 0)))
    o = jax.ShapeDtypeStruct((seq, aw, batch), F32)
    ob = jax.ShapeDtypeStruct((rows, bw), F32)
    return pl.pallas_call(
        functools.partial(_sample_prep_kernel, batch=batch, seq=seq),
        grid=(1,),
        in_specs=[full(a) for a in args],
        out_specs=[full(o)] * 7 + [full(ob)],
        out_shape=[o] * 7 + [ob],
        compiler_params=_cparams(1),
        name="prep_sample",
    )(*args)


SAMPLE_VROWS = 32


def _sample_wkv_kernel(r_ref, w_ref, k_ref, v_ref, a_ref, b_ref, g_ref, rk_ref, lnw_ref, lnb_ref, s_ref, all_ref,
                       y_ref, so_ref, y_sc):
    del all_ref
    seq, hd, batch = r_ref.shape
    vb = s_ref.shape[0]
    h, vs = pl.program_id(0), pl.program_id(1)
    hrow = pl.multiple_of((h % 2) * hd, hd)
    for q in range(vb):
        tile = s_ref[q]
        for ti in range(seq):
            sa = jnp.sum(tile * a_ref[ti], axis=0, keepdims=True)
            tile = tile * w_ref[ti] + sa * b_ref[ti] + v_ref[ti, pl.ds(vs * vb + q, 1), :] * k_ref[ti]
            y_sc[ti, pl.ds(hrow + vs * vb + q, 1), :] = jnp.sum(tile * r_ref[ti], axis=0, keepdims=True)
        so_ref[q] = tile

    @pl.when(vs == pl.num_programs(1) - 1)
    def _():
        for ti in range(seq):
            rows = pl.ds(hrow, hd)
            y = y_sc[ti, rows, :]
            yc = y - jnp.mean(y, axis=0, keepdims=True)
            var = jnp.mean(yc * yc, axis=0, keepdims=True)
            yn = yc * lax.rsqrt(var + GN_EPS) * lnw_ref[...] + lnb_ref[...]
            bonus = jnp.sum(r_ref[ti] * k_ref[ti] * rk_ref[...], axis=0, keepdims=True) * v_ref[ti]
            y_sc[ti, rows, :] = (yn + bonus) * g_ref[ti]

    @pl.when((vs == pl.num_programs(1) - 1) & (h % 2 == 1))
    def _():
        for ti in range(seq):
            y_ref[ti * batch:(ti + 1) * batch, :] = y_sc[ti].T


def _sample_wkv_call(vecs, state_t, new_states, layer, batch, seq, prm):
    n_heads, hd = state_t.shape[1:3]
    aw = n_heads * hd
    vb = SAMPLE_VROWS
    assert 2 * hd == LANES and hd % vb == 0
    vspec = pl.BlockSpec((seq, hd, batch), lambda h, s: (0, h, 0))
    cspec = pl.BlockSpec((hd, 1), lambda h, s: (h, 0))
    sspec = pl.BlockSpec((None, None, vb, hd, batch), lambda h, s: (layer, h, s, 0, 0))
    cols = [prm[n].reshape(aw, 1) for n in ("r_k", "ln_w", "ln_b")]
    args = [*vecs, *cols, state_t, new_states]
    return pl.pallas_call(
        _sample_wkv_kernel,
        grid=(n_heads, hd // vb),
        in_specs=[vspec] * 7 + [cspec] * 3 + [sspec, pl.BlockSpec(memory_space=pl.ANY)],
        out_specs=[pl.BlockSpec((seq * batch, LANES), lambda h, s: (0, h // 2)), sspec],
        out_shape=[jax.ShapeDtypeStruct((seq * batch, aw), F32), jax.ShapeDtypeStruct(state_t.shape, F32)],
        scratch_shapes=[pltpu.VMEM((seq, LANES, batch), F32)],
        input_output_aliases={len(args) - 1: 1},
        compiler_params=_cparams(2),
        name="wkv_sample",
    )(*args)


def kernel(x_prompt, x_sample, state_wkv, state_shift, state_pool, c_prompt, c_sample, norm_g, w_mod, b_mod, w_ffn_in, w_ffn_out, w_in, mu_shift, w0, w2, a0, a2, g2, k_k, k_a, r_k, ln_x_w, ln_x_b, w_pool, pool_scale, w_br_a, w_br_b, w_gate, b_gate, w_out, final_g):
    bp, lp, d = x_prompt.shape
    bs, ls, _ = x_sample.shape
    depth = w_mod.shape[0]
    aw = w0.shape[1]
    a_proj = mu_shift.shape[1]
    bw = pool_scale.shape[1]

    bf = lambda w: w.astype(BF16)
    w_ffn_in_b, w_ffn_out_b = bf(w_ffn_in), bf(w_ffn_out)
    w_in_a, w_in_b = bf(w_in[:, :, :a_proj]), bf(w_in[:, :, a_proj:])
    w_gate_b, w_br_a_b, w_br_b_b, w_out_b = bf(w_gate), bf(w_br_a), bf(w_br_b), bf(w_out)
    w2_b, a2_b, g2_b, w_pool_b = bf(w2), bf(a2), bf(g2), bf(w_pool)
    head_of = jnp.arange(MXU_TILE, dtype=jnp.int32) // HEAD_DIM
    seg_ones = (head_of[:, None] == head_of[None, :]).astype(BF16)

    mod = _mod_call(jnp.concatenate([c_sample, c_prompt], axis=0), w_mod, b_mod)
    grp = _Groups(bp, lp, bs, ls, min(ROW_TILE, lp), mod_row0=bs)

    xp = x_prompt.reshape(bp * lp, d)
    xs = jnp.transpose(x_sample, (1, 0, 2)).reshape(ls * bs, d)
    hist_s = jnp.transpose(state_pool, (0, 2, 1, 3)).reshape(depth, POOL_HIST * bs, bw)
    state_t = jnp.transpose(state_wkv, (0, 2, 3, 4, 1))
    wkv_s = jnp.zeros_like(state_t)

    wkv_p, shift_p, pool_p, shift_s, pool_s = [], [], [], [], []
    for l in range(depth):
        prm = dict(mu=mu_shift[l][None], w0=w0[l][None], w2=w2_b[l], a0=a0[l][None], a2=a2_b[l], g2=g2_b[l],
                   k_k=k_k[l][None], k_a=k_a[l][None], r_k=r_k[l].reshape(1, aw), ln_w=ln_x_w[l][None],
                   ln_b=ln_x_b[l][None], w_pool=w_pool_b[l], pool_scale=pool_scale[l][None], seg=seg_ones)
        xp, xs = _ffn_call(xp, xs, grp, mod, l, 0, norm_g[l, 0], w_ffn_in_b, w_ffn_out_b)
        pa_p, pb_p, pa_s, pb_s = _mixin_call(xp, xs, grp, mod, l, norm_g[l, 1], w_in_a[l], w_in_b[l])
        ya_p, yb_p, s_new = _prompt_mix_call(pa_p, pb_p, bp, lp, prm)
        *vecs, yb_s = _sample_prep_call(pa_s, state_shift[l, :, 0, :], pb_s, hist_s[l], bs, ls, prm)
        ya_s, wkv_s = _sample_wkv_call(vecs, state_t, wkv_s, l, bs, ls, prm)
        xp, xs = _mixout_call(xp, xs, grp, mod, l, norm_g[l, 1], ya_p, yb_p, ya_s, yb_s, w_gate_b, b_gate, w_br_a_b,
                              w_br_b_b, w_out_b)
        xp, xs = _ffn_call(xp, xs, grp, mod, l, 2, norm_g[l, 2], w_ffn_in_b, w_ffn_out_b,
                           final_g if l == depth - 1 else None)
        wkv_p.append(s_new)
        shift_p.append(pa_p.reshape(bp, lp, a_proj)[:, -1:])
        pool_p.append(pb_p.reshape(bp, lp, bw)[:, -POOL_HIST:])
        shift_s.append(pa_s.reshape(ls, bs, a_proj)[-1][:, None, :])
        pb_bm = jnp.transpose(pb_s.reshape(ls, bs, bw), (1, 0, 2))
        pool_s.append(jnp.concatenate([state_pool[l], pb_bm], axis=1)[:, -POOL_HIST:])

    y_prompt = xp.reshape(bp, lp, d)
    y_sample = jnp.transpose(xs.reshape(ls, bs, d), (1, 0, 2))
    return (y_prompt, y_sample, jnp.stack(wkv_p), jnp.stack(shift_p), jnp.stack(pool_p),
            jnp.transpose(wkv_s, (0, 4, 1, 2, 3)), jnp.stack(shift_s), jnp.stack(pool_s))
```

```python
import functools
import math

import jax
import jax.numpy as jnp
from jax import lax
from jax.experimental import pallas as pl
from jax.experimental.pallas import tpu as pltpu

F32 = jnp.float32
BF16 = jnp.bfloat16
MIX_DTYPE = BF16

HEAD_DIM = 64
POOL_WINDOWS = (2, 4, 8, 16)
POOL_HIST = max(POOL_WINDOWS) - 1
N_SUB = 3
NORM_EPS = 1e-6
GN_EPS = 64e-5
PAST_LEN = 16384
DECAY_SCALE = math.exp(-0.5)

LANES = 128
MXU_TILE = 256
ROW_TILE = 1024
WKV_CHUNK = 64
VMEM_LIMIT_BYTES = 56 * 1024 * 1024


_RESIDENT = dict(pipeline_mode=pl.Buffered(1))


def _cparams(n_axes, **kw):
    return pltpu.CompilerParams(dimension_semantics=("arbitrary",) * n_axes,
                                vmem_limit_bytes=VMEM_LIMIT_BYTES, **kw)


def _sigmoid(x):
    return 0.5 * jnp.tanh(0.5 * x) + 0.5


def _dot(a, b):
    return jnp.dot(a.astype(BF16), b.astype(BF16), preferred_element_type=F32)


def _dot_nt(a, b):
    return lax.dot_general(a.astype(BF16), b.astype(BF16), (((1,), (1,)), ((), ())),
                           preferred_element_type=F32)


def _dot_tn(a, b):
    return lax.dot_general(a.astype(BF16), b.astype(BF16), (((0,), (0,)), ((), ())),
                           preferred_element_type=F32)


def _seg_sum(x, seg_ones):
    n, width = x.shape
    tw = seg_ones.shape[0]
    outs = []
    for c0 in range(0, width, tw):
        xc = x[:, c0:c0 + tw]
        hi = xc.astype(BF16)
        lo = (xc - hi.astype(F32)).astype(BF16)
        both = jnp.dot(jnp.concatenate([hi, lo], axis=0), seg_ones, preferred_element_type=F32)
        outs.append(both[:n] + both[n:])
    return jnp.concatenate(outs, axis=1)


def _rms(x):
    return x * lax.rsqrt(jnp.mean(x * x, axis=-1, keepdims=True) + NORM_EPS)


SUBLANES = 8


def _rows(ref, tm, seq_row):
    if seq_row is not None:
        return ref[pl.ds(seq_row, 1), :]
    m = ref[...]
    return m if m.shape[0] == tm else jnp.concatenate([m] * (tm // m.shape[0]), axis=0)


def _modnorm(x, g, shift, scale):
    return _rms(x) * g * (1.0 + scale) + shift


def _mod_kernel(c_ref, w_ref, b_ref, o_ref):
    c = c_ref[...]
    o_ref[...] = _dot(c * _sigmoid(c), w_ref[...]) + b_ref[...]


def _mod_call(c_all, w_mod, b_mod):
    depth, d, n9 = w_mod.shape
    rows = c_all.shape[0]
    n_sub9 = n9 // d
    tn = d
    return pl.pallas_call(
        _mod_kernel,
        grid=(depth, n_sub9),
        in_specs=[pl.BlockSpec((rows, d), lambda l, j: (0, 0)),
                  pl.BlockSpec((None, d, tn), lambda l, j: (l, 0, j)),
                  pl.BlockSpec((None, 1, tn), lambda l, j: (l, 0, j))],
        out_specs=pl.BlockSpec((None, None, rows, tn), lambda l, j: (l, j, 0, 0)),
        out_shape=jax.ShapeDtypeStruct((depth, n_sub9, rows, d), F32),
        compiler_params=_cparams(2),
        name="mod",
    )(c_all, w_mod, b_mod.reshape(depth, 1, n9))


class _Groups:
    def __init__(self, bp, lp, bs, ls, tm, mod_row0):
        assert lp % tm == 0
        self.tm, self.ts, self.bs = tm, bs * ls, bs
        self.n_prompt = bp * lp // tm
        self.seq_rows = (mod_row0, lp // tm)

    def _ptile(self, i):
        return jnp.minimum(i, self.n_prompt - 1)

    def prompt(self, width):
        return pl.BlockSpec((self.tm, width), lambda i: (self._ptile(i), 0))

    def sample(self, width, **kw):
        return pl.BlockSpec((self.ts, width), lambda i: (0, 0), **kw)

    def prompt_mod(self, d, layer, idx):
        row0, per_seq = self.seq_rows
        return pl.BlockSpec((None, None, SUBLANES, d),
                            lambda i: (layer, idx, (row0 + self._ptile(i) // per_seq) // SUBLANES, 0))

    def sample_mod(self, d, layer, idx):
        return pl.BlockSpec((None, None, self.bs, d), lambda i: (layer, idx, 0, 0), **_RESIDENT)

    def mods(self, d, layer, idxs):
        return ([self.prompt_mod(d, layer, k) for k in idxs] + [self.sample_mod(d, layer, k) for k in idxs])

    def run(self, body, prompt_refs, sample_refs):
        i = pl.program_id(0)
        row0, per_seq = self.seq_rows
        seq_row = (row0 + i // per_seq) % SUBLANES
        pl.when(i < self.n_prompt)(lambda: body(prompt_refs, seq_row))
        pl.when(i == self.n_prompt)(lambda: body(sample_refs, None))


FFN_CHUNK = 2 * MXU_TILE


def _ffn_kernel(xp_ref, xs_ref, psh, psc, pgt, ssh, ssc, sgt, g_ref, win_ref, wo_ref, *rest, final, grp):
    if final:
        fg_ref, yp_ref, ys_ref, acc_sc = rest
    else:
        yp_ref, ys_ref, acc_sc = rest
    d_ff = wo_ref.shape[0]

    def body(refs, seq_row):
        x_ref, o_ref, sh_ref, sc_ref, gt_ref = refs
        tm = x_ref.shape[0]
        x = x_ref[...]
        u = _modnorm(x, g_ref[...], _rows(sh_ref, tm, seq_row), _rows(sc_ref, tm, seq_row)).astype(BF16)
        for c0 in range(0, d_ff, FFN_CHUNK):
            c1 = min(c0 + FFN_CHUNK, d_ff)
            hg = jnp.dot(u, win_ref[:, c0:c1], preferred_element_type=F32)
            hu = jnp.dot(u, win_ref[:, d_ff + c0:d_ff + c1], preferred_element_type=F32)
            act = (hg * _sigmoid(hg) * hu).astype(BF16)
            part = jnp.dot(act, wo_ref[c0:c1, :], preferred_element_type=F32)
            if c0 == 0:
                acc_sc[0:tm, :] = part
            else:
                acc_sc[0:tm, :] += part
        y = x + 0.5 * _rows(gt_ref, tm, seq_row) * acc_sc[0:tm, :]
        if final:
            y = _rms(y) * fg_ref[...]
        o_ref[...] = y

    grp.run(body, (xp_ref, yp_ref, psh, psc, pgt), (xs_ref, ys_ref, ssh, ssc, sgt))


def _ffn_call(xp, xs, grp, mod, layer, sub, norm_g, w_in, w_out, final_g=None):
    d = xp.shape[1]
    d_ff = w_out.shape[2]
    assert d_ff % LANES == 0 and FFN_CHUNK % LANES == 0 and grp.ts <= grp.tm
    final = final_g is not None
    const = lambda i: (0, 0)
    half = lambda i: (layer, sub // 2, 0, 0)
    in_specs = ([grp.prompt(d), grp.sample(d, **_RESIDENT)] + grp.mods(d, layer, (3 * sub, 3 * sub + 1, 3 * sub + 2))
                + [pl.BlockSpec((1, d), const),
                   pl.BlockSpec((None, None, d, 2 * d_ff), half, **_RESIDENT),
                   pl.BlockSpec((None, None, d_ff, d), half, **_RESIDENT)])
    args = [xp, xs] + [mod] * 6 + [norm_g.reshape(1, d), w_in, w_out]
    if final:
        in_specs.append(pl.BlockSpec((1, d), const))
        args.append(final_g.reshape(1, d))
    return pl.pallas_call(
        functools.partial(_ffn_kernel, final=final, grp=grp),
        grid=(grp.n_prompt + 1,),
        in_specs=in_specs,
        out_specs=[grp.prompt(d), grp.sample(d)],
        out_shape=[jax.ShapeDtypeStruct(xp.shape, F32), jax.ShapeDtypeStruct(xs.shape, F32)],
        scratch_shapes=[pltpu.VMEM((grp.tm, d), F32)],
        compiler_params=_cparams(1),
        name="ffn",
    )(*args)


def _mixin_kernel(xp_ref, xs_ref, psh, psc, ssh, ssc, g_ref, wa_ref, wb_ref, pap_ref, pbp_ref, pas_ref, pbs_ref, *, grp):
    def body(refs, seq_row):
        x_ref, pa_ref, pb_ref, sh_ref, sc_ref = refs
        tm = x_ref.shape[0]
        u = _modnorm(x_ref[...], g_ref[...], _rows(sh_ref, tm, seq_row), _rows(sc_ref, tm, seq_row)).astype(BF16)
        pa_ref[...] = jnp.dot(u, wa_ref[...], preferred_element_type=F32)
        pb_ref[...] = jnp.dot(u, wb_ref[...], preferred_element_type=F32)

    grp.run(body, (xp_ref, pap_ref, pbp_ref, psh, psc), (xs_ref, pas_ref, pbs_ref, ssh, ssc))


def _mixin_call(xp, xs, grp, mod, layer, norm_g, w_a, w_b):
    d = xp.shape[1]
    na, nb = w_a.shape[1], w_b.shape[1]
    const = lambda i: (0, 0)
    shape = lambda x, n: jax.ShapeDtypeStruct((x.shape[0], n), F32)
    return pl.pallas_call(
        functools.partial(_mixin_kernel, grp=grp),
        grid=(grp.n_prompt + 1,),
        in_specs=([grp.prompt(d), grp.sample(d, **_RESIDENT)] + grp.mods(d, layer, (3, 4))
                  + [pl.BlockSpec((1, d), const), pl.BlockSpec((d, na), const, **_RESIDENT),
                     pl.BlockSpec((d, nb), const, **_RESIDENT)]),
        out_specs=[grp.prompt(na), grp.prompt(nb), grp.sample(na), grp.sample(nb)],
        out_shape=[shape(xp, na), shape(xp, nb), shape(xs, na), shape(xs, nb)],
        compiler_params=_cparams(1),
        name="mixin",
    )(xp, xs, mod, mod, mod, mod, norm_g.reshape(1, d), w_a, w_b)


def _mixout_kernel(xp_ref, xs_ref, yap_ref, ybp_ref, yas_ref, ybs_ref, psh, psc, pgt, ssh, ssc, sgt, g_ref,
                   wgate_ref, bgate_ref, wbra_ref, wbrb_ref, wout_ref, op_ref, os_ref, *, grp):
    def body(refs, seq_row):
        x_ref, ya_ref, yb_ref, o_ref, sh_ref, sc_ref, gt_ref = refs
        tm, d = x_ref.shape
        x = x_ref[...]
        u = _modnorm(x, g_ref[...], _rows(sh_ref, tm, seq_row), _rows(sc_ref, tm, seq_row)).astype(BF16)
        gates = _sigmoid(jnp.dot(u, wgate_ref[...], preferred_element_type=F32) + bgate_ref[...])
        ma = _dot(ya_ref[...], wbra_ref[...])
        mb = _dot(yb_ref[...], wbrb_ref[...])
        merged = gates[:, :d] * ma + gates[:, d:] * mb
        o_ref[...] = x + _rows(gt_ref, tm, seq_row) * _dot(merged, wout_ref[...])

    grp.run(body, (xp_ref, yap_ref, ybp_ref, op_ref, psh, psc, pgt), (xs_ref, yas_ref, ybs_ref, os_ref, ssh, ssc, sgt))


def _mixout_call(xp, xs, grp, mod, layer, norm_g, ya_p, yb_p, ya_s, yb_s, w_gate, b_gate, w_br_a, w_br_b, w_out):
    d = xp.shape[1]
    wa, wb = ya_p.shape[1], yb_p.shape[1]
    const = lambda i: (0, 0)
    lyr = lambda i: (layer, 0, 0)
    return pl.pallas_call(
        functools.partial(_mixout_kernel, grp=grp),
        grid=(grp.n_prompt + 1,),
        in_specs=([grp.prompt(d), grp.sample(d, **_RESIDENT), grp.prompt(wa), grp.prompt(wb),
                   grp.sample(wa, **_RESIDENT), grp.sample(wb, **_RESIDENT)] + grp.mods(d, layer, (3, 4, 5))
                  + [pl.BlockSpec((1, d), const),
                     pl.BlockSpec((None, d, 2 * d), lyr, **_RESIDENT), pl.BlockSpec((None, 1, 2 * d), lyr),
                     pl.BlockSpec((None, wa, d), lyr, **_RESIDENT), pl.BlockSpec((None, wb, d), lyr, **_RESIDENT),
                     pl.BlockSpec((None, d, d), lyr, **_RESIDENT)]),
        out_specs=[grp.prompt(d), grp.sample(d)],
        out_shape=[jax.ShapeDtypeStruct(xp.shape, F32), jax.ShapeDtypeStruct(xs.shape, F32)],
        compiler_params=_cparams(1),
        name="mixout",
    )(xp, xs, ya_p, yb_p, ya_s, yb_s, *([mod] * 6), norm_g.reshape(1, d), w_gate, b_gate.reshape(-1, 1, 2 * d),
      w_br_a, w_br_b, w_out)


def _rwkv_prep(xs, aw, seg_ones, w0, w2, a0, a2, g2, k_k, k_a):
    n_dec, n_aaa = w2.shape[0], a2.shape[0]
    r = xs[:, 0:aw]
    k = xs[:, aw:2 * aw]
    v = xs[:, 2 * aw:3 * aw]
    o = 3 * aw
    wl = xs[:, o:o + n_dec]
    al = xs[:, o + n_dec:o + n_dec + n_aaa]
    gl = xs[:, o + n_dec + n_aaa:]
    logw = -DECAY_SCALE * _sigmoid(w0 + _dot(jnp.tanh(wl), w2))
    asig = _sigmoid(a0 + _dot(al, a2))
    g = _dot(_sigmoid(gl), g2)
    kk = k * k_k
    kk = kk * lax.rsqrt(jnp.maximum(_seg_sum(kk * kk, seg_ones), 1e-24))
    k = k * (1.0 + (asig - 1.0) * k_a)
    return r, k, v, logw, -kk, kk * asig, g


def _pool_group_sum(read_rows, w):
    s = read_rows(0)
    for j in range(1, w):
        s = s + read_rows(j)
    return s


PROMPT_BLOCK = 128
PROMPT_SEQS = 4
SHIFT_PAD = 8
POOL_PAD = 24


def _pool_window_sums(xbuf, e2, e4, e8, s, tb):
    gw = xbuf.shape[2] // len(POOL_WINDOWS)
    lo, hi = POOL_PAD - 16, POOL_PAD + tb
    e2[s, lo:hi, :] = xbuf[s, lo:hi, :] + xbuf[s, lo - 1:hi - 1, :]
    e4[s, lo:hi, :] = e2[s, lo:hi, gw:] + e2[s, lo - 2:hi - 2, gw:]
    e8[s, lo:hi, :] = e4[s, lo:hi, gw:] + e4[s, lo - 4:hi - 4, gw:]
    blk = slice(POOL_PAD, POOL_PAD + tb)
    back8 = slice(POOL_PAD - 8, POOL_PAD - 8 + tb)
    return (e2[s, blk, 0:gw], e4[s, blk, 0:gw], e8[s, blk, 0:gw], e8[s, blk, gw:] + e8[s, back8, gw:])


def _prompt_mix_kernel(pa_ref, pb_ref, mu_ref, w0_ref, w2_ref, a0_ref, a2_ref, g2_ref, kk_ref, ka_ref, rk_ref,
                       lnw_ref, lnb_ref, wpool_ref, pscale_ref, seg_ref,
                       ya_ref, yb_ref, wkv_ref,
                       pbuf, xbuf, e2, e4, e8, s_sc, r_sc, k_sc, v_sc, a_sc, b_sc, lw_sc, g_sc, y_sc, *, n_blocks):
    ns, tb, aw = ya_ref.shape
    bw = yb_ref.shape[2]
    t = WKV_CHUNK
    n_pairs = aw // LANES
    blk = pl.program_id(1)
    seqs = range(ns)

    @pl.when(blk == 0)
    def _():
        pbuf[:, 0:SHIFT_PAD, :] = jnp.zeros((ns, SHIFT_PAD, pbuf.shape[2]), F32)
        xbuf[:, 0:POOL_PAD, :] = jnp.zeros((ns, POOL_PAD, bw), F32)
        e2[:, 0:POOL_PAD, :] = jnp.zeros((ns, POOL_PAD, e2.shape[2]), F32)
        e4[:, 0:POOL_PAD, :] = jnp.zeros((ns, POOL_PAD, e4.shape[2]), F32)
        s_sc[...] = jnp.zeros_like(s_sc)

    xs = []
    for s in seqs:
        pa = pa_ref[s]
        pbuf[s, SHIFT_PAD:SHIFT_PAD + tb, :] = pa
        prev = pbuf[s, SHIFT_PAD - 1:SHIFT_PAD - 1 + tb, :]
        xs.append(pa + (prev - pa) * mu_ref[...])
        pbuf[s, 0:SHIFT_PAD, :] = pbuf[s, tb:tb + SHIFT_PAD, :]
    xs = jnp.concatenate(xs, axis=0)

    seg_ones = seg_ref[...]
    r, k, v, logw, a, b, g = _rwkv_prep(xs, aw, seg_ones, w0_ref[...], w2_ref[...], a0_ref[...], a2_ref[...],
                                        g2_ref[...], kk_ref[...], ka_ref[...])
    r_sc[...] = r
    k_sc[...] = k
    v_sc[...] = v
    a_sc[...] = a
    b_sc[...] = b
    lw_sc[...] = logw
    g_sc[...] = g

    t2 = 2 * t
    ri = lax.broadcasted_iota(jnp.int32, (t, t), 0)
    ci = lax.broadcasted_iota(jnp.int32, (t, t), 1)
    tri_incl = (ri >= ci).astype(BF16)
    rs = lax.broadcasted_iota(jnp.int32, (t2, t2), 0)
    cs = lax.broadcasted_iota(jnp.int32, (t2, t2), 1)
    strict = cs < rs
    rs2 = lax.broadcasted_iota(jnp.int32, (t2, 2 * t2), 0)
    cs2 = lax.broadcasted_iota(jnp.int32, (t2, 2 * t2), 1)
    incl2 = jnp.where(cs2 >= t2, cs2 - t2, cs2) <= rs2
    lane = lax.broadcasted_iota(jnp.int32, (1, LANES), 1)
    head_mask = (lane < HEAD_DIM, lane >= HEAD_DIM)
    units = [(s, p) for s in seqs for p in range(n_pairs)]
    n_levels = int(math.log2(t))

    def stack(x):
        return jnp.concatenate([jnp.where(head_mask[0], x, 0.0), jnp.where(head_mask[1], x, 0.0)], axis=0)

    def chunk(ci_, carry):
        at, rt, bt, kt, vv, p_end, sls = [], [], [], [], [], [], []
        for s in seqs:
            sl = pl.ds(pl.multiple_of(s * tb + ci_ * t, t), t)
            lw = lw_sc[sl, :]
            hi = lw.astype(BF16)
            lo = (lw - hi.astype(F32)).astype(BF16)
            c = (jnp.dot(tri_incl, hi, preferred_element_type=F32)
                 + jnp.dot(tri_incl, lo, preferred_element_type=F32))
            e_in = jnp.exp(c)
            e_neg = jnp.exp(-c)
            p_end.append(e_in[t - 1:t, :])
            at.append(a_sc[sl, :] * jnp.exp(c - lw))
            rt.append(r_sc[sl, :] * e_in)
            bt.append(b_sc[sl, :] * e_neg)
            kt.append(k_sc[sl, :] * e_neg)
            vv.append(v_sc[sl, :])
            sls.append(sl)
        cols = [slice(p * LANES, (p + 1) * LANES) for p in range(n_pairs)]
        un = range(len(units))
        pe = [p_end[s][:, cols[p]] for s, p in units]
        a2 = [stack(at[s][:, cols[p]]).astype(BF16) for s, p in units]
        r2 = [stack(rt[s][:, cols[p]]).astype(BF16) for s, p in units]
        v2b = [stack(vv[s][:, cols[p]]).astype(BF16) for s, p in units]
        bk = [jnp.concatenate([stack(bt[s][:, cols[p]]), stack(kt[s][:, cols[p]])], axis=0) for s, p in units]
        g_ = [_dot_nt(jnp.concatenate([a2[i], r2[i]], axis=0), bk[i]) for i in un]
        n = [jnp.where(strict, g_[i][:t2, :t2], 0.0) for i in un]
        ak = [jnp.where(strict, g_[i][:t2, t2:], 0.0).astype(BF16) for i in un]
        rbk = [jnp.where(incl2, g_[i][t2:, :], 0.0).astype(BF16) for i in un]
        xm = n
        pw = [_dot(n[i], n[i]) for i in un]
        for _ in range(n_levels - 2):
            res = [_dot(jnp.concatenate([pw[i], xm[i]], axis=0), pw[i]) for i in un]
            xm = [xm[i] + pw[i] + res[i][t2:] for i in un]
            pw = [res[i][:t2] for i in un]
        res = [_dot(xm[i], pw[i]) for i in un]
        xm = [(xm[i] + pw[i] + res[i]).astype(BF16) for i in un]
        s_old = [s_sc[s * n_pairs + p] for s, p in units]
        s_bf = [s_old[i].astype(BF16) for i in un]
        rhs = [_dot_nt(a2[i], s_bf[i]) for i in un]
        akv = [jnp.dot(ak[i], v2b[i], preferred_element_type=F32) for i in un]
        rhs = [rhs[i] + akv[i] for i in un]
        u = [rhs[i] + _dot(xm[i], rhs[i]) for i in un]
        uv = [jnp.concatenate([u[i].astype(BF16), v2b[i]], axis=0) for i in un]
        ys = [_dot_nt(r2[i], s_bf[i]) for i in un]
        yi = [jnp.dot(rbk[i], uv[i], preferred_element_type=F32) for i in un]
        sn = [_dot_tn(uv[i], bk[i] * pe[i]) for i in un]
        for i, (s, p) in enumerate(units):
            y2 = ys[i] + yi[i]
            y_sc[sls[s], cols[p]] = y2[:t] + y2[t:]
            s_sc[s * n_pairs + p] = s_old[i] * pe[i] + sn[i]
        return carry

    lax.fori_loop(0, tb // t, chunk, 0)

    y = y_sc[...]
    inv_n = 1.0 / HEAD_DIM
    mean = _seg_sum(y, seg_ones) * inv_n
    yc = y - mean
    var = _seg_sum(yc * yc, seg_ones) * inv_n
    yn = yc * lax.rsqrt(var + GN_EPS) * lnw_ref[...] + lnb_ref[...]
    bonus = _seg_sum(r_sc[...] * k_sc[...] * rk_ref[...], seg_ones) * v_sc[...]
    ya = (yn + bonus) * g_sc[...]
    for s in seqs:
        ya_ref[s] = ya[s * tb:(s + 1) * tb].astype(ya_ref.dtype)

    pos = blk * tb + lax.broadcasted_iota(jnp.int32, (tb, 1), 0)
    gw = bw // len(POOL_WINDOWS)
    for s in seqs:
        pb = pb_ref[s]
        xbuf[s, POOL_PAD:POOL_PAD + tb, :] = pb
        sums = _pool_window_sums(xbuf, e2, e4, e8, s, tb)
        for gi, w in enumerate(POOL_WINDOWS):
            cols = slice(gi * gw, (gi + 1) * gw)
            inv_cnt = 1.0 / jnp.minimum(w, pos + 1).astype(F32)
            dlt = sums[gi] * inv_cnt - pb[:, cols]
            yb_ref[s, :, cols] = (_dot(dlt, wpool_ref[gi]) * pscale_ref[:, cols]).astype(yb_ref.dtype)
        xbuf[s, POOL_PAD - 16:POOL_PAD, :] = xbuf[s, tb + POOL_PAD - 16:tb + POOL_PAD, :]

    @pl.when(blk == n_blocks - 1)
    def _():
        for s in seqs:
            for p in range(n_pairs):
                s_p = s_sc[s * n_pairs + p]
                for h in range(2):
                    o = h * HEAD_DIM
                    wkv_ref[s, 2 * p + h] = s_p[o:o + HEAD_DIM, o:o + HEAD_DIM]


def _prompt_mix_call(pa, pb, batch, seq, prm):
    na = pa.shape[1]
    bw = pb.shape[1]
    aw = prm["w0"].shape[1]
    n_heads = aw // HEAD_DIM
    tb, ns = min(PROMPT_BLOCK, seq), PROMPT_SEQS
    assert seq % tb == 0 and tb % WKV_CHUNK == 0 and aw % LANES == 0 and batch % ns == 0
    assert len(POOL_WINDOWS) == 4 and POOL_HIST < 16 <= POOL_PAD - 8
    n_blocks = seq // tb
    gw = bw // len(POOL_WINDOWS)
    row = lambda b, c: (b, c, 0)
    const2 = lambda b, c: (0, 0)
    const3 = lambda b, c: (0, 0, 0)
    full = lambda a: pl.BlockSpec(a.shape, const2 if a.ndim == 2 else const3)
    names = ("mu", "w0", "w2", "a0", "a2", "g2", "k_k", "k_a", "r_k", "ln_w", "ln_b", "w_pool", "pool_scale", "seg")
    params = [prm[n] for n in names]
    blk = lambda: pltpu.VMEM((ns * tb, aw), F32)
    ya, yb, wkv = pl.pallas_call(
        functools.partial(_prompt_mix_kernel, n_blocks=n_blocks),
        grid=(batch // ns, n_blocks),
        in_specs=[pl.BlockSpec((ns, tb, na), row), pl.BlockSpec((ns, tb, bw), row)] + [full(a) for a in params],
        out_specs=[pl.BlockSpec((ns, tb, aw), row), pl.BlockSpec((ns, tb, bw), row),
                   pl.BlockSpec((ns, n_heads, HEAD_DIM, HEAD_DIM), lambda b, c: (b, 0, 0, 0))],
        out_shape=[jax.ShapeDtypeStruct((batch, seq, aw), MIX_DTYPE), jax.ShapeDtypeStruct((batch, seq, bw), MIX_DTYPE),
                   jax.ShapeDtypeStruct((batch, n_heads, HEAD_DIM, HEAD_DIM), F32)],
        scratch_shapes=[pltpu.VMEM((ns, tb + SHIFT_PAD, na), F32), pltpu.VMEM((ns, tb + POOL_PAD, bw), F32),
                        pltpu.VMEM((ns, tb + POOL_PAD, bw), F32), pltpu.VMEM((ns, tb + POOL_PAD, bw - gw), F32),
                        pltpu.VMEM((ns, tb + POOL_PAD, bw - 2 * gw), F32),
                        pltpu.VMEM((ns * (aw // LANES), LANES, LANES), F32),
                        blk(), blk(), blk(), blk(), blk(), blk(), blk(), blk()],
        compiler_params=_cparams(2),
        name="mix_prompt",
    )(pa.reshape(batch, seq, na), pb.reshape(batch, seq, bw), *params)
    return ya.reshape(batch * seq, aw), yb.reshape(batch * seq, bw), wkv


def _sample_prep_kernel(pa_ref, sh0_ref, pb_ref, hist_ref, mu_ref, w0_ref, w2_ref, a0_ref, a2_ref, g2_ref,
                        kk_ref, ka_ref, wpool_ref, pscale_ref, seg_ref,
                        r_ref, w_ref, k_ref, v_ref, a_ref, b_ref, g_ref, yb_ref, *, batch, seq):
    aw = r_ref.shape[1]
    bw = yb_ref.shape[1]
    pa = pa_ref[...]
    prev = jnp.concatenate([sh0_ref[...], pa[:(seq - 1) * batch]], axis=0)
    xs = pa + (prev - pa) * mu_ref[...]
    r, k, v, logw, a, b, g = _rwkv_prep(xs, aw, seg_ref[...], w0_ref[...], w2_ref[...], a0_ref[...], a2_ref[...],
                                        g2_ref[...], kk_ref[...], ka_ref[...])
    outs = ((r_ref, r), (w_ref, jnp.exp(logw)), (k_ref, k), (v_ref, v), (a_ref, a), (b_ref, b), (g_ref, g))
    for ref, val in outs:
        for ti in range(seq):
            ref[ti] = val[ti * batch:(ti + 1) * batch, :].T

    gw = bw // len(POOL_WINDOWS)

    def slab(i, cols):
        if i < POOL_HIST:
            return hist_ref[i * batch:(i + 1) * batch, cols]
        return pb_ref[(i - POOL_HIST) * batch:(i - POOL_HIST + 1) * batch, cols]

    for ti in range(seq):
        rows = slice(ti * batch, (ti + 1) * batch)
        for gi, w in enumerate(POOL_WINDOWS):
            cols = slice(gi * gw, (gi + 1) * gw)
            s = _pool_group_sum(lambda j: slab(POOL_HIST + ti - j, cols), w)
            cnt = float(min(w, PAST_LEN + ti + 1))
            dlt = s / cnt - pb_ref[rows, cols]
            yb_ref[rows, cols] = (_dot(dlt, wpool_ref[gi]) * pscale_ref[:, cols]).astype(yb_ref.dtype)


def _sample_prep_call(pa, sh0, pb, hist, batch, seq, prm):
    rows = pa.shape[0]
    aw = prm["w0"].shape[1]
    bw = pb.shape[1]
    names = ("mu", "w0", "w2", "a0", "a2", "g2", "k_k", "k_a", "w_pool", "pool_scale", "seg")
    args = [pa, sh0, pb, hist] + [prm[n] for n in names]
    full = lambda a: pl.BlockSpec(a.shape, (lambda i: (0, 0)) if a.ndim == 2 else (lambda i: (0, 0, 0)))
    o = jax.ShapeDtypeStruct((seq, aw, batch), F32)
    ob = jax.ShapeDtypeStruct((rows, bw), MIX_DTYPE)
    return pl.pallas_call(
        functools.partial(_sample_prep_kernel, batch=batch, seq=seq),
        grid=(1,),
        in_specs=[full(a) for a in args],
        out_specs=[full(o)] * 7 + [full(ob)],
        out_shape=[o] * 7 + [ob],
        compiler_params=_cparams(1),
        name="prep_sample",
    )(*args)


SAMPLE_VROWS = 64


def _sample_wkv_kernel(r_ref, w_ref, k_ref, v_ref, a_ref, b_ref, g_ref, rk_ref, lnw_ref, lnb_ref, s_ref, all_ref,
                       y_ref, so_ref, y_sc):
    del all_ref
    seq, hd, batch = r_ref.shape
    vb = s_ref.shape[0]
    h, vs = pl.program_id(0), pl.program_id(1)
    hrow = pl.multiple_of((h % 2) * hd, hd)
    for q in range(vb):
        tile = s_ref[q]
        for ti in range(seq):
            sa = jnp.sum(tile * a_ref[ti], axis=0, keepdims=True)
            tile = tile * w_ref[ti] + sa * b_ref[ti] + v_ref[ti, pl.ds(vs * vb + q, 1), :] * k_ref[ti]
            y_sc[ti, pl.ds(hrow + vs * vb + q, 1), :] = jnp.sum(tile * r_ref[ti], axis=0, keepdims=True)
        so_ref[q] = tile

    @pl.when(vs == pl.num_programs(1) - 1)
    def _():
        for ti in range(seq):
            rows = pl.ds(hrow, hd)
            y = y_sc[ti, rows, :]
            yc = y - jnp.mean(y, axis=0, keepdims=True)
            var = jnp.mean(yc * yc, axis=0, keepdims=True)
            yn = yc * lax.rsqrt(var + GN_EPS) * lnw_ref[...] + lnb_ref[...]
            bonus = jnp.sum(r_ref[ti] * k_ref[ti] * rk_ref[...], axis=0, keepdims=True) * v_ref[ti]
            y_sc[ti, rows, :] = (yn + bonus) * g_ref[ti]

    @pl.when((vs == pl.num_programs(1) - 1) & (h % 2 == 1))
    def _():
        for ti in range(seq):
            y_ref[ti * batch:(ti + 1) * batch, :] = y_sc[ti].T.astype(y_ref.dtype)


def _sample_wkv_call(vecs, state_t, new_states, layer, batch, seq, prm):
    n_heads, hd = state_t.shape[1:3]
    aw = n_heads * hd
    vb = SAMPLE_VROWS
    assert 2 * hd == LANES and hd % vb == 0
    vspec = pl.BlockSpec((seq, hd, batch), lambda h, s: (0, h, 0))
    cspec = pl.BlockSpec((hd, 1), lambda h, s: (h, 0))
    sspec = pl.BlockSpec((None, None, vb, hd, batch), lambda h, s: (layer, h, s, 0, 0))
    cols = [prm[n].reshape(aw, 1) for n in ("r_k", "ln_w", "ln_b")]
    args = [*vecs, *cols, state_t, new_states]
    return pl.pallas_call(
        _sample_wkv_kernel,
        grid=(n_heads, hd // vb),
        in_specs=[vspec] * 7 + [cspec] * 3 + [sspec, pl.BlockSpec(memory_space=pl.ANY)],
        out_specs=[pl.BlockSpec((seq * batch, LANES), lambda h, s: (0, h // 2)), sspec],
        out_shape=[jax.ShapeDtypeStruct((seq * batch, aw), MIX_DTYPE), jax.ShapeDtypeStruct(state_t.shape, F32)],
        scratch_shapes=[pltpu.VMEM((seq, LANES, batch), F32)],
        input_output_aliases={len(args) - 1: 1},
        compiler_params=_cparams(2),
        name="wkv_sample",
    )(*args)


def kernel(x_prompt, x_sample, state_wkv, state_shift, state_pool, c_prompt, c_sample, norm_g, w_mod, b_mod, w_ffn_in, w_ffn_out, w_in, mu_shift, w0, w2, a0, a2, g2, k_k, k_a, r_k, ln_x_w, ln_x_b, w_pool, pool_scale, w_br_a, w_br_b, w_gate, b_gate, w_out, final_g):
    bp, lp, d = x_prompt.shape
    bs, ls, _ = x_sample.shape
    depth = w_mod.shape[0]
    aw = w0.shape[1]
    a_proj = mu_shift.shape[1]
    bw = pool_scale.shape[1]

    bf = lambda w: w.astype(BF16)
    w_ffn_in_b, w_ffn_out_b = bf(w_ffn_in), bf(w_ffn_out)
    w_in_a, w_in_b = bf(w_in[:, :, :a_proj]), bf(w_in[:, :, a_proj:])
    w_gate_b, w_br_a_b, w_br_b_b, w_out_b = bf(w_gate), bf(w_br_a), bf(w_br_b), bf(w_out)
    w2_b, a2_b, g2_b, w_pool_b = bf(w2), bf(a2), bf(g2), bf(w_pool)
    head_of = jnp.arange(MXU_TILE, dtype=jnp.int32) // HEAD_DIM
    seg_ones = (head_of[:, None] == head_of[None, :]).astype(BF16)

    mod = _mod_call(jnp.concatenate([c_sample, c_prompt], axis=0), w_mod, b_mod)
    grp = _Groups(bp, lp, bs, ls, min(ROW_TILE, lp), mod_row0=bs)

    xp = x_prompt.reshape(bp * lp, d)
    xs = jnp.transpose(x_sample, (1, 0, 2)).reshape(ls * bs, d)
    hist_s = jnp.transpose(state_pool, (0, 2, 1, 3)).reshape(depth, POOL_HIST * bs, bw)
    state_t = jnp.transpose(state_wkv, (0, 2, 3, 4, 1))
    wkv_s = jnp.zeros_like(state_t)

    wkv_p, shift_p, pool_p, shift_s, pool_s = [], [], [], [], []
    for l in range(depth):
        prm = dict(mu=mu_shift[l][None], w0=w0[l][None], w2=w2_b[l], a0=a0[l][None], a2=a2_b[l], g2=g2_b[l],
                   k_k=k_k[l][None], k_a=k_a[l][None], r_k=r_k[l].reshape(1, aw), ln_w=ln_x_w[l][None],
                   ln_b=ln_x_b[l][None], w_pool=w_pool_b[l], pool_scale=pool_scale[l][None], seg=seg_ones)
        xp, xs = _ffn_call(xp, xs, grp, mod, l, 0, norm_g[l, 0], w_ffn_in_b, w_ffn_out_b)
        pa_p, pb_p, pa_s, pb_s = _mixin_call(xp, xs, grp, mod, l, norm_g[l, 1], w_in_a[l], w_in_b[l])
        ya_p, yb_p, s_new = _prompt_mix_call(pa_p, pb_p, bp, lp, prm)
        *vecs, yb_s = _sample_prep_call(pa_s, state_shift[l, :, 0, :], pb_s, hist_s[l], bs, ls, prm)
        ya_s, wkv_s = _sample_wkv_call(vecs, state_t, wkv_s, l, bs, ls, prm)
        xp, xs = _mixout_call(xp, xs, grp, mod, l, norm_g[l, 1], ya_p, yb_p, ya_s, yb_s, w_gate_b, b_gate, w_br_a_b,
                              w_br_b_b, w_out_b)
        xp, xs = _ffn_call(xp, xs, grp, mod, l, 2, norm_g[l, 2], w_ffn_in_b, w_ffn_out_b,
                           final_g if l == depth - 1 else None)
        wkv_p.append(s_new)
        shift_p.append(pa_p.reshape(bp, lp, a_proj)[:, -1:])
        pool_p.append(pb_p.reshape(bp, lp, bw)[:, -POOL_HIST:])
        shift_s.append(pa_s.reshape(ls, bs, a_proj)[-1][:, None, :])
        pb_bm = jnp.transpose(pb_s.reshape(ls, bs, bw), (1, 0, 2))
        pool_s.append(jnp.concatenate([state_pool[l], pb_bm], axis=1)[:, -POOL_HIST:])

    y_prompt = xp.reshape(bp, lp, d)
    y_sample = jnp.transpose(xs.reshape(ls, bs, d), (1, 0, 2))
    return (y_prompt, y_sample, jnp.stack(wkv_p), jnp.stack(shift_p), jnp.stack(pool_p),
            jnp.transpose(wkv_s, (0, 4, 1, 2, 3)), jnp.stack(shift_s), jnp.stack(pool_s))
```

```python
import functools
import math

import jax
import jax.numpy as jnp
from jax import lax
from jax.experimental import pallas as pl
from jax.experimental.pallas import tpu as pltpu

F32 = jnp.float32
BF16 = jnp.bfloat16
MIX_DTYPE = BF16

HEAD_DIM = 64
POOL_WINDOWS = (2, 4, 8, 16)
POOL_HIST = max(POOL_WINDOWS) - 1
N_SUB = 3
NORM_EPS = 1e-6
GN_EPS = 64e-5
PAST_LEN = 16384
DECAY_SCALE = math.exp(-0.5)

LANES = 128
MXU_TILE = 256
ROW_TILE = 1024
WKV_CHUNK = 64
VMEM_LIMIT_BYTES = 56 * 1024 * 1024


_RESIDENT = dict(pipeline_mode=pl.Buffered(1))


def _cparams(n_axes, **kw):
    return pltpu.CompilerParams(dimension_semantics=("arbitrary",) * n_axes,
                                vmem_limit_bytes=VMEM_LIMIT_BYTES, **kw)


def _sigmoid(x):
    return 0.5 * jnp.tanh(0.5 * x) + 0.5


def _dot(a, b):
    return jnp.dot(a.astype(BF16), b.astype(BF16), preferred_element_type=F32)


def _dot_nt(a, b):
    return lax.dot_general(a.astype(BF16), b.astype(BF16), (((1,), (1,)), ((), ())),
                           preferred_element_type=F32)


def _dot_tn(a, b):
    return lax.dot_general(a.astype(BF16), b.astype(BF16), (((0,), (0,)), ((), ())),
                           preferred_element_type=F32)


def _seg_sum(x, seg_ones):
    n, width = x.shape
    tw = seg_ones.shape[0]
    outs = []
    for c0 in range(0, width, tw):
        xc = x[:, c0:c0 + tw]
        hi = xc.astype(BF16)
        lo = (xc - hi.astype(F32)).astype(BF16)
        both = jnp.dot(jnp.concatenate([hi, lo], axis=0), seg_ones, preferred_element_type=F32)
        outs.append(both[:n] + both[n:])
    return jnp.concatenate(outs, axis=1)


def _rms(x):
    return x * lax.rsqrt(jnp.mean(x * x, axis=-1, keepdims=True) + NORM_EPS)


SUBLANES = 8


def _rows(ref, tm, seq_row):
    if seq_row is not None:
        return ref[pl.ds(seq_row, 1), :]
    m = ref[...]
    return m if m.shape[0] == tm else jnp.concatenate([m] * (tm // m.shape[0]), axis=0)


def _modnorm(x, g, shift, scale):
    return _rms(x) * g * (1.0 + scale) + shift


def _mod_kernel(c_ref, w_ref, b_ref, o_ref):
    c = c_ref[...]
    o_ref[...] = _dot(c * _sigmoid(c), w_ref[...]) + b_ref[...]


def _mod_call(c_all, w_mod, b_mod):
    depth, d, n9 = w_mod.shape
    rows = c_all.shape[0]
    n_sub9 = n9 // d
    tn = d
    return pl.pallas_call(
        _mod_kernel,
        grid=(depth, n_sub9),
        in_specs=[pl.BlockSpec((rows, d), lambda l, j: (0, 0)),
                  pl.BlockSpec((None, d, tn), lambda l, j: (l, 0, j)),
                  pl.BlockSpec((None, 1, tn), lambda l, j: (l, 0, j))],
        out_specs=pl.BlockSpec((None, None, rows, tn), lambda l, j: (l, j, 0, 0)),
        out_shape=jax.ShapeDtypeStruct((depth, n_sub9, rows, d), F32),
        compiler_params=_cparams(2),
        name="mod",
    )(c_all, w_mod, b_mod.reshape(depth, 1, n9))


class _Groups:
    def __init__(self, bp, lp, bs, ls, tm, mod_row0):
        assert lp % tm == 0
        self.tm, self.ts, self.bs = tm, bs * ls, bs
        self.n_prompt = bp * lp // tm
        self.seq_rows = (mod_row0, lp // tm)

    def _ptile(self, i):
        return jnp.minimum(i, self.n_prompt - 1)

    def prompt(self, width):
        return pl.BlockSpec((self.tm, width), lambda i: (self._ptile(i), 0))

    def sample(self, width, **kw):
        return pl.BlockSpec((self.ts, width), lambda i: (0, 0), **kw)

    def prompt_mod(self, d, layer, idx):
        row0, per_seq = self.seq_rows
        return pl.BlockSpec((None, None, SUBLANES, d),
                            lambda i: (layer, idx, (row0 + self._ptile(i) // per_seq) // SUBLANES, 0))

    def sample_mod(self, d, layer, idx):
        return pl.BlockSpec((None, None, self.bs, d), lambda i: (layer, idx, 0, 0), **_RESIDENT)

    def mods(self, d, layer, idxs):
        return ([self.prompt_mod(d, layer, k) for k in idxs] + [self.sample_mod(d, layer, k) for k in idxs])

    def run(self, body, prompt_refs, sample_refs):
        i = pl.program_id(0)
        row0, per_seq = self.seq_rows
        seq_row = (row0 + i // per_seq) % SUBLANES
        pl.when(i < self.n_prompt)(lambda: body(prompt_refs, seq_row))
        pl.when(i == self.n_prompt)(lambda: body(sample_refs, None))


FFN_CHUNK = 2 * MXU_TILE


def _ffn_kernel(xp_ref, xs_ref, psh, psc, pgt, ssh, ssc, sgt, g_ref, win_ref, wo_ref, *rest, final, grp):
    if final:
        fg_ref, yp_ref, ys_ref, acc_sc = rest
    else:
        yp_ref, ys_ref, acc_sc = rest
    d_ff = wo_ref.shape[0]

    def body(refs, seq_row):
        x_ref, o_ref, sh_ref, sc_ref, gt_ref = refs
        tm = x_ref.shape[0]
        x = x_ref[...]
        u = _modnorm(x, g_ref[...], _rows(sh_ref, tm, seq_row), _rows(sc_ref, tm, seq_row)).astype(BF16)
        for c0 in range(0, d_ff, FFN_CHUNK):
            c1 = min(c0 + FFN_CHUNK, d_ff)
            hg = jnp.dot(u, win_ref[:, c0:c1], preferred_element_type=F32)
            hu = jnp.dot(u, win_ref[:, d_ff + c0:d_ff + c1], preferred_element_type=F32)
            act = (hg * _sigmoid(hg) * hu).astype(BF16)
            part = jnp.dot(act, wo_ref[c0:c1, :], preferred_element_type=F32)
            if c0 == 0:
                acc_sc[0:tm, :] = part
            else:
                acc_sc[0:tm, :] += part
        y = x + 0.5 * _rows(gt_ref, tm, seq_row) * acc_sc[0:tm, :]
        if final:
            y = _rms(y) * fg_ref[...]
        o_ref[...] = y

    grp.run(body, (xp_ref, yp_ref, psh, psc, pgt), (xs_ref, ys_ref, ssh, ssc, sgt))


def _ffn_call(xp, xs, grp, mod, layer, sub, norm_g, w_in, w_out, final_g=None):
    d = xp.shape[1]
    d_ff = w_out.shape[2]
    assert d_ff % LANES == 0 and FFN_CHUNK % LANES == 0 and grp.ts <= grp.tm
    final = final_g is not None
    const = lambda i: (0, 0)
    half = lambda i: (layer, sub // 2, 0, 0)
    in_specs = ([grp.prompt(d), grp.sample(d, **_RESIDENT)] + grp.mods(d, layer, (3 * sub, 3 * sub + 1, 3 * sub + 2))
                + [pl.BlockSpec((1, d), const),
                   pl.BlockSpec((None, None, d, 2 * d_ff), half, **_RESIDENT),
                   pl.BlockSpec((None, None, d_ff, d), half, **_RESIDENT)])
    args = [xp, xs] + [mod] * 6 + [norm_g.reshape(1, d), w_in, w_out]
    if final:
        in_specs.append(pl.BlockSpec((1, d), const))
        args.append(final_g.reshape(1, d))
    return pl.pallas_call(
        functools.partial(_ffn_kernel, final=final, grp=grp),
        grid=(grp.n_prompt + 1,),
        in_specs=in_specs,
        out_specs=[grp.prompt(d), grp.sample(d)],
        out_shape=[jax.ShapeDtypeStruct(xp.shape, F32), jax.ShapeDtypeStruct(xs.shape, F32)],
        scratch_shapes=[pltpu.VMEM((grp.tm, d), F32)],
        compiler_params=_cparams(1),
        name="ffn",
    )(*args)


def _mixin_kernel(xp_ref, xs_ref, psh, psc, ssh, ssc, g_ref, wa_ref, wb_ref, pap_ref, pbp_ref, pas_ref, pbs_ref, *, grp):
    def body(refs, seq_row):
        x_ref, pa_ref, pb_ref, sh_ref, sc_ref = refs
        tm = x_ref.shape[0]
        u = _modnorm(x_ref[...], g_ref[...], _rows(sh_ref, tm, seq_row), _rows(sc_ref, tm, seq_row)).astype(BF16)
        pa_ref[...] = jnp.dot(u, wa_ref[...], preferred_element_type=F32)
        pb_ref[...] = jnp.dot(u, wb_ref[...], preferred_element_type=F32)

    grp.run(body, (xp_ref, pap_ref, pbp_ref, psh, psc), (xs_ref, pas_ref, pbs_ref, ssh, ssc))


def _mixin_call(xp, xs, grp, mod, layer, norm_g, w_a, w_b):
    d = xp.shape[1]
    na, nb = w_a.shape[1], w_b.shape[1]
    const = lambda i: (0, 0)
    shape = lambda x, n: jax.ShapeDtypeStruct((x.shape[0], n), F32)
    return pl.pallas_call(
        functools.partial(_mixin_kernel, grp=grp),
        grid=(grp.n_prompt + 1,),
        in_specs=([grp.prompt(d), grp.sample(d, **_RESIDENT)] + grp.mods(d, layer, (3, 4))
                  + [pl.BlockSpec((1, d), const), pl.BlockSpec((d, na), const, **_RESIDENT),
                     pl.BlockSpec((d, nb), const, **_RESIDENT)]),
        out_specs=[grp.prompt(na), grp.prompt(nb), grp.sample(na), grp.sample(nb)],
        out_shape=[shape(xp, na), shape(xp, nb), shape(xs, na), shape(xs, nb)],
        compiler_params=_cparams(1),
        name="mixin",
    )(xp, xs, mod, mod, mod, mod, norm_g.reshape(1, d), w_a, w_b)


def _mixout_kernel(xp_ref, xs_ref, yap_ref, ybp_ref, yas_ref, ybs_ref, psh, psc, pgt, ssh, ssc, sgt, g_ref,
                   wgate_ref, bgate_ref, wbra_ref, wbrb_ref, wout_ref, op_ref, os_ref, *, grp):
    def body(refs, seq_row):
        x_ref, ya_ref, yb_ref, o_ref, sh_ref, sc_ref, gt_ref = refs
        tm, d = x_ref.shape
        x = x_ref[...]
        u = _modnorm(x, g_ref[...], _rows(sh_ref, tm, seq_row), _rows(sc_ref, tm, seq_row)).astype(BF16)
        gates = _sigmoid(jnp.dot(u, wgate_ref[...], preferred_element_type=F32) + bgate_ref[...])
        ma = _dot(ya_ref[...], wbra_ref[...])
        mb = _dot(yb_ref[...], wbrb_ref[...])
        merged = gates[:, :d] * ma + gates[:, d:] * mb
        o_ref[...] = x + _rows(gt_ref, tm, seq_row) * _dot(merged, wout_ref[...])

    grp.run(body, (xp_ref, yap_ref, ybp_ref, op_ref, psh, psc, pgt), (xs_ref, yas_ref, ybs_ref, os_ref, ssh, ssc, sgt))


def _mixout_call(xp, xs, grp, mod, layer, norm_g, ya_p, yb_p, ya_s, yb_s, w_gate, b_gate, w_br_a, w_br_b, w_out):
    d = xp.shape[1]
    wa, wb = ya_p.shape[1], yb_p.shape[1]
    const = lambda i: (0, 0)
    lyr = lambda i: (layer, 0, 0)
    return pl.pallas_call(
        functools.partial(_mixout_kernel, grp=grp),
        grid=(grp.n_prompt + 1,),
        in_specs=([grp.prompt(d), grp.sample(d, **_RESIDENT), grp.prompt(wa), grp.prompt(wb),
                   grp.sample(wa, **_RESIDENT), grp.sample(wb, **_RESIDENT)] + grp.mods(d, layer, (3, 4, 5))
                  + [pl.BlockSpec((1, d), const),
                     pl.BlockSpec((None, d, 2 * d), lyr, **_RESIDENT), pl.BlockSpec((None, 1, 2 * d), lyr),
                     pl.BlockSpec((None, wa, d), lyr, **_RESIDENT), pl.BlockSpec((None, wb, d), lyr, **_RESIDENT),
                     pl.BlockSpec((None, d, d), lyr, **_RESIDENT)]),
        out_specs=[grp.prompt(d), grp.sample(d)],
        out_shape=[jax.ShapeDtypeStruct(xp.shape, F32), jax.ShapeDtypeStruct(xs.shape, F32)],
        compiler_params=_cparams(1),
        name="mixout",
    )(xp, xs, ya_p, yb_p, ya_s, yb_s, *([mod] * 6), norm_g.reshape(1, d), w_gate, b_gate.reshape(-1, 1, 2 * d),
      w_br_a, w_br_b, w_out)


def _rwkv_prep(xs, aw, seg_ones, w0, w2, a0, a2, g2, k_k, k_a):
    n_dec, n_aaa = w2.shape[0], a2.shape[0]
    r = xs[:, 0:aw]
    k = xs[:, aw:2 * aw]
    v = xs[:, 2 * aw:3 * aw]
    o = 3 * aw
    wl = xs[:, o:o + n_dec]
    al = xs[:, o + n_dec:o + n_dec + n_aaa]
    gl = xs[:, o + n_dec + n_aaa:]
    logw = -DECAY_SCALE * _sigmoid(w0 + _dot(jnp.tanh(wl), w2))
    asig = _sigmoid(a0 + _dot(al, a2))
    g = _dot(_sigmoid(gl), g2)
    kk = k * k_k
    kk = kk * lax.rsqrt(jnp.maximum(_seg_sum(kk * kk, seg_ones), 1e-24))
    k = k * (1.0 + (asig - 1.0) * k_a)
    return r, k, v, logw, -kk, kk * asig, g


def _pool_group_sum(read_rows, w):
    s = read_rows(0)
    for j in range(1, w):
        s = s + read_rows(j)
    return s


PROMPT_BLOCK = 64
PROMPT_SEQS = 8
SHIFT_PAD = 8
POOL_PAD = 24


def _pool_window_sums(xbuf, e2, e4, e8, s, tb):
    gw = xbuf.shape[2] // len(POOL_WINDOWS)
    lo, hi = POOL_PAD - 16, POOL_PAD + tb
    e2[s, lo:hi, :] = xbuf[s, lo:hi, :] + xbuf[s, lo - 1:hi - 1, :]
    e4[s, lo:hi, :] = e2[s, lo:hi, gw:] + e2[s, lo - 2:hi - 2, gw:]
    e8[s, lo:hi, :] = e4[s, lo:hi, gw:] + e4[s, lo - 4:hi - 4, gw:]
    blk = slice(POOL_PAD, POOL_PAD + tb)
    back8 = slice(POOL_PAD - 8, POOL_PAD - 8 + tb)
    return (e2[s, blk, 0:gw], e4[s, blk, 0:gw], e8[s, blk, 0:gw], e8[s, blk, gw:] + e8[s, back8, gw:])


def _prompt_mix_kernel(pa_ref, pb_ref, mu_ref, w0_ref, w2_ref, a0_ref, a2_ref, g2_ref, kk_ref, ka_ref, rk_ref,
                       lnw_ref, lnb_ref, wpool_ref, pscale_ref, seg_ref,
                       ya_ref, yb_ref, wkv_ref,
                       pbuf, xbuf, e2, e4, e8, s_sc, r_sc, k_sc, v_sc, a_sc, b_sc, lw_sc, g_sc, y_sc, *, n_blocks):
    ns, tb, aw = ya_ref.shape
    bw = yb_ref.shape[2]
    t = WKV_CHUNK
    n_pairs = aw // LANES
    blk = pl.program_id(1)
    seqs = range(ns)

    @pl.when(blk == 0)
    def _():
        pbuf[:, 0:SHIFT_PAD, :] = jnp.zeros((ns, SHIFT_PAD, pbuf.shape[2]), F32)
        xbuf[:, 0:POOL_PAD, :] = jnp.zeros((ns, POOL_PAD, bw), F32)
        e2[:, 0:POOL_PAD, :] = jnp.zeros((ns, POOL_PAD, e2.shape[2]), F32)
        e4[:, 0:POOL_PAD, :] = jnp.zeros((ns, POOL_PAD, e4.shape[2]), F32)
        s_sc[...] = jnp.zeros_like(s_sc)

    xs = []
    for s in seqs:
        pa = pa_ref[s]
        pbuf[s, SHIFT_PAD:SHIFT_PAD + tb, :] = pa
        prev = pbuf[s, SHIFT_PAD - 1:SHIFT_PAD - 1 + tb, :]
        xs.append(pa + (prev - pa) * mu_ref[...])
        pbuf[s, 0:SHIFT_PAD, :] = pbuf[s, tb:tb + SHIFT_PAD, :]
    xs = jnp.concatenate(xs, axis=0)

    seg_ones = seg_ref[...]
    r, k, v, logw, a, b, g = _rwkv_prep(xs, aw, seg_ones, w0_ref[...], w2_ref[...], a0_ref[...], a2_ref[...],
                                        g2_ref[...], kk_ref[...], ka_ref[...])
    r_sc[...] = r
    k_sc[...] = k
    v_sc[...] = v
    a_sc[...] = a
    b_sc[...] = b
    lw_sc[...] = logw
    g_sc[...] = g

    t2 = 2 * t
    ri = lax.broadcasted_iota(jnp.int32, (t, t), 0)
    ci = lax.broadcasted_iota(jnp.int32, (t, t), 1)
    tri_incl = (ri >= ci).astype(BF16)
    rs = lax.broadcasted_iota(jnp.int32, (t2, t2), 0)
    cs = lax.broadcasted_iota(jnp.int32, (t2, t2), 1)
    strict = cs < rs
    rs2 = lax.broadcasted_iota(jnp.int32, (t2, 2 * t2), 0)
    cs2 = lax.broadcasted_iota(jnp.int32, (t2, 2 * t2), 1)
    incl2 = jnp.where(cs2 >= t2, cs2 - t2, cs2) <= rs2
    lane = lax.broadcasted_iota(jnp.int32, (1, LANES), 1)
    head_mask = (lane < HEAD_DIM, lane >= HEAD_DIM)
    units = [(s, p) for s in seqs for p in range(n_pairs)]
    n_levels = int(math.log2(t))

    def stack(x):
        return jnp.concatenate([jnp.where(head_mask[0], x, 0.0), jnp.where(head_mask[1], x, 0.0)], axis=0)

    def chunk(ci_, carry):
        at, rt, bt, kt, vv, p_end, sls = [], [], [], [], [], [], []
        for s in seqs:
            sl = pl.ds(pl.multiple_of(s * tb + ci_ * t, t), t)
            lw = lw_sc[sl, :]
            hi = lw.astype(BF16)
            lo = (lw - hi.astype(F32)).astype(BF16)
            c = (jnp.dot(tri_incl, hi, preferred_element_type=F32)
                 + jnp.dot(tri_incl, lo, preferred_element_type=F32))
            e_in = jnp.exp(c)
            e_neg = jnp.exp(-c)
            p_end.append(e_in[t - 1:t, :])
            at.append(a_sc[sl, :] * jnp.exp(c - lw))
            rt.append(r_sc[sl, :] * e_in)
            bt.append(b_sc[sl, :] * e_neg)
            kt.append(k_sc[sl, :] * e_neg)
            vv.append(v_sc[sl, :])
            sls.append(sl)
        cols = [slice(p * LANES, (p + 1) * LANES) for p in range(n_pairs)]
        un = range(len(units))
        pe = [p_end[s][:, cols[p]] for s, p in units]
        a2 = [stack(at[s][:, cols[p]]).astype(BF16) for s, p in units]
        r2 = [stack(rt[s][:, cols[p]]).astype(BF16) for s, p in units]
        v2b = [stack(vv[s][:, cols[p]]).astype(BF16) for s, p in units]
        bk = [jnp.concatenate([stack(bt[s][:, cols[p]]), stack(kt[s][:, cols[p]])], axis=0) for s, p in units]
        g_ = [_dot_nt(jnp.concatenate([a2[i], r2[i]], axis=0), bk[i]) for i in un]
        n = [jnp.where(strict, g_[i][:t2, :t2], 0.0) for i in un]
        ak = [jnp.where(strict, g_[i][:t2, t2:], 0.0).astype(BF16) for i in un]
        rbk = [jnp.where(incl2, g_[i][t2:, :], 0.0).astype(BF16) for i in un]
        xm = n
        pw = [_dot(n[i], n[i]) for i in un]
        for _ in range(n_levels - 2):
            res = [_dot(jnp.concatenate([pw[i], xm[i]], axis=0), pw[i]) for i in un]
            xm = [xm[i] + pw[i] + res[i][t2:] for i in un]
            pw = [res[i][:t2] for i in un]
        res = [_dot(xm[i], pw[i]) for i in un]
        xm = [(xm[i] + pw[i] + res[i]).astype(BF16) for i in un]
        s_old = [s_sc[s * n_pairs + p] for s, p in units]
        s_bf = [s_old[i].astype(BF16) for i in un]
        rhs = [_dot_nt(a2[i], s_bf[i]) for i in un]
        akv = [jnp.dot(ak[i], v2b[i], preferred_element_type=F32) for i in un]
        rhs = [rhs[i] + akv[i] for i in un]
        u = [rhs[i] + _dot(xm[i], rhs[i]) for i in un]
        uv = [jnp.concatenate([u[i].astype(BF16), v2b[i]], axis=0) for i in un]
        ys = [_dot_nt(r2[i], s_bf[i]) for i in un]
        yi = [jnp.dot(rbk[i], uv[i], preferred_element_type=F32) for i in un]
        sn = [_dot_tn(uv[i], bk[i] * pe[i]) for i in un]
        for i, (s, p) in enumerate(units):
            y2 = ys[i] + yi[i]
            y_sc[sls[s], cols[p]] = y2[:t] + y2[t:]
            s_sc[s * n_pairs + p] = s_old[i] * pe[i] + sn[i]
        return carry

    lax.fori_loop(0, tb // t, chunk, 0)

    y = y_sc[...]
    inv_n = 1.0 / HEAD_DIM
    mean = _seg_sum(y, seg_ones) * inv_n
    yc = y - mean
    var = _seg_sum(yc * yc, seg_ones) * inv_n
    yn = yc * lax.rsqrt(var + GN_EPS) * lnw_ref[...] + lnb_ref[...]
    bonus = _seg_sum(r_sc[...] * k_sc[...] * rk_ref[...], seg_ones) * v_sc[...]
    ya = (yn + bonus) * g_sc[...]
    for s in seqs:
        ya_ref[s] = ya[s * tb:(s + 1) * tb].astype(ya_ref.dtype)

    pos = blk * tb + lax.broadcasted_iota(jnp.int32, (tb, 1), 0)
    gw = bw // len(POOL_WINDOWS)
    for s in seqs:
        pb = pb_ref[s]
        xbuf[s, POOL_PAD:POOL_PAD + tb, :] = pb
        sums = _pool_window_sums(xbuf, e2, e4, e8, s, tb)
        for gi, w in enumerate(POOL_WINDOWS):
            cols = slice(gi * gw, (gi + 1) * gw)
            inv_cnt = 1.0 / jnp.minimum(w, pos + 1).astype(F32)
            dlt = sums[gi] * inv_cnt - pb[:, cols]
            yb_ref[s, :, cols] = (_dot(dlt, wpool_ref[gi]) * pscale_ref[:, cols]).astype(yb_ref.dtype)
        xbuf[s, POOL_PAD - 16:POOL_PAD, :] = xbuf[s, tb + POOL_PAD - 16:tb + POOL_PAD, :]

    @pl.when(blk == n_blocks - 1)
    def _():
        for s in seqs:
            for p in range(n_pairs):
                s_p = s_sc[s * n_pairs + p]
                for h in range(2):
                    o = h * HEAD_DIM
                    wkv_ref[s, 2 * p + h] = s_p[o:o + HEAD_DIM, o:o + HEAD_DIM]


def _prompt_mix_call(pa, pb, batch, seq, prm):
    na = pa.shape[1]
    bw = pb.shape[1]
    aw = prm["w0"].shape[1]
    n_heads = aw // HEAD_DIM
    tb, ns = min(PROMPT_BLOCK, seq), PROMPT_SEQS
    assert seq % tb == 0 and tb % WKV_CHUNK == 0 and aw % LANES == 0 and batch % ns == 0
    assert len(POOL_WINDOWS) == 4 and POOL_HIST < 16 <= POOL_PAD - 8
    n_blocks = seq // tb
    gw = bw // len(POOL_WINDOWS)
    row = lambda b, c: (b, c, 0)
    const2 = lambda b, c: (0, 0)
    const3 = lambda b, c: (0, 0, 0)
    full = lambda a: pl.BlockSpec(a.shape, const2 if a.ndim == 2 else const3)
    names = ("mu", "w0", "w2", "a0", "a2", "g2", "k_k", "k_a", "r_k", "ln_w", "ln_b", "w_pool", "pool_scale", "seg")
    params = [prm[n] for n in names]
    blk = lambda: pltpu.VMEM((ns * tb, aw), F32)
    ya, yb, wkv = pl.pallas_call(
        functools.partial(_prompt_mix_kernel, n_blocks=n_blocks),
        grid=(batch // ns, n_blocks),
        in_specs=[pl.BlockSpec((ns, tb, na), row), pl.BlockSpec((ns, tb, bw), row)] + [full(a) for a in params],
        out_specs=[pl.BlockSpec((ns, tb, aw), row), pl.BlockSpec((ns, tb, bw), row),
                   pl.BlockSpec((ns, n_heads, HEAD_DIM, HEAD_DIM), lambda b, c: (b, 0, 0, 0))],
        out_shape=[jax.ShapeDtypeStruct((batch, seq, aw), MIX_DTYPE), jax.ShapeDtypeStruct((batch, seq, bw), MIX_DTYPE),
                   jax.ShapeDtypeStruct((batch, n_heads, HEAD_DIM, HEAD_DIM), F32)],
        scratch_shapes=[pltpu.VMEM((ns, tb + SHIFT_PAD, na), F32), pltpu.VMEM((ns, tb + POOL_PAD, bw), F32),
                        pltpu.VMEM((ns, tb + POOL_PAD, bw), F32), pltpu.VMEM((ns, tb + POOL_PAD, bw - gw), F32),
                        pltpu.VMEM((ns, tb + POOL_PAD, bw - 2 * gw), F32),
                        pltpu.VMEM((ns * (aw // LANES), LANES, LANES), F32),
                        blk(), blk(), blk(), blk(), blk(), blk(), blk(), blk()],
        compiler_params=_cparams(2),
        name="mix_prompt",
    )(pa.reshape(batch, seq, na), pb.reshape(batch, seq, bw), *params)
    return ya.reshape(batch * seq, aw), yb.reshape(batch * seq, bw), wkv


def _sample_prep_kernel(pa_ref, sh0_ref, pb_ref, hist_ref, mu_ref, w0_ref, w2_ref, a0_ref, a2_ref, g2_ref,
                        kk_ref, ka_ref, wpool_ref, pscale_ref, seg_ref,
                        r_ref, w_ref, k_ref, v_ref, a_ref, b_ref, g_ref, yb_ref, *, batch, seq):
    aw = r_ref.shape[1]
    bw = yb_ref.shape[1]
    pa = pa_ref[...]
    prev = jnp.concatenate([sh0_ref[...], pa[:(seq - 1) * batch]], axis=0)
    xs = pa + (prev - pa) * mu_ref[...]
    r, k, v, logw, a, b, g = _rwkv_prep(xs, aw, seg_ref[...], w0_ref[...], w2_ref[...], a0_ref[...], a2_ref[...],
                                        g2_ref[...], kk_ref[...], ka_ref[...])
    outs = ((r_ref, r), (w_ref, jnp.exp(logw)), (k_ref, k), (v_ref, v), (a_ref, a), (b_ref, b), (g_ref, g))
    for ref, val in outs:
        for ti in range(seq):
            ref[ti] = val[ti * batch:(ti + 1) * batch, :].T

    gw = bw // len(POOL_WINDOWS)

    def slab(i, cols):
        if i < POOL_HIST:
            return hist_ref[i * batch:(i + 1) * batch, cols]
        return pb_ref[(i - POOL_HIST) * batch:(i - POOL_HIST + 1) * batch, cols]

    for ti in range(seq):
        rows = slice(ti * batch, (ti + 1) * batch)
        for gi, w in enumerate(POOL_WINDOWS):
            cols = slice(gi * gw, (gi + 1) * gw)
            s = _pool_group_sum(lambda j: slab(POOL_HIST + ti - j, cols), w)
            cnt = float(min(w, PAST_LEN + ti + 1))
            dlt = s / cnt - pb_ref[rows, cols]
            yb_ref[rows, cols] = (_dot(dlt, wpool_ref[gi]) * pscale_ref[:, cols]).astype(yb_ref.dtype)


def _sample_prep_call(pa, sh0, pb, hist, batch, seq, prm):
    rows = pa.shape[0]
    aw = prm["w0"].shape[1]
    bw = pb.shape[1]
    names = ("mu", "w0", "w2", "a0", "a2", "g2", "k_k", "k_a", "w_pool", "pool_scale", "seg")
    args = [pa, sh0, pb, hist] + [prm[n] for n in names]
    full = lambda a: pl.BlockSpec(a.shape, (lambda i: (0, 0)) if a.ndim == 2 else (lambda i: (0, 0, 0)))
    o = jax.ShapeDtypeStruct((seq, aw, batch), F32)
    ob = jax.ShapeDtypeStruct((rows, bw), MIX_DTYPE)
    return pl.pallas_call(
        functools.partial(_sample_prep_kernel, batch=batch, seq=seq),
        grid=(1,),
        in_specs=[full(a) for a in args],
        out_specs=[full(o)] * 7 + [full(ob)],
        out_shape=[o] * 7 + [ob],
        compiler_params=_cparams(1),
        name="prep_sample",
    )(*args)


SAMPLE_VROWS = 64


def _sample_wkv_kernel(r_ref, w_ref, k_ref, v_ref, a_ref, b_ref, g_ref, rk_ref, lnw_ref, lnb_ref, s_ref, all_ref,
                       y_ref, so_ref, y_sc):
    del all_ref
    seq, hd, batch = r_ref.shape
    vb = s_ref.shape[0]
    h, vs = pl.program_id(0), pl.program_id(1)
    hrow = pl.multiple_of((h % 2) * hd, hd)
    for q in range(vb):
        tile = s_ref[q]
        for ti in range(seq):
            sa = jnp.sum(tile * a_ref[ti], axis=0, keepdims=True)
            tile = tile * w_ref[ti] + sa * b_ref[ti] + v_ref[ti, pl.ds(vs * vb + q, 1), :] * k_ref[ti]
            y_sc[ti, pl.ds(hrow + vs * vb + q, 1), :] = jnp.sum(tile * r_ref[ti], axis=0, keepdims=True)
        so_ref[q] = tile

    @pl.when(vs == pl.num_programs(1) - 1)
    def _():
        for ti in range(seq):
            rows = pl.ds(hrow, hd)
            y = y_sc[ti, rows, :]
            yc = y - jnp.mean(y, axis=0, keepdims=True)
            var = jnp.mean(yc * yc, axis=0, keepdims=True)
            yn = yc * lax.rsqrt(var + GN_EPS) * lnw_ref[...] + lnb_ref[...]
            bonus = jnp.sum(r_ref[ti] * k_ref[ti] * rk_ref[...], axis=0, keepdims=True) * v_ref[ti]
            y_sc[ti, rows, :] = (yn + bonus) * g_ref[ti]

    @pl.when((vs == pl.num_programs(1) - 1) & (h % 2 == 1))
    def _():
        for ti in range(seq):
            y_ref[ti * batch:(ti + 1) * batch, :] = y_sc[ti].T.astype(y_ref.dtype)


def _sample_wkv_call(vecs, state_t, new_states, layer, batch, seq, prm):
    n_heads, hd = state_t.shape[1:3]
    aw = n_heads * hd
    vb = SAMPLE_VROWS
    assert 2 * hd == LANES and hd % vb == 0
    vspec = pl.BlockSpec((seq, hd, batch), lambda h, s: (0, h, 0))
    cspec = pl.BlockSpec((hd, 1), lambda h, s: (h, 0))
    sspec = pl.BlockSpec((None, None, vb, hd, batch), lambda h, s: (layer, h, s, 0, 0))
    cols = [prm[n].reshape(aw, 1) for n in ("r_k", "ln_w", "ln_b")]
    args = [*vecs, *cols, state_t, new_states]
    return pl.pallas_call(
        _sample_wkv_kernel,
        grid=(n_heads, hd // vb),
        in_specs=[vspec] * 7 + [cspec] * 3 + [sspec, pl.BlockSpec(memory_space=pl.ANY)],
        out_specs=[pl.BlockSpec((seq * batch, LANES), lambda h, s: (0, h // 2)), sspec],
        out_shape=[jax.ShapeDtypeStruct((seq * batch, aw), MIX_DTYPE), jax.ShapeDtypeStruct(state_t.shape, F32)],
        scratch_shapes=[pltpu.VMEM((seq, LANES, batch), F32)],
        input_output_aliases={len(args) - 1: 1},
        compiler_params=_cparams(2),
        name="wkv_sample",
    )(*args)


def kernel(x_prompt, x_sample, state_wkv, state_shift, state_pool, c_prompt, c_sample, norm_g, w_mod, b_mod, w_ffn_in, w_ffn_out, w_in, mu_shift, w0, w2, a0, a2, g2, k_k, k_a, r_k, ln_x_w, ln_x_b, w_pool, pool_scale, w_br_a, w_br_b, w_gate, b_gate, w_out, final_g):
    bp, lp, d = x_prompt.shape
    bs, ls, _ = x_sample.shape
    depth = w_mod.shape[0]
    aw = w0.shape[1]
    a_proj = mu_shift.shape[1]
    bw = pool_scale.shape[1]

    bf = lambda w: w.astype(BF16)
    w_ffn_in_b, w_ffn_out_b = bf(w_ffn_in), bf(w_ffn_out)
    w_in_a, w_in_b = bf(w_in[:, :, :a_proj]), bf(w_in[:, :, a_proj:])
    w_gate_b, w_br_a_b, w_br_b_b, w_out_b = bf(w_gate), bf(w_br_a), bf(w_br_b), bf(w_out)
    w2_b, a2_b, g2_b, w_pool_b = bf(w2), bf(a2), bf(g2), bf(w_pool)
    head_of = jnp.arange(MXU_TILE, dtype=jnp.int32) // HEAD_DIM
    seg_ones = (head_of[:, None] == head_of[None, :]).astype(BF16)

    mod = _mod_call(jnp.concatenate([c_sample, c_prompt], axis=0), w_mod, b_mod)
    grp = _Groups(bp, lp, bs, ls, min(ROW_TILE, lp), mod_row0=bs)

    xp = x_prompt.reshape(bp * lp, d)
    xs = jnp.transpose(x_sample, (1, 0, 2)).reshape(ls * bs, d)
    hist_s = jnp.transpose(state_pool, (0, 2, 1, 3)).reshape(depth, POOL_HIST * bs, bw)
    state_t = jnp.transpose(state_wkv, (0, 2, 3, 4, 1))
    wkv_s = jnp.zeros_like(state_t)

    wkv_p, shift_p, pool_p, shift_s, pool_s = [], [], [], [], []
    for l in range(depth):
        prm = dict(mu=mu_shift[l][None], w0=w0[l][None], w2=w2_b[l], a0=a0[l][None], a2=a2_b[l], g2=g2_b[l],
                   k_k=k_k[l][None], k_a=k_a[l][None], r_k=r_k[l].reshape(1, aw), ln_w=ln_x_w[l][None],
                   ln_b=ln_x_b[l][None], w_pool=w_pool_b[l], pool_scale=pool_scale[l][None], seg=seg_ones)
        xp, xs = _ffn_call(xp, xs, grp, mod, l, 0, norm_g[l, 0], w_ffn_in_b, w_ffn_out_b)
        pa_p, pb_p, pa_s, pb_s = _mixin_call(xp, xs, grp, mod, l, norm_g[l, 1], w_in_a[l], w_in_b[l])
        ya_p, yb_p, s_new = _prompt_mix_call(pa_p, pb_p, bp, lp, prm)
        *vecs, yb_s = _sample_prep_call(pa_s, state_shift[l, :, 0, :], pb_s, hist_s[l], bs, ls, prm)
        ya_s, wkv_s = _sample_wkv_call(vecs, state_t, wkv_s, l, bs, ls, prm)
        xp, xs = _mixout_call(xp, xs, grp, mod, l, norm_g[l, 1], ya_p, yb_p, ya_s, yb_s, w_gate_b, b_gate, w_br_a_b,
                              w_br_b_b, w_out_b)
        xp, xs = _ffn_call(xp, xs, grp, mod, l, 2, norm_g[l, 2], w_ffn_in_b, w_ffn_out_b,
                           final_g if l == depth - 1 else None)
        wkv_p.append(s_new)
        shift_p.append(pa_p.reshape(bp, lp, a_proj)[:, -1:])
        pool_p.append(pb_p.reshape(bp, lp, bw)[:, -POOL_HIST:])
        shift_s.append(pa_s.reshape(ls, bs, a_proj)[-1][:, None, :])
        pb_bm = jnp.transpose(pb_s.reshape(ls, bs, bw), (1, 0, 2))
        pool_s.append(jnp.concatenate([state_pool[l], pb_bm], axis=1)[:, -POOL_HIST:])

    y_prompt = xp.reshape(bp, lp, d)
    y_sample = jnp.transpose(xs.reshape(ls, bs, d), (1, 0, 2))
    return (y_prompt, y_sample, jnp.stack(wkv_p), jnp.stack(shift_p), jnp.stack(pool_p),
            jnp.transpose(wkv_s, (0, 4, 1, 2, 3)), jnp.stack(shift_s), jnp.stack(pool_s))
```

```python
import functools
import math

import jax
import jax.numpy as jnp
from jax import lax
from jax.experimental import pallas as pl
from jax.experimental.pallas import tpu as pltpu

F32 = jnp.float32
BF16 = jnp.bfloat16
MIX_DTYPE = BF16

HEAD_DIM = 64
POOL_WINDOWS = (2, 4, 8, 16)
POOL_HIST = max(POOL_WINDOWS) - 1
N_SUB = 3
NORM_EPS = 1e-6
GN_EPS = 64e-5
PAST_LEN = 16384
DECAY_SCALE = math.exp(-0.5)

LANES = 128
MXU_TILE = 256
ROW_TILE = 1024
WKV_CHUNK = 64
VMEM_LIMIT_BYTES = 56 * 1024 * 1024


_RESIDENT = dict(pipeline_mode=pl.Buffered(1))


def _cparams(n_axes, **kw):
    return pltpu.CompilerParams(dimension_semantics=("arbitrary",) * n_axes,
                                vmem_limit_bytes=VMEM_LIMIT_BYTES, **kw)


def _sigmoid(x):
    return 0.5 * jnp.tanh(0.5 * x) + 0.5


def _dot(a, b):
    return jnp.dot(a.astype(BF16), b.astype(BF16), preferred_element_type=F32)


def _dot_nt(a, b):
    return lax.dot_general(a.astype(BF16), b.astype(BF16), (((1,), (1,)), ((), ())),
                           preferred_element_type=F32)


def _dot_tn(a, b):
    return lax.dot_general(a.astype(BF16), b.astype(BF16), (((0,), (0,)), ((), ())),
                           preferred_element_type=F32)


def _seg_sum(x, seg_ones):
    n, width = x.shape
    tw = seg_ones.shape[0]
    outs = []
    for c0 in range(0, width, tw):
        xc = x[:, c0:c0 + tw]
        hi = xc.astype(BF16)
        lo = (xc - hi.astype(F32)).astype(BF16)
        both = jnp.dot(jnp.concatenate([hi, lo], axis=0), seg_ones, preferred_element_type=F32)
        outs.append(both[:n] + both[n:])
    return jnp.concatenate(outs, axis=1)


def _rms(x):
    return x * lax.rsqrt(jnp.mean(x * x, axis=-1, keepdims=True) + NORM_EPS)


SUBLANES = 8


def _rows(ref, tm, seq_row):
    if seq_row is not None:
        return ref[pl.ds(seq_row, 1), :]
    m = ref[...]
    return m if m.shape[0] == tm else jnp.concatenate([m] * (tm // m.shape[0]), axis=0)


def _modnorm(x, g, shift, scale):
    return _rms(x) * g * (1.0 + scale) + shift


def _mod_kernel(c_ref, w_ref, b_ref, o_ref):
    c = c_ref[...]
    o_ref[...] = _dot(c * _sigmoid(c), w_ref[...]) + b_ref[...]


def _mod_call(c_all, w_mod, b_mod):
    depth, d, n9 = w_mod.shape
    rows = c_all.shape[0]
    n_sub9 = n9 // d
    tn = d
    return pl.pallas_call(
        _mod_kernel,
        grid=(depth, n_sub9),
        in_specs=[pl.BlockSpec((rows, d), lambda l, j: (0, 0)),
                  pl.BlockSpec((None, d, tn), lambda l, j: (l, 0, j)),
                  pl.BlockSpec((None, 1, tn), lambda l, j: (l, 0, j))],
        out_specs=pl.BlockSpec((None, None, rows, tn), lambda l, j: (l, j, 0, 0)),
        out_shape=jax.ShapeDtypeStruct((depth, n_sub9, rows, d), F32),
        compiler_params=_cparams(2),
        name="mod",
    )(c_all, w_mod, b_mod.reshape(depth, 1, n9))


class _Groups:
    def __init__(self, bp, lp, bs, ls, tm, mod_row0):
        assert lp % tm == 0
        self.tm, self.ts, self.bs = tm, bs * ls, bs
        self.n_prompt = bp * lp // tm
        self.seq_rows = (mod_row0, lp // tm)

    def _ptile(self, i):
        return jnp.minimum(i, self.n_prompt - 1)

    def prompt(self, width):
        return pl.BlockSpec((self.tm, width), lambda i: (self._ptile(i), 0))

    def sample(self, width, **kw):
        return pl.BlockSpec((self.ts, width), lambda i: (0, 0), **kw)

    def prompt_mod(self, d, layer, idx):
        row0, per_seq = self.seq_rows
        return pl.BlockSpec((None, None, SUBLANES, d),
                            lambda i: (layer, idx, (row0 + self._ptile(i) // per_seq) // SUBLANES, 0))

    def sample_mod(self, d, layer, idx):
        return pl.BlockSpec((None, None, self.bs, d), lambda i: (layer, idx, 0, 0), **_RESIDENT)

    def mods(self, d, layer, idxs):
        return ([self.prompt_mod(d, layer, k) for k in idxs] + [self.sample_mod(d, layer, k) for k in idxs])

    def run(self, body, prompt_refs, sample_refs):
        i = pl.program_id(0)
        row0, per_seq = self.seq_rows
        seq_row = (row0 + i // per_seq) % SUBLANES
        pl.when(i < self.n_prompt)(lambda: body(prompt_refs, seq_row))
        pl.when(i == self.n_prompt)(lambda: body(sample_refs, None))


FFN_CHUNK = 3 * MXU_TILE


def _ffn_kernel(xp_ref, xs_ref, psh, psc, pgt, ssh, ssc, sgt, g_ref, win_ref, wo_ref, *rest, final, grp):
    if final:
        fg_ref, yp_ref, ys_ref, acc_sc = rest
    else:
        yp_ref, ys_ref, acc_sc = rest
    d_ff = wo_ref.shape[0]

    def body(refs, seq_row):
        x_ref, o_ref, sh_ref, sc_ref, gt_ref = refs
        tm = x_ref.shape[0]
        x = x_ref[...]
        u = _modnorm(x, g_ref[...], _rows(sh_ref, tm, seq_row), _rows(sc_ref, tm, seq_row)).astype(BF16)
        for c0 in range(0, d_ff, FFN_CHUNK):
            c1 = min(c0 + FFN_CHUNK, d_ff)
            hg = jnp.dot(u, win_ref[:, c0:c1], preferred_element_type=F32)
            hu = jnp.dot(u, win_ref[:, d_ff + c0:d_ff + c1], preferred_element_type=F32)
            act = (hg * _sigmoid(hg) * hu).astype(BF16)
            part = jnp.dot(act, wo_ref[c0:c1, :], preferred_element_type=F32)
            if c0 == 0:
                acc_sc[0:tm, :] = part
            else:
                acc_sc[0:tm, :] += part
        y = x + 0.5 * _rows(gt_ref, tm, seq_row) * acc_sc[0:tm, :]
        if final:
            y = _rms(y) * fg_ref[...]
        o_ref[...] = y

    grp.run(body, (xp_ref, yp_ref, psh, psc, pgt), (xs_ref, ys_ref, ssh, ssc, sgt))


def _ffn_call(xp, xs, grp, mod, layer, sub, norm_g, w_in, w_out, final_g=None):
    d = xp.shape[1]
    d_ff = w_out.shape[2]
    assert d_ff % LANES == 0 and FFN_CHUNK % LANES == 0 and grp.ts <= grp.tm
    final = final_g is not None
    const = lambda i: (0, 0)
    half = lambda i: (layer, sub // 2, 0, 0)
    in_specs = ([grp.prompt(d), grp.sample(d, **_RESIDENT)] + grp.mods(d, layer, (3 * sub, 3 * sub + 1, 3 * sub + 2))
                + [pl.BlockSpec((1, d), const),
                   pl.BlockSpec((None, None, d, 2 * d_ff), half, **_RESIDENT),
                   pl.BlockSpec((None, None, d_ff, d), half, **_RESIDENT)])
    args = [xp, xs] + [mod] * 6 + [norm_g.reshape(1, d), w_in, w_out]
    if final:
        in_specs.append(pl.BlockSpec((1, d), const))
        args.append(final_g.reshape(1, d))
    return pl.pallas_call(
        functools.partial(_ffn_kernel, final=final, grp=grp),
        grid=(grp.n_prompt + 1,),
        in_specs=in_specs,
        out_specs=[grp.prompt(d), grp.sample(d)],
        out_shape=[jax.ShapeDtypeStruct(xp.shape, F32), jax.ShapeDtypeStruct(xs.shape, F32)],
        scratch_shapes=[pltpu.VMEM((grp.tm, d), F32)],
        compiler_params=_cparams(1),
        name="ffn",
    )(*args)


def _mixin_kernel(xp_ref, xs_ref, psh, psc, ssh, ssc, g_ref, wa_ref, wb_ref, pap_ref, pbp_ref, pas_ref, pbs_ref, *, grp):
    def body(refs, seq_row):
        x_ref, pa_ref, pb_ref, sh_ref, sc_ref = refs
        tm = x_ref.shape[0]
        u = _modnorm(x_ref[...], g_ref[...], _rows(sh_ref, tm, seq_row), _rows(sc_ref, tm, seq_row)).astype(BF16)
        pa_ref[...] = jnp.dot(u, wa_ref[...], preferred_element_type=F32)
        pb_ref[...] = jnp.dot(u, wb_ref[...], preferred_element_type=F32)

    grp.run(body, (xp_ref, pap_ref, pbp_ref, psh, psc), (xs_ref, pas_ref, pbs_ref, ssh, ssc))


def _mixin_call(xp, xs, grp, mod, layer, norm_g, w_a, w_b):
    d = xp.shape[1]
    na, nb = w_a.shape[1], w_b.shape[1]
    const = lambda i: (0, 0)
    shape = lambda x, n: jax.ShapeDtypeStruct((x.shape[0], n), F32)
    return pl.pallas_call(
        functools.partial(_mixin_kernel, grp=grp),
        grid=(grp.n_prompt + 1,),
        in_specs=([grp.prompt(d), grp.sample(d, **_RESIDENT)] + grp.mods(d, layer, (3, 4))
                  + [pl.BlockSpec((1, d), const), pl.BlockSpec((d, na), const, **_RESIDENT),
                     pl.BlockSpec((d, nb), const, **_RESIDENT)]),
        out_specs=[grp.prompt(na), grp.prompt(nb), grp.sample(na), grp.sample(nb)],
        out_shape=[shape(xp, na), shape(xp, nb), shape(xs, na), shape(xs, nb)],
        compiler_params=_cparams(1),
        name="mixin",
    )(xp, xs, mod, mod, mod, mod, norm_g.reshape(1, d), w_a, w_b)


def _mixout_kernel(xp_ref, xs_ref, yap_ref, ybp_ref, yas_ref, ybs_ref, psh, psc, pgt, ssh, ssc, sgt, g_ref,
                   wgate_ref, bgate_ref, wbra_ref, wbrb_ref, wout_ref, op_ref, os_ref, *, grp):
    def body(refs, seq_row):
        x_ref, ya_ref, yb_ref, o_ref, sh_ref, sc_ref, gt_ref = refs
        tm, d = x_ref.shape
        x = x_ref[...]
        u = _modnorm(x, g_ref[...], _rows(sh_ref, tm, seq_row), _rows(sc_ref, tm, seq_row)).astype(BF16)
        gate = lambda c0: _sigmoid(jnp.dot(u, wgate_ref[:, c0:c0 + d], preferred_element_type=F32)
                                   + bgate_ref[:, c0:c0 + d])
        merged = gate(0) * _dot(ya_ref[...], wbra_ref[...])
        merged = merged + gate(d) * _dot(yb_ref[...], wbrb_ref[...])
        o_ref[...] = x + _rows(gt_ref, tm, seq_row) * _dot(merged, wout_ref[...])

    grp.run(body, (xp_ref, yap_ref, ybp_ref, op_ref, psh, psc, pgt), (xs_ref, yas_ref, ybs_ref, os_ref, ssh, ssc, sgt))


def _mixout_call(xp, xs, grp, mod, layer, norm_g, ya_p, yb_p, ya_s, yb_s, w_gate, b_gate, w_br_a, w_br_b, w_out):
    d = xp.shape[1]
    wa, wb = ya_p.shape[1], yb_p.shape[1]
    const = lambda i: (0, 0)
    lyr = lambda i: (layer, 0, 0)
    return pl.pallas_call(
        functools.partial(_mixout_kernel, grp=grp),
        grid=(grp.n_prompt + 1,),
        in_specs=([grp.prompt(d), grp.sample(d, **_RESIDENT), grp.prompt(wa), grp.prompt(wb),
                   grp.sample(wa, **_RESIDENT), grp.sample(wb, **_RESIDENT)] + grp.mods(d, layer, (3, 4, 5))
                  + [pl.BlockSpec((1, d), const),
                     pl.BlockSpec((None, d, 2 * d), lyr, **_RESIDENT), pl.BlockSpec((None, 1, 2 * d), lyr),
                     pl.BlockSpec((None, wa, d), lyr, **_RESIDENT), pl.BlockSpec((None, wb, d), lyr, **_RESIDENT),
                     pl.BlockSpec((None, d, d), lyr, **_RESIDENT)]),
        out_specs=[grp.prompt(d), grp.sample(d)],
        out_shape=[jax.ShapeDtypeStruct(xp.shape, F32), jax.ShapeDtypeStruct(xs.shape, F32)],
        compiler_params=_cparams(1),
        name="mixout",
    )(xp, xs, ya_p, yb_p, ya_s, yb_s, *([mod] * 6), norm_g.reshape(1, d), w_gate, b_gate.reshape(-1, 1, 2 * d),
      w_br_a, w_br_b, w_out)


def _rwkv_prep(xs, aw, seg_ones, w0, w2, a0, a2, g2, k_k, k_a):
    n_dec, n_aaa = w2.shape[0], a2.shape[0]
    r = xs[:, 0:aw]
    k = xs[:, aw:2 * aw]
    v = xs[:, 2 * aw:3 * aw]
    o = 3 * aw
    wl = xs[:, o:o + n_dec]
    al = xs[:, o + n_dec:o + n_dec + n_aaa]
    gl = xs[:, o + n_dec + n_aaa:]
    logw = -DECAY_SCALE * _sigmoid(w0 + _dot(jnp.tanh(wl), w2))
    asig = _sigmoid(a0 + _dot(al, a2))
    g = _dot(_sigmoid(gl), g2)
    kk = k * k_k
    kk = kk * lax.rsqrt(jnp.maximum(_seg_sum(kk * kk, seg_ones), 1e-24))
    k = k * (1.0 + (asig - 1.0) * k_a)
    return r, k, v, logw, -kk, kk * asig, g


def _pool_group_sum(read_rows, w):
    s = read_rows(0)
    for j in range(1, w):
        s = s + read_rows(j)
    return s


PROMPT_BLOCK = 64
PROMPT_SEQS = 8
SHIFT_PAD = 8
POOL_PAD = 24


def _pool_window_sums(xbuf, e2, e4, e8, s, tb):
    gw = xbuf.shape[2] // len(POOL_WINDOWS)
    lo, hi = POOL_PAD - 16, POOL_PAD + tb
    e2[s, lo:hi, :] = xbuf[s, lo:hi, :] + xbuf[s, lo - 1:hi - 1, :]
    e4[s, lo:hi, :] = e2[s, lo:hi, gw:] + e2[s, lo - 2:hi - 2, gw:]
    e8[s, lo:hi, :] = e4[s, lo:hi, gw:] + e4[s, lo - 4:hi - 4, gw:]
    blk = slice(POOL_PAD, POOL_PAD + tb)
    back8 = slice(POOL_PAD - 8, POOL_PAD - 8 + tb)
    return (e2[s, blk, 0:gw], e4[s, blk, 0:gw], e8[s, blk, 0:gw], e8[s, blk, gw:] + e8[s, back8, gw:])


def _prompt_mix_kernel(pa_ref, pb_ref, mu_ref, w0_ref, w2_ref, a0_ref, a2_ref, g2_ref, kk_ref, ka_ref, rk_ref,
                       lnw_ref, lnb_ref, wpool_ref, pscale_ref, seg_ref,
                       ya_ref, yb_ref, wkv_ref,
                       pbuf, xbuf, e2, e4, e8, s_sc, r_sc, k_sc, v_sc, a_sc, b_sc, lw_sc, g_sc, y_sc, *, n_blocks):
    ns, tb, aw = ya_ref.shape
    bw = yb_ref.shape[2]
    t = WKV_CHUNK
    n_pairs = aw // LANES
    blk = pl.program_id(1)
    seqs = range(ns)

    @pl.when(blk == 0)
    def _():
        pbuf[:, 0:SHIFT_PAD, :] = jnp.zeros((ns, SHIFT_PAD, pbuf.shape[2]), F32)
        xbuf[:, 0:POOL_PAD, :] = jnp.zeros((ns, POOL_PAD, bw), F32)
        e2[:, 0:POOL_PAD, :] = jnp.zeros((ns, POOL_PAD, e2.shape[2]), F32)
        e4[:, 0:POOL_PAD, :] = jnp.zeros((ns, POOL_PAD, e4.shape[2]), F32)
        s_sc[...] = jnp.zeros_like(s_sc)

    xs = []
    for s in seqs:
        pa = pa_ref[s]
        pbuf[s, SHIFT_PAD:SHIFT_PAD + tb, :] = pa
        prev = pbuf[s, SHIFT_PAD - 1:SHIFT_PAD - 1 + tb, :]
        xs.append(pa + (prev - pa) * mu_ref[...])
        pbuf[s, 0:SHIFT_PAD, :] = pbuf[s, tb:tb + SHIFT_PAD, :]
    xs = jnp.concatenate(xs, axis=0)

    seg_ones = seg_ref[...]
    r, k, v, logw, a, b, g = _rwkv_prep(xs, aw, seg_ones, w0_ref[...], w2_ref[...], a0_ref[...], a2_ref[...],
                                        g2_ref[...], kk_ref[...], ka_ref[...])
    r_sc[...] = r
    k_sc[...] = k
    v_sc[...] = v
    a_sc[...] = a
    b_sc[...] = b
    lw_sc[...] = logw
    g_sc[...] = g

    t2 = 2 * t
    ri = lax.broadcasted_iota(jnp.int32, (t, t), 0)
    ci = lax.broadcasted_iota(jnp.int32, (t, t), 1)
    tri_incl = (ri >= ci).astype(BF16)
    rs = lax.broadcasted_iota(jnp.int32, (t2, t2), 0)
    cs = lax.broadcasted_iota(jnp.int32, (t2, t2), 1)
    strict = cs < rs
    rs2 = lax.broadcasted_iota(jnp.int32, (t2, 2 * t2), 0)
    cs2 = lax.broadcasted_iota(jnp.int32, (t2, 2 * t2), 1)
    incl2 = jnp.where(cs2 >= t2, cs2 - t2, cs2) <= rs2
    lane = lax.broadcasted_iota(jnp.int32, (1, LANES), 1)
    head_mask = (lane < HEAD_DIM, lane >= HEAD_DIM)
    units = [(s, p) for s in seqs for p in range(n_pairs)]
    n_levels = int(math.log2(t))

    def stack(x):
        return jnp.concatenate([jnp.where(head_mask[0], x, 0.0), jnp.where(head_mask[1], x, 0.0)], axis=0)

    def chunk(ci_, carry):
        at, rt, bt, kt, vv, p_end, sls = [], [], [], [], [], [], []
        for s in seqs:
            sl = pl.ds(pl.multiple_of(s * tb + ci_ * t, t), t)
            lw = lw_sc[sl, :]
            hi = lw.astype(BF16)
            lo = (lw - hi.astype(F32)).astype(BF16)
            c = (jnp.dot(tri_incl, hi, preferred_element_type=F32)
                 + jnp.dot(tri_incl, lo, preferred_element_type=F32))
            e_in = jnp.exp(c)
            e_neg = jnp.exp(-c)
            p_end.append(e_in[t - 1:t, :])
            at.append(a_sc[sl, :] * jnp.exp(c - lw))
            rt.append(r_sc[sl, :] * e_in)
            bt.append(b_sc[sl, :] * e_neg)
            kt.append(k_sc[sl, :] * e_neg)
            vv.append(v_sc[sl, :])
            sls.append(sl)
        cols = [slice(p * LANES, (p + 1) * LANES) for p in range(n_pairs)]
        un = range(len(units))
        pe = [p_end[s][:, cols[p]] for s, p in units]
        a2 = [stack(at[s][:, cols[p]]).astype(BF16) for s, p in units]
        r2 = [stack(rt[s][:, cols[p]]).astype(BF16) for s, p in units]
        v2b = [stack(vv[s][:, cols[p]]).astype(BF16) for s, p in units]
        bk = [jnp.concatenate([stack(bt[s][:, cols[p]]), stack(kt[s][:, cols[p]])], axis=0) for s, p in units]
        g_ = [_dot_nt(jnp.concatenate([a2[i], r2[i]], axis=0), bk[i]) for i in un]
        n = [jnp.where(strict, g_[i][:t2, :t2], 0.0) for i in un]
        ak = [jnp.where(strict, g_[i][:t2, t2:], 0.0).astype(BF16) for i in un]
        rbk = [jnp.where(incl2, g_[i][t2:, :], 0.0).astype(BF16) for i in un]
        xm = n
        pw = [_dot(n[i], n[i]) for i in un]
        for _ in range(n_levels - 2):
            res = [_dot(jnp.concatenate([pw[i], xm[i]], axis=0), pw[i]) for i in un]
            xm = [xm[i] + pw[i] + res[i][t2:] for i in un]
            pw = [res[i][:t2] for i in un]
        res = [_dot(xm[i], pw[i]) for i in un]
        xm = [(xm[i] + pw[i] + res[i]).astype(BF16) for i in un]
        s_old = [s_sc[s * n_pairs + p] for s, p in units]
        s_bf = [s_old[i].astype(BF16) for i in un]
        rhs = [_dot_nt(a2[i], s_bf[i]) for i in un]
        akv = [jnp.dot(ak[i], v2b[i], preferred_element_type=F32) for i in un]
        rhs = [rhs[i] + akv[i] for i in un]
        u = [rhs[i] + _dot(xm[i], rhs[i]) for i in un]
        uv = [jnp.concatenate([u[i].astype(BF16), v2b[i]], axis=0) for i in un]
        ys = [_dot_nt(r2[i], s_bf[i]) for i in un]
        yi = [jnp.dot(rbk[i], uv[i], preferred_element_type=F32) for i in un]
        sn = [_dot_tn(uv[i], bk[i] * pe[i]) for i in un]
        for i, (s, p) in enumerate(units):
            y2 = ys[i] + yi[i]
            y_sc[sls[s], cols[p]] = y2[:t] + y2[t:]
            s_sc[s * n_pairs + p] = s_old[i] * pe[i] + sn[i]
        return carry

    lax.fori_loop(0, tb // t, chunk, 0)

    y = y_sc[...]
    inv_n = 1.0 / HEAD_DIM
    mean = _seg_sum(y, seg_ones) * inv_n
    yc = y - mean
    var = _seg_sum(yc * yc, seg_ones) * inv_n
    yn = yc * lax.rsqrt(var + GN_EPS) * lnw_ref[...] + lnb_ref[...]
    bonus = _seg_sum(r_sc[...] * k_sc[...] * rk_ref[...], seg_ones) * v_sc[...]
    ya = (yn + bonus) * g_sc[...]
    for s in seqs:
        ya_ref[s] = ya[s * tb:(s + 1) * tb].astype(ya_ref.dtype)

    pos = blk * tb + lax.broadcasted_iota(jnp.int32, (tb, 1), 0)
    gw = bw // len(POOL_WINDOWS)
    for s in seqs:
        pb = pb_ref[s]
        xbuf[s, POOL_PAD:POOL_PAD + tb, :] = pb
        sums = _pool_window_sums(xbuf, e2, e4, e8, s, tb)
        for gi, w in enumerate(POOL_WINDOWS):
            cols = slice(gi * gw, (gi + 1) * gw)
            inv_cnt = 1.0 / jnp.minimum(w, pos + 1).astype(F32)
            dlt = sums[gi] * inv_cnt - pb[:, cols]
            yb_ref[s, :, cols] = (_dot(dlt, wpool_ref[gi]) * pscale_ref[:, cols]).astype(yb_ref.dtype)
        xbuf[s, POOL_PAD - 16:POOL_PAD, :] = xbuf[s, tb + POOL_PAD - 16:tb + POOL_PAD, :]

    @pl.when(blk == n_blocks - 1)
    def _():
        for s in seqs:
            for p in range(n_pairs):
                s_p = s_sc[s * n_pairs + p]
                for h in range(2):
                    o = h * HEAD_DIM
                    wkv_ref[s, 2 * p + h] = s_p[o:o + HEAD_DIM, o:o + HEAD_DIM]


def _prompt_mix_call(pa, pb, batch, seq, prm):
    na = pa.shape[1]
    bw = pb.shape[1]
    aw = prm["w0"].shape[1]
    n_heads = aw // HEAD_DIM
    tb, ns = min(PROMPT_BLOCK, seq), PROMPT_SEQS
    assert seq % tb == 0 and tb % WKV_CHUNK == 0 and aw % LANES == 0 and batch % ns == 0
    assert len(POOL_WINDOWS) == 4 and POOL_HIST < 16 <= POOL_PAD - 8
    n_blocks = seq // tb
    gw = bw // len(POOL_WINDOWS)
    row = lambda b, c: (b, c, 0)
    const2 = lambda b, c: (0, 0)
    const3 = lambda b, c: (0, 0, 0)
    full = lambda a: pl.BlockSpec(a.shape, const2 if a.ndim == 2 else const3)
    names = ("mu", "w0", "w2", "a0", "a2", "g2", "k_k", "k_a", "r_k", "ln_w", "ln_b", "w_pool", "pool_scale", "seg")
    params = [prm[n] for n in names]
    blk = lambda: pltpu.VMEM((ns * tb, aw), F32)
    ya, yb, wkv = pl.pallas_call(
        functools.partial(_prompt_mix_kernel, n_blocks=n_blocks),
        grid=(batch // ns, n_blocks),
        in_specs=[pl.BlockSpec((ns, tb, na), row), pl.BlockSpec((ns, tb, bw), row)] + [full(a) for a in params],
        out_specs=[pl.BlockSpec((ns, tb, aw), row), pl.BlockSpec((ns, tb, bw), row),
                   pl.BlockSpec((ns, n_heads, HEAD_DIM, HEAD_DIM), lambda b, c: (b, 0, 0, 0))],
        out_shape=[jax.ShapeDtypeStruct((batch, seq, aw), MIX_DTYPE), jax.ShapeDtypeStruct((batch, seq, bw), MIX_DTYPE),
                   jax.ShapeDtypeStruct((batch, n_heads, HEAD_DIM, HEAD_DIM), F32)],
        scratch_shapes=[pltpu.VMEM((ns, tb + SHIFT_PAD, na), F32), pltpu.VMEM((ns, tb + POOL_PAD, bw), F32),
                        pltpu.VMEM((ns, tb + POOL_PAD, bw), F32), pltpu.VMEM((ns, tb + POOL_PAD, bw - gw), F32),
                        pltpu.VMEM((ns, tb + POOL_PAD, bw - 2 * gw), F32),
                        pltpu.VMEM((ns * (aw // LANES), LANES, LANES), F32),
                        blk(), blk(), blk(), blk(), blk(), blk(), blk(), blk()],
        compiler_params=_cparams(2),
        name="mix_prompt",
    )(pa.reshape(batch, seq, na), pb.reshape(batch, seq, bw), *params)
    return ya.reshape(batch * seq, aw), yb.reshape(batch * seq, bw), wkv


def _sample_prep_kernel(pa_ref, sh0_ref, pb_ref, hist_ref, mu_ref, w0_ref, w2_ref, a0_ref, a2_ref, g2_ref,
                        kk_ref, ka_ref, wpool_ref, pscale_ref, seg_ref,
                        r_ref, w_ref, k_ref, v_ref, a_ref, b_ref, g_ref, yb_ref, *, batch, seq):
    aw = r_ref.shape[1]
    bw = yb_ref.shape[1]
    pa = pa_ref[...]
    prev = jnp.concatenate([sh0_ref[...], pa[:(seq - 1) * batch]], axis=0)
    xs = pa + (prev - pa) * mu_ref[...]
    r, k, v, logw, a, b, g = _rwkv_prep(xs, aw, seg_ref[...], w0_ref[...], w2_ref[...], a0_ref[...], a2_ref[...],
                                        g2_ref[...], kk_ref[...], ka_ref[...])
    outs = ((r_ref, r), (w_ref, jnp.exp(logw)), (k_ref, k), (v_ref, v), (a_ref, a), (b_ref, b), (g_ref, g))
    for ref, val in outs:
        for ti in range(seq):
            ref[ti] = val[ti * batch:(ti + 1) * batch, :].T

    gw = bw // len(POOL_WINDOWS)

    def slab(i, cols):
        if i < POOL_HIST:
            return hist_ref[i * batch:(i + 1) * batch, cols]
        return pb_ref[(i - POOL_HIST) * batch:(i - POOL_HIST + 1) * batch, cols]

    for ti in range(seq):
        rows = slice(ti * batch, (ti + 1) * batch)
        for gi, w in enumerate(POOL_WINDOWS):
            cols = slice(gi * gw, (gi + 1) * gw)
            s = _pool_group_sum(lambda j: slab(POOL_HIST + ti - j, cols), w)
            cnt = float(min(w, PAST_LEN + ti + 1))
            dlt = s / cnt - pb_ref[rows, cols]
            yb_ref[rows, cols] = (_dot(dlt, wpool_ref[gi]) * pscale_ref[:, cols]).astype(yb_ref.dtype)


def _sample_prep_call(pa, sh0, pb, hist, batch, seq, prm):
    rows = pa.shape[0]
    aw = prm["w0"].shape[1]
    bw = pb.shape[1]
    names = ("mu", "w0", "w2", "a0", "a2", "g2", "k_k", "k_a", "w_pool", "pool_scale", "seg")
    args = [pa, sh0, pb, hist] + [prm[n] for n in names]
    full = lambda a: pl.BlockSpec(a.shape, (lambda i: (0, 0)) if a.ndim == 2 else (lambda i: (0, 0, 0)))
    o = jax.ShapeDtypeStruct((seq, aw, batch), F32)
    ob = jax.ShapeDtypeStruct((rows, bw), MIX_DTYPE)
    return pl.pallas_call(
        functools.partial(_sample_prep_kernel, batch=batch, seq=seq),
        grid=(1,),
        in_specs=[full(a) for a in args],
        out_specs=[full(o)] * 7 + [full(ob)],
        out_shape=[o] * 7 + [ob],
        compiler_params=_cparams(1),
        name="prep_sample",
    )(*args)


SAMPLE_VROWS = 64


def _sample_wkv_kernel(r_ref, w_ref, k_ref, v_ref, a_ref, b_ref, g_ref, rk_ref, lnw_ref, lnb_ref, s_ref, all_ref,
                       y_ref, so_ref, y_sc):
    del all_ref
    seq, hd, batch = r_ref.shape
    vb = s_ref.shape[0]
    h, vs = pl.program_id(0), pl.program_id(1)
    hrow = pl.multiple_of((h % 2) * hd, hd)
    for q in range(vb):
        tile = s_ref[q]
        for ti in range(seq):
            sa = jnp.sum(tile * a_ref[ti], axis=0, keepdims=True)
            tile = tile * w_ref[ti] + sa * b_ref[ti] + v_ref[ti, pl.ds(vs * vb + q, 1), :] * k_ref[ti]
            y_sc[ti, pl.ds(hrow + vs * vb + q, 1), :] = jnp.sum(tile * r_ref[ti], axis=0, keepdims=True)
        so_ref[q] = tile

    @pl.when(vs == pl.num_programs(1) - 1)
    def _():
        for ti in range(seq):
            rows = pl.ds(hrow, hd)
            y = y_sc[ti, rows, :]
            yc = y - jnp.mean(y, axis=0, keepdims=True)
            var = jnp.mean(yc * yc, axis=0, keepdims=True)
            yn = yc * lax.rsqrt(var + GN_EPS) * lnw_ref[...] + lnb_ref[...]
            bonus = jnp.sum(r_ref[ti] * k_ref[ti] * rk_ref[...], axis=0, keepdims=True) * v_ref[ti]
            y_sc[ti, rows, :] = (yn + bonus) * g_ref[ti]

    @pl.when((vs == pl.num_programs(1) - 1) & (h % 2 == 1))
    def _():
        for ti in range(seq):
            y_ref[ti * batch:(ti + 1) * batch, :] = y_sc[ti].T.astype(y_ref.dtype)


def _sample_wkv_call(vecs, state_t, new_states, layer, batch, seq, prm):
    n_heads, hd = state_t.shape[1:3]
    aw = n_heads * hd
    vb = SAMPLE_VROWS
    assert 2 * hd == LANES and hd % vb == 0
    vspec = pl.BlockSpec((seq, hd, batch), lambda h, s: (0, h, 0))
    cspec = pl.BlockSpec((hd, 1), lambda h, s: (h, 0))
    sspec = pl.BlockSpec((None, None, vb, hd, batch), lambda h, s: (layer, h, s, 0, 0))
    cols = [prm[n].reshape(aw, 1) for n in ("r_k", "ln_w", "ln_b")]
    args = [*vecs, *cols, state_t, new_states]
    return pl.pallas_call(
        _sample_wkv_kernel,
        grid=(n_heads, hd // vb),
        in_specs=[vspec] * 7 + [cspec] * 3 + [sspec, pl.BlockSpec(memory_space=pl.ANY)],
        out_specs=[pl.BlockSpec((seq * batch, LANES), lambda h, s: (0, h // 2)), sspec],
        out_shape=[jax.ShapeDtypeStruct((seq * batch, aw), MIX_DTYPE), jax.ShapeDtypeStruct(state_t.shape, F32)],
        scratch_shapes=[pltpu.VMEM((seq, LANES, batch), F32)],
        input_output_aliases={len(args) - 1: 1},
        compiler_params=_cparams(2),
        name="wkv_sample",
    )(*args)


def kernel(x_prompt, x_sample, state_wkv, state_shift, state_pool, c_prompt, c_sample, norm_g, w_mod, b_mod, w_ffn_in, w_ffn_out, w_in, mu_shift, w0, w2, a0, a2, g2, k_k, k_a, r_k, ln_x_w, ln_x_b, w_pool, pool_scale, w_br_a, w_br_b, w_gate, b_gate, w_out, final_g):
    bp, lp, d = x_prompt.shape
    bs, ls, _ = x_sample.shape
    depth = w_mod.shape[0]
    aw = w0.shape[1]
    a_proj = mu_shift.shape[1]
    bw = pool_scale.shape[1]

    bf = lambda w: w.astype(BF16)
    w_ffn_in_b, w_ffn_out_b = bf(w_ffn_in), bf(w_ffn_out)
    w_in_a, w_in_b = bf(w_in[:, :, :a_proj]), bf(w_in[:, :, a_proj:])
    w_gate_b, w_br_a_b, w_br_b_b, w_out_b = bf(w_gate), bf(w_br_a), bf(w_br_b), bf(w_out)
    w2_b, a2_b, g2_b, w_pool_b = bf(w2), bf(a2), bf(g2), bf(w_pool)
    head_of = jnp.arange(MXU_TILE, dtype=jnp.int32) // HEAD_DIM
    seg_ones = (head_of[:, None] == head_of[None, :]).astype(BF16)

    mod = _mod_call(jnp.concatenate([c_sample, c_prompt], axis=0), w_mod, b_mod)
    grp = _Groups(bp, lp, bs, ls, min(ROW_TILE, lp), mod_row0=bs)

    xp = x_prompt.reshape(bp * lp, d)
    xs = jnp.transpose(x_sample, (1, 0, 2)).reshape(ls * bs, d)
    hist_s = jnp.transpose(state_pool, (0, 2, 1, 3)).reshape(depth, POOL_HIST * bs, bw)
    state_t = jnp.transpose(state_wkv, (0, 2, 3, 4, 1))
    wkv_s = jnp.zeros_like(state_t)

    wkv_p, shift_p, pool_p, shift_s, pool_s = [], [], [], [], []
    for l in range(depth):
        prm = dict(mu=mu_shift[l][None], w0=w0[l][None], w2=w2_b[l], a0=a0[l][None], a2=a2_b[l], g2=g2_b[l],
                   k_k=k_k[l][None], k_a=k_a[l][None], r_k=r_k[l].reshape(1, aw), ln_w=ln_x_w[l][None],
                   ln_b=ln_x_b[l][None], w_pool=w_pool_b[l], pool_scale=pool_scale[l][None], seg=seg_ones)
        xp, xs = _ffn_call(xp, xs, grp, mod, l, 0, norm_g[l, 0], w_ffn_in_b, w_ffn_out_b)
        pa_p, pb_p, pa_s, pb_s = _mixin_call(xp, xs, grp, mod, l, norm_g[l, 1], w_in_a[l], w_in_b[l])
        ya_p, yb_p, s_new = _prompt_mix_call(pa_p, pb_p, bp, lp, prm)
        *vecs, yb_s = _sample_prep_call(pa_s, state_shift[l, :, 0, :], pb_s, hist_s[l], bs, ls, prm)
        ya_s, wkv_s = _sample_wkv_call(vecs, state_t, wkv_s, l, bs, ls, prm)
        xp, xs = _mixout_call(xp, xs, grp, mod, l, norm_g[l, 1], ya_p, yb_p, ya_s, yb_s, w_gate_b, b_gate, w_br_a_b,
                              w_br_b_b, w_out_b)
        xp, xs = _ffn_call(xp, xs, grp, mod, l, 2, norm_g[l, 2], w_ffn_in_b, w_ffn_out_b,
                           final_g if l == depth - 1 else None)
        wkv_p.append(s_new)
        shift_p.append(pa_p.reshape(bp, lp, a_proj)[:, -1:])
        pool_p.append(pb_p.reshape(bp, lp, bw)[:, -POOL_HIST:])
        shift_s.append(pa_s.reshape(ls, bs, a_proj)[-1][:, None, :])
        pb_bm = jnp.transpose(pb_s.reshape(ls, bs, bw), (1, 0, 2))
        pool_s.append(jnp.concatenate([state_pool[l], pb_bm], axis=1)[:, -POOL_HIST:])

    y_prompt = xp.reshape(bp, lp, d)
    y_sample = jnp.transpose(xs.reshape(ls, bs, d), (1, 0, 2))
    return (y_prompt, y_sample, jnp.stack(wkv_p), jnp.stack(shift_p), jnp.stack(pool_p),
            jnp.transpose(wkv_s, (0, 4, 1, 2, 3)), jnp.stack(shift_s), jnp.stack(pool_s))
```

```python
import functools
import math

import jax
import jax.numpy as jnp
from jax import lax
from jax.experimental import pallas as pl
from jax.experimental.pallas import tpu as pltpu

F32 = jnp.float32
BF16 = jnp.bfloat16
MIX_DTYPE = BF16

HEAD_DIM = 64
POOL_WINDOWS = (2, 4, 8, 16)
POOL_HIST = max(POOL_WINDOWS) - 1
N_SUB = 3
NORM_EPS = 1e-6
GN_EPS = 64e-5
PAST_LEN = 16384
DECAY_SCALE = math.exp(-0.5)

LANES = 128
MXU_TILE = 256
ROW_TILE = 1024
WKV_CHUNK = 64
VMEM_LIMIT_BYTES = 56 * 1024 * 1024


_RESIDENT = dict(pipeline_mode=pl.Buffered(1))


def _cparams(n_axes, **kw):
    return pltpu.CompilerParams(dimension_semantics=("arbitrary",) * n_axes,
                                vmem_limit_bytes=VMEM_LIMIT_BYTES, **kw)


def _sigmoid(x):
    return 0.5 * jnp.tanh(0.5 * x) + 0.5


def _dot(a, b):
    return jnp.dot(a.astype(BF16), b.astype(BF16), preferred_element_type=F32)


def _dot_nt(a, b):
    return lax.dot_general(a.astype(BF16), b.astype(BF16), (((1,), (1,)), ((), ())),
                           preferred_element_type=F32)


def _dot_tn(a, b):
    return lax.dot_general(a.astype(BF16), b.astype(BF16), (((0,), (0,)), ((), ())),
                           preferred_element_type=F32)


def _seg_sum(x, seg_ones):
    n, width = x.shape
    tw = seg_ones.shape[0]
    outs = []
    for c0 in range(0, width, tw):
        xc = x[:, c0:c0 + tw]
        hi = xc.astype(BF16)
        lo = (xc - hi.astype(F32)).astype(BF16)
        both = jnp.dot(jnp.concatenate([hi, lo], axis=0), seg_ones, preferred_element_type=F32)
        outs.append(both[:n] + both[n:])
    return jnp.concatenate(outs, axis=1)


def _rms(x):
    return x * lax.rsqrt(jnp.mean(x * x, axis=-1, keepdims=True) + NORM_EPS)


SUBLANES = 8


def _rows(ref, tm, seq_row):
    if seq_row is not None:
        return ref[pl.ds(seq_row, 1), :]
    m = ref[...]
    return m if m.shape[0] == tm else jnp.concatenate([m] * (tm // m.shape[0]), axis=0)


def _modnorm(x, g, shift, scale):
    return _rms(x) * g * (1.0 + scale) + shift


def _mod_kernel(c_ref, w_ref, b_ref, o_ref):
    c = c_ref[...]
    o_ref[...] = _dot(c * _sigmoid(c), w_ref[...]) + b_ref[...]


def _mod_call(c_all, w_mod, b_mod):
    depth, d, n9 = w_mod.shape
    rows = c_all.shape[0]
    n_sub9 = n9 // d
    tn = d
    return pl.pallas_call(
        _mod_kernel,
        grid=(depth, n_sub9),
        in_specs=[pl.BlockSpec((rows, d), lambda l, j: (0, 0)),
                  pl.BlockSpec((None, d, tn), lambda l, j: (l, 0, j)),
                  pl.BlockSpec((None, 1, tn), lambda l, j: (l, 0, j))],
        out_specs=pl.BlockSpec((None, None, rows, tn), lambda l, j: (l, j, 0, 0)),
        out_shape=jax.ShapeDtypeStruct((depth, n_sub9, rows, d), F32),
        compiler_params=_cparams(2),
        name="mod",
    )(c_all, w_mod, b_mod.reshape(depth, 1, n9))


class _Groups:
    def __init__(self, bp, lp, bs, ls, tm, mod_row0):
        assert lp % tm == 0
        self.tm, self.ts, self.bs = tm, bs * ls, bs
        self.n_prompt = bp * lp // tm
        self.seq_rows = (mod_row0, lp // tm)

    def _ptile(self, i):
        return jnp.minimum(i, self.n_prompt - 1)

    def prompt(self, width):
        return pl.BlockSpec((self.tm, width), lambda i: (self._ptile(i), 0))

    def sample(self, width, **kw):
        return pl.BlockSpec((self.ts, width), lambda i: (0, 0), **kw)

    def prompt_mod(self, d, layer, idx):
        row0, per_seq = self.seq_rows
        return pl.BlockSpec((None, None, SUBLANES, d),
                            lambda i: (layer, idx, (row0 + self._ptile(i) // per_seq) // SUBLANES, 0))

    def sample_mod(self, d, layer, idx):
        return pl.BlockSpec((None, None, self.bs, d), lambda i: (layer, idx, 0, 0), **_RESIDENT)

    def mods(self, d, layer, idxs):
        return ([self.prompt_mod(d, layer, k) for k in idxs] + [self.sample_mod(d, layer, k) for k in idxs])

    def run(self, body, prompt_refs, sample_refs):
        i = pl.program_id(0)
        row0, per_seq = self.seq_rows
        seq_row = (row0 + i // per_seq) % SUBLANES
        pl.when(i < self.n_prompt)(lambda: body(prompt_refs, seq_row))
        pl.when(i == self.n_prompt)(lambda: body(sample_refs, None))


FFN_CHUNK = 3 * MXU_TILE


def _ffn_kernel(xp_ref, xs_ref, psh, psc, pgt, ssh, ssc, sgt, g_ref, win_ref, wo_ref, *rest, final, grp):
    if final:
        fg_ref, yp_ref, ys_ref, acc_sc = rest
    else:
        yp_ref, ys_ref, acc_sc = rest
    d_ff = wo_ref.shape[0]

    def body(refs, seq_row):
        x_ref, o_ref, sh_ref, sc_ref, gt_ref = refs
        tm = x_ref.shape[0]
        x = x_ref[...]
        u = _modnorm(x, g_ref[...], _rows(sh_ref, tm, seq_row), _rows(sc_ref, tm, seq_row)).astype(BF16)
        for c0 in range(0, d_ff, FFN_CHUNK):
            c1 = min(c0 + FFN_CHUNK, d_ff)
            hg = jnp.dot(u, win_ref[:, c0:c1], preferred_element_type=F32)
            hu = jnp.dot(u, win_ref[:, d_ff + c0:d_ff + c1], preferred_element_type=F32)
            act = (hg * _sigmoid(hg) * hu).astype(BF16)
            part = jnp.dot(act, wo_ref[c0:c1, :], preferred_element_type=F32)
            if c0 == 0:
                acc_sc[0:tm, :] = part
            else:
                acc_sc[0:tm, :] += part
        y = x + 0.5 * _rows(gt_ref, tm, seq_row) * acc_sc[0:tm, :]
        if final:
            y = _rms(y) * fg_ref[...]
        o_ref[...] = y

    grp.run(body, (xp_ref, yp_ref, psh, psc, pgt), (xs_ref, ys_ref, ssh, ssc, sgt))


def _ffn_call(xp, xs, grp, mod, layer, sub, norm_g, w_in, w_out, final_g=None):
    d = xp.shape[1]
    d_ff = w_out.shape[2]
    assert d_ff % LANES == 0 and FFN_CHUNK % LANES == 0 and grp.ts <= grp.tm
    final = final_g is not None
    const = lambda i: (0, 0)
    half = lambda i: (layer, sub // 2, 0, 0)
    in_specs = ([grp.prompt(d), grp.sample(d, **_RESIDENT)] + grp.mods(d, layer, (3 * sub, 3 * sub + 1, 3 * sub + 2))
                + [pl.BlockSpec((1, d), const),
                   pl.BlockSpec((None, None, d, 2 * d_ff), half, **_RESIDENT),
                   pl.BlockSpec((None, None, d_ff, d), half, **_RESIDENT)])
    args = [xp, xs] + [mod] * 6 + [norm_g.reshape(1, d), w_in, w_out]
    if final:
        in_specs.append(pl.BlockSpec((1, d), const))
        args.append(final_g.reshape(1, d))
    return pl.pallas_call(
        functools.partial(_ffn_kernel, final=final, grp=grp),
        grid=(grp.n_prompt + 1,),
        in_specs=in_specs,
        out_specs=[grp.prompt(d), grp.sample(d)],
        out_shape=[jax.ShapeDtypeStruct(xp.shape, F32), jax.ShapeDtypeStruct(xs.shape, F32)],
        scratch_shapes=[pltpu.VMEM((grp.tm, d), F32)],
        compiler_params=_cparams(1),
        name="ffn",
    )(*args)


def _mixin_kernel(xp_ref, xs_ref, psh, psc, ssh, ssc, g_ref, wa_ref, wb_ref, pap_ref, pbp_ref, pas_ref, pbs_ref, *, grp):
    def body(refs, seq_row):
        x_ref, pa_ref, pb_ref, sh_ref, sc_ref = refs
        tm = x_ref.shape[0]
        u = _modnorm(x_ref[...], g_ref[...], _rows(sh_ref, tm, seq_row), _rows(sc_ref, tm, seq_row)).astype(BF16)
        pa_ref[...] = _dot_nt(u, wa_ref[...])
        pb_ref[...] = _dot_nt(u, wb_ref[...])

    grp.run(body, (xp_ref, pap_ref, pbp_ref, psh, psc), (xs_ref, pas_ref, pbs_ref, ssh, ssc))


def _mixin_call(xp, xs, grp, mod, layer, norm_g, w_a, w_b):
    d = xp.shape[1]
    na, nb = w_a.shape[0], w_b.shape[0]
    const = lambda i: (0, 0)
    shape = lambda x, n: jax.ShapeDtypeStruct((x.shape[0], n), F32)
    return pl.pallas_call(
        functools.partial(_mixin_kernel, grp=grp),
        grid=(grp.n_prompt + 1,),
        in_specs=([grp.prompt(d), grp.sample(d, **_RESIDENT)] + grp.mods(d, layer, (3, 4))
                  + [pl.BlockSpec((1, d), const), pl.BlockSpec((na, d), const, **_RESIDENT),
                     pl.BlockSpec((nb, d), const, **_RESIDENT)]),
        out_specs=[grp.prompt(na), grp.prompt(nb), grp.sample(na), grp.sample(nb)],
        out_shape=[shape(xp, na), shape(xp, nb), shape(xs, na), shape(xs, nb)],
        compiler_params=_cparams(1),
        name="mixin",
    )(xp, xs, mod, mod, mod, mod, norm_g.reshape(1, d), w_a, w_b)


def _mixout_kernel(xp_ref, xs_ref, yap_ref, ybp_ref, yas_ref, ybs_ref, psh, psc, pgt, ssh, ssc, sgt, g_ref,
                   wgate_ref, bgate_ref, wbra_ref, wbrb_ref, wout_ref, op_ref, os_ref, *, grp):
    def body(refs, seq_row):
        x_ref, ya_ref, yb_ref, o_ref, sh_ref, sc_ref, gt_ref = refs
        tm, d = x_ref.shape
        x = x_ref[...]
        u = _modnorm(x, g_ref[...], _rows(sh_ref, tm, seq_row), _rows(sc_ref, tm, seq_row)).astype(BF16)
        gate = lambda c0: _sigmoid(jnp.dot(u, wgate_ref[:, c0:c0 + d], preferred_element_type=F32)
                                   + bgate_ref[:, c0:c0 + d])
        merged = gate(0) * _dot(ya_ref[...], wbra_ref[...])
        merged = merged + gate(d) * _dot(yb_ref[...], wbrb_ref[...])
        o_ref[...] = x + _rows(gt_ref, tm, seq_row) * _dot(merged, wout_ref[...])

    grp.run(body, (xp_ref, yap_ref, ybp_ref, op_ref, psh, psc, pgt), (xs_ref, yas_ref, ybs_ref, os_ref, ssh, ssc, sgt))


def _mixout_call(xp, xs, grp, mod, layer, norm_g, ya_p, yb_p, ya_s, yb_s, w_gate, b_gate, w_br_a, w_br_b, w_out):
    d = xp.shape[1]
    wa, wb = ya_p.shape[1], yb_p.shape[1]
    const = lambda i: (0, 0)
    lyr = lambda i: (layer, 0, 0)
    return pl.pallas_call(
        functools.partial(_mixout_kernel, grp=grp),
        grid=(grp.n_prompt + 1,),
        in_specs=([grp.prompt(d), grp.sample(d, **_RESIDENT), grp.prompt(wa), grp.prompt(wb),
                   grp.sample(wa, **_RESIDENT), grp.sample(wb, **_RESIDENT)] + grp.mods(d, layer, (3, 4, 5))
                  + [pl.BlockSpec((1, d), const),
                     pl.BlockSpec((None, d, 2 * d), lyr, **_RESIDENT), pl.BlockSpec((None, 1, 2 * d), lyr),
                     pl.BlockSpec((None, wa, d), lyr, **_RESIDENT), pl.BlockSpec((None, wb, d), lyr, **_RESIDENT),
                     pl.BlockSpec((None, d, d), lyr, **_RESIDENT)]),
        out_specs=[grp.prompt(d), grp.sample(d)],
        out_shape=[jax.ShapeDtypeStruct(xp.shape, F32), jax.ShapeDtypeStruct(xs.shape, F32)],
        compiler_params=_cparams(1),
        name="mixout",
    )(xp, xs, ya_p, yb_p, ya_s, yb_s, *([mod] * 6), norm_g.reshape(1, d), w_gate, b_gate.reshape(-1, 1, 2 * d),
      w_br_a, w_br_b, w_out)


def _rwkv_prep(xs, aw, seg_ones, w0, w2, a0, a2, g2, k_k, k_a):
    n_dec, n_aaa = w2.shape[0], a2.shape[0]
    r = xs[:, 0:aw]
    k = xs[:, aw:2 * aw]
    v = xs[:, 2 * aw:3 * aw]
    o = 3 * aw
    wl = xs[:, o:o + n_dec]
    al = xs[:, o + n_dec:o + n_dec + n_aaa]
    gl = xs[:, o + n_dec + n_aaa:]
    logw = -DECAY_SCALE * _sigmoid(w0 + _dot(jnp.tanh(wl), w2))
    asig = _sigmoid(a0 + _dot(al, a2))
    g = _dot(_sigmoid(gl), g2)
    kk = k * k_k
    kk = kk * lax.rsqrt(jnp.maximum(_seg_sum(kk * kk, seg_ones), 1e-24))
    k = k * (1.0 + (asig - 1.0) * k_a)
    return r, k, v, logw, -kk, kk * asig, g


def _pool_group_sum(read_rows, w):
    s = read_rows(0)
    for j in range(1, w):
        s = s + read_rows(j)
    return s


PROMPT_BLOCK = 64
PROMPT_SEQS = 8
SHIFT_PAD = 8
POOL_PAD = 24


def _pool_window_sums(xbuf, e2, e4, e8, s, tb):
    gw = xbuf.shape[2] // len(POOL_WINDOWS)
    lo, hi = POOL_PAD - 16, POOL_PAD + tb
    e2[s, lo:hi, :] = xbuf[s, lo:hi, :] + xbuf[s, lo - 1:hi - 1, :]
    e4[s, lo:hi, :] = e2[s, lo:hi, gw:] + e2[s, lo - 2:hi - 2, gw:]
    e8[s, lo:hi, :] = e4[s, lo:hi, gw:] + e4[s, lo - 4:hi - 4, gw:]
    blk = slice(POOL_PAD, POOL_PAD + tb)
    back8 = slice(POOL_PAD - 8, POOL_PAD - 8 + tb)
    return (e2[s, blk, 0:gw], e4[s, blk, 0:gw], e8[s, blk, 0:gw], e8[s, blk, gw:] + e8[s, back8, gw:])


def _prompt_mix_kernel(pa_ref, pb_ref, mu_ref, w0_ref, w2_ref, a0_ref, a2_ref, g2_ref, kk_ref, ka_ref, rk_ref,
                       lnw_ref, lnb_ref, wpool_ref, pscale_ref, seg_ref,
                       ya_ref, yb_ref, wkv_ref,
                       pbuf, xbuf, e2, e4, e8, s_sc, r_sc, k_sc, v_sc, a_sc, b_sc, lw_sc, g_sc, y_sc, *, n_blocks):
    ns, tb, aw = ya_ref.shape
    bw = yb_ref.shape[2]
    t = WKV_CHUNK
    n_pairs = aw // LANES
    blk = pl.program_id(1)
    seqs = range(ns)

    @pl.when(blk == 0)
    def _():
        pbuf[:, 0:SHIFT_PAD, :] = jnp.zeros((ns, SHIFT_PAD, pbuf.shape[2]), F32)
        xbuf[:, 0:POOL_PAD, :] = jnp.zeros((ns, POOL_PAD, bw), F32)
        e2[:, 0:POOL_PAD, :] = jnp.zeros((ns, POOL_PAD, e2.shape[2]), F32)
        e4[:, 0:POOL_PAD, :] = jnp.zeros((ns, POOL_PAD, e4.shape[2]), F32)
        s_sc[...] = jnp.zeros_like(s_sc)

    xs = []
    for s in seqs:
        pa = pa_ref[s]
        pbuf[s, SHIFT_PAD:SHIFT_PAD + tb, :] = pa
        prev = pbuf[s, SHIFT_PAD - 1:SHIFT_PAD - 1 + tb, :]
        xs.append(pa + (prev - pa) * mu_ref[...])
        pbuf[s, 0:SHIFT_PAD, :] = pbuf[s, tb:tb + SHIFT_PAD, :]
    xs = jnp.concatenate(xs, axis=0)

    seg_ones = seg_ref[...]
    r, k, v, logw, a, b, g = _rwkv_prep(xs, aw, seg_ones, w0_ref[...], w2_ref[...], a0_ref[...], a2_ref[...],
                                        g2_ref[...], kk_ref[...], ka_ref[...])
    r_sc[...] = r
    k_sc[...] = k
    v_sc[...] = v
    a_sc[...] = a
    b_sc[...] = b
    lw_sc[...] = logw
    g_sc[...] = g

    t2 = 2 * t
    ri = lax.broadcasted_iota(jnp.int32, (t, t), 0)
    ci = lax.broadcasted_iota(jnp.int32, (t, t), 1)
    tri_incl = (ri >= ci).astype(BF16)
    rs = lax.broadcasted_iota(jnp.int32, (t2, t2), 0)
    cs = lax.broadcasted_iota(jnp.int32, (t2, t2), 1)
    strict = cs < rs
    rs2 = lax.broadcasted_iota(jnp.int32, (t2, 2 * t2), 0)
    cs2 = lax.broadcasted_iota(jnp.int32, (t2, 2 * t2), 1)
    incl2 = jnp.where(cs2 >= t2, cs2 - t2, cs2) <= rs2
    lane = lax.broadcasted_iota(jnp.int32, (1, LANES), 1)
    head_mask = (lane < HEAD_DIM, lane >= HEAD_DIM)
    units = [(s, p) for s in seqs for p in range(n_pairs)]
    n_levels = int(math.log2(t))

    def stack(x):
        return jnp.concatenate([jnp.where(head_mask[0], x, 0.0), jnp.where(head_mask[1], x, 0.0)], axis=0)

    def chunk(ci_, carry):
        at, rt, bt, kt, vv, p_end, sls = [], [], [], [], [], [], []
        for s in seqs:
            sl = pl.ds(pl.multiple_of(s * tb + ci_ * t, t), t)
            lw = lw_sc[sl, :]
            hi = lw.astype(BF16)
            lo = (lw - hi.astype(F32)).astype(BF16)
            c = (jnp.dot(tri_incl, hi, preferred_element_type=F32)
                 + jnp.dot(tri_incl, lo, preferred_element_type=F32))
            e_in = jnp.exp(c)
            e_neg = jnp.exp(-c)
            p_end.append(e_in[t - 1:t, :])
            at.append(a_sc[sl, :] * jnp.exp(c - lw))
            rt.append(r_sc[sl, :] * e_in)
            bt.append(b_sc[sl, :] * e_neg)
            kt.append(k_sc[sl, :] * e_neg)
            vv.append(v_sc[sl, :])
            sls.append(sl)
        cols = [slice(p * LANES, (p + 1) * LANES) for p in range(n_pairs)]
        un = range(len(units))
        pe = [p_end[s][:, cols[p]] for s, p in units]
        a2 = [stack(at[s][:, cols[p]]).astype(BF16) for s, p in units]
        r2 = [stack(rt[s][:, cols[p]]).astype(BF16) for s, p in units]
        v2b = [stack(vv[s][:, cols[p]]).astype(BF16) for s, p in units]
        bk = [jnp.concatenate([stack(bt[s][:, cols[p]]), stack(kt[s][:, cols[p]])], axis=0) for s, p in units]
        g_ = [_dot_nt(jnp.concatenate([a2[i], r2[i]], axis=0), bk[i]) for i in un]
        n = [jnp.where(strict, g_[i][:t2, :t2], 0.0) for i in un]
        ak = [jnp.where(strict, g_[i][:t2, t2:], 0.0).astype(BF16) for i in un]
        rbk = [jnp.where(incl2, g_[i][t2:, :], 0.0).astype(BF16) for i in un]
        xm = n
        pw = [_dot(n[i], n[i]) for i in un]
        for _ in range(n_levels - 2):
            res = [_dot(jnp.concatenate([pw[i], xm[i]], axis=0), pw[i]) for i in un]
            xm = [xm[i] + pw[i] + res[i][t2:] for i in un]
            pw = [res[i][:t2] for i in un]
        res = [_dot(xm[i], pw[i]) for i in un]
        xm = [(xm[i] + pw[i] + res[i]).astype(BF16) for i in un]
        s_old = [s_sc[s * n_pairs + p] for s, p in units]
        s_bf = [s_old[i].astype(BF16) for i in un]
        rhs = [_dot_nt(a2[i], s_bf[i]) for i in un]
        akv = [jnp.dot(ak[i], v2b[i], preferred_element_type=F32) for i in un]
        rhs = [rhs[i] + akv[i] for i in un]
        u = [rhs[i] + _dot(xm[i], rhs[i]) for i in un]
        uv = [jnp.concatenate([u[i].astype(BF16), v2b[i]], axis=0) for i in un]
        ys = [_dot_nt(r2[i], s_bf[i]) for i in un]
        yi = [jnp.dot(rbk[i], uv[i], preferred_element_type=F32) for i in un]
        sn = [_dot_tn(uv[i], bk[i] * pe[i]) for i in un]
        for i, (s, p) in enumerate(units):
            y2 = ys[i] + yi[i]
            y_sc[sls[s], cols[p]] = y2[:t] + y2[t:]
            s_sc[s * n_pairs + p] = s_old[i] * pe[i] + sn[i]
        return carry

    lax.fori_loop(0, tb // t, chunk, 0)

    y = y_sc[...]
    inv_n = 1.0 / HEAD_DIM
    mean = _seg_sum(y, seg_ones) * inv_n
    yc = y - mean
    var = _seg_sum(yc * yc, seg_ones) * inv_n
    yn = yc * lax.rsqrt(var + GN_EPS) * lnw_ref[...] + lnb_ref[...]
    bonus = _seg_sum(r_sc[...] * k_sc[...] * rk_ref[...], seg_ones) * v_sc[...]
    ya = (yn + bonus) * g_sc[...]
    for s in seqs:
        ya_ref[s] = ya[s * tb:(s + 1) * tb].astype(ya_ref.dtype)

    pos = blk * tb + lax.broadcasted_iota(jnp.int32, (tb, 1), 0)
    gw = bw // len(POOL_WINDOWS)
    for s in seqs:
        pb = pb_ref[s]
        xbuf[s, POOL_PAD:POOL_PAD + tb, :] = pb
        sums = _pool_window_sums(xbuf, e2, e4, e8, s, tb)
        for gi, w in enumerate(POOL_WINDOWS):
            cols = slice(gi * gw, (gi + 1) * gw)
            inv_cnt = 1.0 / jnp.minimum(w, pos + 1).astype(F32)
            dlt = sums[gi] * inv_cnt - pb[:, cols]
            yb_ref[s, :, cols] = (_dot(dlt, wpool_ref[gi]) * pscale_ref[:, cols]).astype(yb_ref.dtype)
        xbuf[s, POOL_PAD - 16:POOL_PAD, :] = xbuf[s, tb + POOL_PAD - 16:tb + POOL_PAD, :]

    @pl.when(blk == n_blocks - 1)
    def _():
        for s in seqs:
            for p in range(n_pairs):
                s_p = s_sc[s * n_pairs + p]
                for h in range(2):
                    o = h * HEAD_DIM
                    wkv_ref[s, 2 * p + h] = s_p[o:o + HEAD_DIM, o:o + HEAD_DIM]


def _prompt_mix_call(pa, pb, batch, seq, prm):
    na = pa.shape[1]
    bw = pb.shape[1]
    aw = prm["w0"].shape[1]
    n_heads = aw // HEAD_DIM
    tb, ns = min(PROMPT_BLOCK, seq), PROMPT_SEQS
    assert seq % tb == 0 and tb % WKV_CHUNK == 0 and aw % LANES == 0 and batch % ns == 0
    assert len(POOL_WINDOWS) == 4 and POOL_HIST < 16 <= POOL_PAD - 8
    n_blocks = seq // tb
    gw = bw // len(POOL_WINDOWS)
    row = lambda b, c: (b, c, 0)
    const2 = lambda b, c: (0, 0)
    const3 = lambda b, c: (0, 0, 0)
    full = lambda a: pl.BlockSpec(a.shape, const2 if a.ndim == 2 else const3)
    names = ("mu", "w0", "w2", "a0", "a2", "g2", "k_k", "k_a", "r_k", "ln_w", "ln_b", "w_pool", "pool_scale", "seg")
    params = [prm[n] for n in names]
    blk = lambda: pltpu.VMEM((ns * tb, aw), F32)
    ya, yb, wkv = pl.pallas_call(
        functools.partial(_prompt_mix_kernel, n_blocks=n_blocks),
        grid=(batch // ns, n_blocks),
        in_specs=[pl.BlockSpec((ns, tb, na), row), pl.BlockSpec((ns, tb, bw), row)] + [full(a) for a in params],
        out_specs=[pl.BlockSpec((ns, tb, aw), row), pl.BlockSpec((ns, tb, bw), row),
                   pl.BlockSpec((ns, n_heads, HEAD_DIM, HEAD_DIM), lambda b, c: (b, 0, 0, 0))],
        out_shape=[jax.ShapeDtypeStruct((batch, seq, aw), MIX_DTYPE), jax.ShapeDtypeStruct((batch, seq, bw), MIX_DTYPE),
                   jax.ShapeDtypeStruct((batch, n_heads, HEAD_DIM, HEAD_DIM), F32)],
        scratch_shapes=[pltpu.VMEM((ns, tb + SHIFT_PAD, na), F32), pltpu.VMEM((ns, tb + POOL_PAD, bw), F32),
                        pltpu.VMEM((ns, tb + POOL_PAD, bw), F32), pltpu.VMEM((ns, tb + POOL_PAD, bw - gw), F32),
                        pltpu.VMEM((ns, tb + POOL_PAD, bw - 2 * gw), F32),
                        pltpu.VMEM((ns * (aw // LANES), LANES, LANES), F32),
                        blk(), blk(), blk(), blk(), blk(), blk(), blk(), blk()],
        compiler_params=_cparams(2),
        name="mix_prompt",
    )(pa.reshape(batch, seq, na), pb.reshape(batch, seq, bw), *params)
    return ya.reshape(batch * seq, aw), yb.reshape(batch * seq, bw), wkv


def _sample_prep_kernel(pa_ref, sh0_ref, pb_ref, hist_ref, mu_ref, w0_ref, w2_ref, a0_ref, a2_ref, g2_ref,
                        kk_ref, ka_ref, wpool_ref, pscale_ref, seg_ref,
                        r_ref, w_ref, k_ref, v_ref, a_ref, b_ref, g_ref, yb_ref, *, batch, seq):
    aw = r_ref.shape[1]
    bw = yb_ref.shape[1]
    pa = pa_ref[...]
    prev = jnp.concatenate([sh0_ref[...], pa[:(seq - 1) * batch]], axis=0)
    xs = pa + (prev - pa) * mu_ref[...]
    r, k, v, logw, a, b, g = _rwkv_prep(xs, aw, seg_ref[...], w0_ref[...], w2_ref[...], a0_ref[...], a2_ref[...],
                                        g2_ref[...], kk_ref[...], ka_ref[...])
    outs = ((r_ref, r), (w_ref, jnp.exp(logw)), (k_ref, k), (v_ref, v), (a_ref, a), (b_ref, b), (g_ref, g))
    for ref, val in outs:
        for ti in range(seq):
            ref[ti] = val[ti * batch:(ti + 1) * batch, :].T

    gw = bw // len(POOL_WINDOWS)

    def slab(i, cols):
        if i < POOL_HIST:
            return hist_ref[i * batch:(i + 1) * batch, cols]
        return pb_ref[(i - POOL_HIST) * batch:(i - POOL_HIST + 1) * batch, cols]

    for ti in range(seq):
        rows = slice(ti * batch, (ti + 1) * batch)
        for gi, w in enumerate(POOL_WINDOWS):
            cols = slice(gi * gw, (gi + 1) * gw)
            s = _pool_group_sum(lambda j: slab(POOL_HIST + ti - j, cols), w)
            cnt = float(min(w, PAST_LEN + ti + 1))
            dlt = s / cnt - pb_ref[rows, cols]
            yb_ref[rows, cols] = (_dot(dlt, wpool_ref[gi]) * pscale_ref[:, cols]).astype(yb_ref.dtype)


def _sample_prep_call(pa, sh0, pb, hist, batch, seq, prm):
    rows = pa.shape[0]
    aw = prm["w0"].shape[1]
    bw = pb.shape[1]
    names = ("mu", "w0", "w2", "a0", "a2", "g2", "k_k", "k_a", "w_pool", "pool_scale", "seg")
    args = [pa, sh0, pb, hist] + [prm[n] for n in names]
    full = lambda a: pl.BlockSpec(a.shape, (lambda i: (0, 0)) if a.ndim == 2 else (lambda i: (0, 0, 0)))
    o = jax.ShapeDtypeStruct((seq, aw, batch), F32)
    ob = jax.ShapeDtypeStruct((rows, bw), MIX_DTYPE)
    return pl.pallas_call(
        functools.partial(_sample_prep_kernel, batch=batch, seq=seq),
        grid=(1,),
        in_specs=[full(a) for a in args],
        out_specs=[full(o)] * 7 + [full(ob)],
        out_shape=[o] * 7 + [ob],
        compiler_params=_cparams(1),
        name="prep_sample",
    )(*args)


SAMPLE_VROWS = 64


def _sample_wkv_kernel(r_ref, w_ref, k_ref, v_ref, a_ref, b_ref, g_ref, rk_ref, lnw_ref, lnb_ref, s_ref, all_ref,
                       y_ref, so_ref, y_sc):
    del all_ref
    seq, hd, batch = r_ref.shape
    vb = s_ref.shape[0]
    h, vs = pl.program_id(0), pl.program_id(1)
    hrow = pl.multiple_of((h % 2) * hd, hd)
    for q in range(vb):
        tile = s_ref[q]
        for ti in range(seq):
            sa = jnp.sum(tile * a_ref[ti], axis=0, keepdims=True)
            tile = tile * w_ref[ti] + sa * b_ref[ti] + v_ref[ti, pl.ds(vs * vb + q, 1), :] * k_ref[ti]
            y_sc[ti, pl.ds(hrow + vs * vb + q, 1), :] = jnp.sum(tile * r_ref[ti], axis=0, keepdims=True)
        so_ref[q] = tile

    @pl.when(vs == pl.num_programs(1) - 1)
    def _():
        for ti in range(seq):
            rows = pl.ds(hrow, hd)
            y = y_sc[ti, rows, :]
            yc = y - jnp.mean(y, axis=0, keepdims=True)
            var = jnp.mean(yc * yc, axis=0, keepdims=True)
            yn = yc * lax.rsqrt(var + GN_EPS) * lnw_ref[...] + lnb_ref[...]
            bonus = jnp.sum(r_ref[ti] * k_ref[ti] * rk_ref[...], axis=0, keepdims=True) * v_ref[ti]
            y_sc[ti, rows, :] = (yn + bonus) * g_ref[ti]

    @pl.when((vs == pl.num_programs(1) - 1) & (h % 2 == 1))
    def _():
        for ti in range(seq):
            y_ref[ti * batch:(ti + 1) * batch, :] = y_sc[ti].T.astype(y_ref.dtype)


def _sample_wkv_call(vecs, state_t, new_states, layer, batch, seq, prm):
    n_heads, hd = state_t.shape[1:3]
    aw = n_heads * hd
    vb = SAMPLE_VROWS
    assert 2 * hd == LANES and hd % vb == 0
    vspec = pl.BlockSpec((seq, hd, batch), lambda h, s: (0, h, 0))
    cspec = pl.BlockSpec((hd, 1), lambda h, s: (h, 0))
    sspec = pl.BlockSpec((None, None, vb, hd, batch), lambda h, s: (layer, h, s, 0, 0))
    cols = [prm[n].reshape(aw, 1) for n in ("r_k", "ln_w", "ln_b")]
    args = [*vecs, *cols, state_t, new_states]
    return pl.pallas_call(
        _sample_wkv_kernel,
        grid=(n_heads, hd // vb),
        in_specs=[vspec] * 7 + [cspec] * 3 + [sspec, pl.BlockSpec(memory_space=pl.ANY)],
        out_specs=[pl.BlockSpec((seq * batch, LANES), lambda h, s: (0, h // 2)), sspec],
        out_shape=[jax.ShapeDtypeStruct((seq * batch, aw), MIX_DTYPE), jax.ShapeDtypeStruct(state_t.shape, F32)],
        scratch_shapes=[pltpu.VMEM((seq, LANES, batch), F32)],
        input_output_aliases={len(args) - 1: 1},
        compiler_params=_cparams(2),
        name="wkv_sample",
    )(*args)


def kernel(x_prompt, x_sample, state_wkv, state_shift, state_pool, c_prompt, c_sample, norm_g, w_mod, b_mod, w_ffn_in, w_ffn_out, w_in, mu_shift, w0, w2, a0, a2, g2, k_k, k_a, r_k, ln_x_w, ln_x_b, w_pool, pool_scale, w_br_a, w_br_b, w_gate, b_gate, w_out, final_g):
    bp, lp, d = x_prompt.shape
    bs, ls, _ = x_sample.shape
    depth = w_mod.shape[0]
    aw = w0.shape[1]
    a_proj = mu_shift.shape[1]
    bw = pool_scale.shape[1]

    bf = lambda w: w.astype(BF16)
    w_ffn_in_b, w_ffn_out_b = bf(w_ffn_in), bf(w_ffn_out)
    w_in_t = bf(jnp.transpose(w_in, (0, 2, 1)))
    w_in_a, w_in_b = w_in_t[:, :a_proj], w_in_t[:, a_proj:]
    w_gate_b, w_br_a_b, w_br_b_b, w_out_b = bf(w_gate), bf(w_br_a), bf(w_br_b), bf(w_out)
    w2_b, a2_b, g2_b, w_pool_b = bf(w2), bf(a2), bf(g2), bf(w_pool)
    head_of = jnp.arange(MXU_TILE, dtype=jnp.int32) // HEAD_DIM
    seg_ones = (head_of[:, None] == head_of[None, :]).astype(BF16)

    mod = _mod_call(jnp.concatenate([c_sample, c_prompt], axis=0), w_mod, b_mod)
    grp = _Groups(bp, lp, bs, ls, min(ROW_TILE, lp), mod_row0=bs)

    xp = x_prompt.reshape(bp * lp, d)
    xs = jnp.transpose(x_sample, (1, 0, 2)).reshape(ls * bs, d)
    hist_s = jnp.transpose(state_pool, (0, 2, 1, 3)).reshape(depth, POOL_HIST * bs, bw)
    state_t = jnp.transpose(state_wkv, (0, 2, 3, 4, 1))
    wkv_s = jnp.zeros_like(state_t)

    wkv_p, shift_p, pool_p, shift_s, pool_s = [], [], [], [], []
    for l in range(depth):
        prm = dict(mu=mu_shift[l][None], w0=w0[l][None], w2=w2_b[l], a0=a0[l][None], a2=a2_b[l], g2=g2_b[l],
                   k_k=k_k[l][None], k_a=k_a[l][None], r_k=r_k[l].reshape(1, aw), ln_w=ln_x_w[l][None],
                   ln_b=ln_x_b[l][None], w_pool=w_pool_b[l], pool_scale=pool_scale[l][None], seg=seg_ones)
        xp, xs = _ffn_call(xp, xs, grp, mod, l, 0, norm_g[l, 0], w_ffn_in_b, w_ffn_out_b)
        pa_p, pb_p, pa_s, pb_s = _mixin_call(xp, xs, grp, mod, l, norm_g[l, 1], w_in_a[l], w_in_b[l])
        ya_p, yb_p, s_new = _prompt_mix_call(pa_p, pb_p, bp, lp, prm)
        *vecs, yb_s = _sample_prep_call(pa_s, state_shift[l, :, 0, :], pb_s, hist_s[l], bs, ls, prm)
        ya_s, wkv_s = _sample_wkv_call(vecs, state_t, wkv_s, l, bs, ls, prm)
        xp, xs = _mixout_call(xp, xs, grp, mod, l, norm_g[l, 1], ya_p, yb_p, ya_s, yb_s, w_gate_b, b_gate, w_br_a_b,
                              w_br_b_b, w_out_b)
        xp, xs = _ffn_call(xp, xs, grp, mod, l, 2, norm_g[l, 2], w_ffn_in_b, w_ffn_out_b,
                           final_g if l == depth - 1 else None)
        wkv_p.append(s_new)
        shift_p.append(pa_p.reshape(bp, lp, a_proj)[:, -1:])
        pool_p.append(pb_p.reshape(bp, lp, bw)[:, -POOL_HIST:])
        shift_s.append(pa_s.reshape(ls, bs, a_proj)[-1][:, None, :])
        pb_bm = jnp.transpose(pb_s.reshape(ls, bs, bw), (1, 0, 2))
        pool_s.append(jnp.concatenate([state_pool[l], pb_bm], axis=1)[:, -POOL_HIST:])

    y_prompt = xp.reshape(bp, lp, d)
    y_sample = jnp.transpose(xs.reshape(ls, bs, d), (1, 0, 2))
    return (y_prompt, y_sample, jnp.stack(wkv_p), jnp.stack(shift_p), jnp.stack(pool_p),
            jnp.transpose(wkv_s, (0, 4, 1, 2, 3)), jnp.stack(shift_s), jnp.stack(pool_s))
```
